```python
import jax, jax.numpy as jnp
from jax import lax
import numpy as np

D_MODEL = 1024
BATCH = 1
SEQ = 16384
DEPTH = 4

N_A = DEPTH // 2
N_B = DEPTH - N_A
N_DENSE = (DEPTH + 1) // 2
N_MOE = DEPTH // 2
EPS = 1e-6
CHUNK = 128
A_WIDTH = 2 * D_MODEL
A_GROUPS = 8
N_HEADS = 16
N_KV = 4
HPG = N_HEADS // N_KV
HEAD_DIM = D_MODEL // N_HEADS
L_CMP = 32
D_CMP = 16
L_SEL = 64
N_SELECT = 16
WINDOW = 512
Q_BLK = 128
CMP_HIDDEN = 4 * HEAD_DIM
D_FF = 2816
N_EXPERTS = 8
TOP_K = 2
D_EXPERT = 7 * D_MODEL // 2
MOE_BLOCK = 256
NEG = -1e30
FORCE_BONUS = 1e4

kernel_name = 'hybrid_gmlp_nsa_yoco_block'


def rms_norm(x, g):
    x32 = x.astype(jnp.float32)
    y = x32 * lax.rsqrt(jnp.mean(x32 * x32, axis=-1, keepdims=True) + EPS)
    return y.astype(x.dtype) * g


def layer_norm(x, g, b):
    x32 = x.astype(jnp.float32)
    mu = jnp.mean(x32, axis=-1, keepdims=True)
    var = jnp.mean(jnp.square(x32 - mu), axis=-1, keepdims=True)
    return ((x32 - mu) * lax.rsqrt(var + EPS)).astype(x.dtype) * g + b


def ada_mod(c_act, w, b, n):
    m = (c_act @ w + b)[:, None, :]
    return jnp.split(m, n, axis=-1)


def gmlp_mix(h, w_in, ln_g, ln_b, ws, bs, w_out):
    B_, S_, _ = h.shape
    z = jax.nn.gelu(h @ w_in)
    u, v = jnp.split(z, 2, axis=-1)
    v = layer_norm(v, ln_g, ln_b)
    causal = jnp.tril(jnp.ones((CHUNK, CHUNK), dtype=bool))
    wsm = jnp.where(causal[None], ws, jnp.zeros_like(ws))
    vc = v.reshape(B_, S_ // CHUNK, CHUNK, A_GROUPS, A_WIDTH // A_GROUPS)
    mixed = jnp.einsum('gts,bcsgd->bctgd', wsm, vc) + bs.T[None, None, :, :, None]
    return (u * mixed.reshape(B_, S_, A_WIDTH)) @ w_out


def nsa_shared_kv(x, c_act, kv_norm_g, kv_ada_w, kv_ada_b, w_kv, cmp_pos, cmp_w1, cmp_b1, cmp_w2, cmp_b2):
    B_, S_, _ = x.shape
    shift, scale = ada_mod(c_act, kv_ada_w, kv_ada_b, 2)
    h = rms_norm(x, kv_norm_g) * (1.0 + scale) + shift
    kv = (h @ w_kv).reshape(B_, S_, 6, N_KV, HEAD_DIM)
    kc_raw, vc_raw, ks, vs, kw, vw = [kv[:, :, i] for i in range(6)]
    n_cmp = (S_ - L_CMP) // D_CMP + 1
    idx = jnp.arange(n_cmp)[:, None] * D_CMP + jnp.arange(L_CMP)[None, :]

    def compress(raw, j):
        blk = raw[:, idx] + cmp_pos[j][None, None, :, None, :]
        blk = jnp.moveaxis(blk, 3, 2).reshape(B_, n_cmp, N_KV, L_CMP * HEAD_DIM)
        return jax.nn.gelu(blk @ cmp_w1[j] + cmp_b1[j]) @ cmp_w2[j] + cmp_b2[j]

    k_cmp = compress(kc_raw, 0)
    v_cmp = compress(vc_raw, 1)
    n_sel = S_ // L_SEL

    def to_blocks(t):
        return jnp.moveaxis(t, 2, 1).reshape(B_, N_KV, n_sel, L_SEL, HEAD_DIM)

    pad = ((0, 0), (WINDOW, 0), (0, 0), (0, 0))
    return (k_cmp, v_cmp, to_blocks(ks), to_blocks(vs), jnp.pad(kw, pad), jnp.pad(vw, pad))


def nsa_mix(h, w_in, w_out, k_cmp, v_cmp, ks_blk, vs_blk, kw_pad, vw_pad):
    B_, S_, _ = h.shape
    n_cmp = k_cmp.shape[1]
    n_sel = ks_blk.shape[2]
    k_sel = min(N_SELECT, n_sel)
    n_qb = S_ // Q_BLK
    proj = h @ w_in
    q = proj[..., :N_HEADS * HEAD_DIM].reshape(B_, S_, N_KV, HPG, HEAD_DIM) * (HEAD_DIM ** -0.5)
    gates = jax.nn.sigmoid(proj[..., N_HEADS * HEAD_DIM:]).reshape(B_, S_, N_KV, HPG, 3)
    ci = jnp.arange(n_cmp)[:, None] * D_CMP
    sj = jnp.arange(n_sel)[None, :] * L_SEL
    overlap = ((ci < sj + L_SEL) & (ci + L_CMP > sj)).astype(jnp.float32)
    cmp_end = jnp.arange(n_cmp) * D_CMP + L_CMP - 1
    sel_ids = jnp.arange(n_sel)
    b_ix = jnp.arange(B_)[:, None, None, None]
    g_ix = jnp.arange(N_KV)[None, :, None, None]

    def block(args):
        qb, q_blk, g_blk = args
        t = qb * Q_BLK + jnp.arange(Q_BLK)
        s = jnp.einsum('btghd,bngd->bghtn', q_blk, k_cmp).astype(jnp.float32)
        valid = cmp_end[None, :] <= t[:, None]
        p_cmp = jnp.where(valid, jax.nn.softmax(jnp.where(valid, s, NEG), axis=-1), 0.0)
        o_cmp = jnp.einsum('bghtn,bngd->btghd', p_cmp.astype(v_cmp.dtype), v_cmp)
        imp = jnp.einsum('bghtn,nj->bgtj', p_cmp, overlap)
        jt = t // L_SEL
        forced = (sel_ids[None, :] == 0) | (sel_ids[None, :] == jt[:, None]) | (sel_ids[None, :] == jt[:, None] - 1)
        imp = jnp.where(sel_ids[None, :] * L_SEL <= t[:, None], imp + jnp.where(forced, FORCE_BONUS, 0.0), NEG)
        _, top_idx = lax.top_k(imp, k_sel)
        k_g = ks_blk[b_ix, g_ix, top_idx].reshape(B_, N_KV, Q_BLK, k_sel * L_SEL, HEAD_DIM)
        v_g = vs_blk[b_ix, g_ix, top_idx].reshape(B_, N_KV, Q_BLK, k_sel * L_SEL, HEAD_DIM)
        kpos = (top_idx[..., None] * L_SEL + jnp.arange(L_SEL)).reshape(B_, N_KV, Q_BLK, k_sel * L_SEL)
        s = jnp.einsum('btghd,bgtkd->bghtk', q_blk, k_g).astype(jnp.float32)
        valid = (kpos <= t[None, None, :, None])[:, :, None]
        p = jax.nn.softmax(jnp.where(valid, s, NEG), axis=-1)
        o_slc = jnp.einsum('bghtk,bgtkd->btghd', p.astype(v_g.dtype), v_g)
        kw = lax.dynamic_slice_in_dim(kw_pad, qb * Q_BLK, WINDOW + Q_BLK, axis=1)
        vw = lax.dynamic_slice_in_dim(vw_pad, qb * Q_BLK, WINDOW + Q_BLK, axis=1)
        wpos = qb * Q_BLK - WINDOW + jnp.arange(WINDOW + Q_BLK)
        valid = (wpos[None, :] <= t[:, None]) & (wpos[None, :] > t[:, None] - WINDOW) & (wpos[None, :] >= 0)
        s = jnp.einsum('btghd,bsgd->bghts', q_blk, kw).astype(jnp.float32)
        p = jax.nn.softmax(jnp.where(valid, s, NEG), axis=-1)
        o_win = jnp.einsum('bghts,bsgd->btghd', p.astype(vw.dtype), vw)
        return o_cmp * g_blk[..., 0:1] + o_slc * g_blk[..., 1:2] + o_win * g_blk[..., 2:3]

    q_blocks = jnp.moveaxis(q.reshape(B_, n_qb, Q_BLK, N_KV, HPG, HEAD_DIM), 1, 0)
    g_blocks = jnp.moveaxis(gates.reshape(B_, n_qb, Q_BLK, N_KV, HPG, 3), 1, 0)
    o = lax.map(block, (jnp.arange(n_qb), q_blocks, g_blocks))
    o = jnp.moveaxis(o, 0, 1).reshape(B_, S_, N_HEADS * HEAD_DIM)
    return o @ w_out


def swiglu(h, w_gate, w_up, w_down):
    return (jax.nn.silu(h @ w_gate) * (h @ w_up)) @ w_down


def moe_swiglu(h, w_r, b_r, w_g, w_u, w_d):
    B_, S_, D_ = h.shape
    xt = h.reshape(-1, D_)
    T = xt.shape[0]
    n_assign = T * TOP_K
    logits = (xt @ w_r + b_r).astype(jnp.float32)
    top_v, top_e = lax.top_k(logits, TOP_K)
    gates = jax.nn.softmax(top_v, axis=-1).astype(h.dtype)
    e_flat = top_e.reshape(-1)
    g_flat = gates.reshape(-1)
    tok_flat = jnp.arange(n_assign) // TOP_K
    order = jnp.argsort(e_flat)
    se, stok, sg = e_flat[order], tok_flat[order], g_flat[order]
    counts = jnp.bincount(e_flat, length=N_EXPERTS)
    padded = (counts + MOE_BLOCK - 1) // MOE_BLOCK * MOE_BLOCK
    pend = jnp.cumsum(padded)
    pstart = pend - padded
    ustart = jnp.cumsum(counts) - counts
    dest = pstart[se] + jnp.arange(n_assign) - ustart[se]
    n_rows = -(-n_assign // MOE_BLOCK) * MOE_BLOCK + N_EXPERTS * MOE_BLOCK
    n_blk = n_rows // MOE_BLOCK
    x_buf = jnp.zeros((n_rows, D_), h.dtype).at[dest].set(xt[stok])
    blk_e = jnp.minimum(jnp.sum(jnp.arange(n_blk)[:, None] * MOE_BLOCK >= pend[None, :], axis=1), N_EXPERTS - 1)

    def expert_block(args):
        xb, e = args
        return swiglu(xb, w_g[e], w_u[e], w_d[e])

    y_buf = lax.map(expert_block, (x_buf.reshape(n_blk, MOE_BLOCK, D_), blk_e)).reshape(n_rows, D_)
    y = jax.ops.segment_sum(y_buf[dest] * sg[:, None], stok, num_segments=T)
    return y.reshape(B_, S_, D_)


def setup_inputs(seed: int = 0) -> dict:
    key = jax.random.key(seed)
    keys = iter(jax.random.split(key, 40))
    D = D_MODEL

    def nrm(shape, s):
        return jax.random.normal(next(keys), shape, jnp.float32) * s

    return {
        'x': nrm((BATCH, SEQ, D), 1.0),
        'c': nrm((BATCH, D), 1.0),
        'ada_w': nrm((DEPTH, 2, D, 3 * D), 0.5 * D ** -0.5),
        'ada_b': nrm((DEPTH, 2, 3 * D), 0.02),
        'norm_pre_g': 1.0 + nrm((DEPTH, 2, D), 0.02),
        'norm_post_g': 1.0 + nrm((DEPTH, 2, D), 0.02),
        'a_w_in': nrm((N_A, D, 2 * A_WIDTH), D ** -0.5),
        'a_ln_g': 1.0 + nrm((N_A, A_WIDTH), 0.02),
        'a_ln_b': nrm((N_A, A_WIDTH), 0.02),
        'a_ws': nrm((N_A, A_GROUPS, CHUNK, CHUNK), CHUNK ** -0.5),
        'a_bs': 1.0 + nrm((N_A, A_GROUPS, CHUNK), 0.02),
        'a_w_out': nrm((N_A, A_WIDTH, D), A_WIDTH ** -0.5),
        'kv_norm_g': 1.0 + nrm((D,), 0.02),
        'kv_ada_w': nrm((D, 2 * D), 0.5 * D ** -0.5),
        'kv_ada_b': nrm((2 * D,), 0.02),
        'w_kv': nrm((D, 6 * N_KV * HEAD_DIM), D ** -0.5),
        'cmp_pos': nrm((2, L_CMP, HEAD_DIM), 0.1),
        'cmp_w1': nrm((2, L_CMP * HEAD_DIM, CMP_HIDDEN), (L_CMP * HEAD_DIM) ** -0.5),
        'cmp_b1': nrm((2, CMP_HIDDEN), 0.02),
        'cmp_w2': nrm((2, CMP_HIDDEN, HEAD_DIM), CMP_HIDDEN ** -0.5),
        'cmp_b2': nrm((2, HEAD_DIM), 0.02),
        'b_w_in': nrm((N_B, D, N_HEADS * HEAD_DIM + 3 * N_HEADS), D ** -0.5),
        'b_w_out': nrm((N_B, N_HEADS * HEAD_DIM, D), (N_HEADS * HEAD_DIM) ** -0.5),
        'ffn_w_gate': nrm((N_DENSE, D, D_FF), D ** -0.5),
        'ffn_w_up': nrm((N_DENSE, D, D_FF), D ** -0.5),
        'ffn_w_down': nrm((N_DENSE, D_FF, D), D_FF ** -0.5),
        'moe_router': nrm((N_MOE, D, N_EXPERTS), D ** -0.5),
        'moe_router_b': nrm((N_MOE, N_EXPERTS), 0.01),
        'moe_w_gate': nrm((N_MOE, N_EXPERTS, D, D_EXPERT), D ** -0.5),
        'moe_w_up': nrm((N_MOE, N_EXPERTS, D, D_EXPERT), D ** -0.5),
        'moe_w_down': nrm((N_MOE, N_EXPERTS, D_EXPERT, D), D_EXPERT ** -0.5),
    }


def reference(x, c, ada_w, ada_b, norm_pre_g, norm_post_g, a_w_in, a_ln_g, a_ln_b, a_ws, a_bs, a_w_out, kv_norm_g, kv_ada_w, kv_ada_b, w_kv, cmp_pos, cmp_w1, cmp_b1, cmp_w2, cmp_b2, b_w_in, b_w_out, ffn_w_gate, ffn_w_up, ffn_w_down, moe_router, moe_router_b, moe_w_gate, moe_w_up, moe_w_down):
    c_act = jax.nn.silu(c)
    shared_kv = None
    for layer in range(DEPTH):
        shift, scale, gate = ada_mod(c_act, ada_w[layer, 0], ada_b[layer, 0], 3)
        h = rms_norm(x, norm_pre_g[layer, 0]) * (1.0 + scale) + shift
        if layer < N_A:
            y = gmlp_mix(h, a_w_in[layer], a_ln_g[layer], a_ln_b[layer], a_ws[layer], a_bs[layer], a_w_out[layer])
        else:
            if layer == N_A:
                shared_kv = nsa_shared_kv(x, c_act, kv_norm_g, kv_ada_w, kv_ada_b, w_kv, cmp_pos, cmp_w1, cmp_b1, cmp_w2, cmp_b2)
            i = layer - N_A
            y = nsa_mix(h, b_w_in[i], b_w_out[i], *shared_kv)
        x = x + gate * rms_norm(y, norm_post_g[layer, 0])
        shift, scale, gate = ada_mod(c_act, ada_w[layer, 1], ada_b[layer, 1], 3)
        h = rms_norm(x, norm_pre_g[layer, 1]) * (1.0 + scale) + shift
        j = layer // 2
        if layer % 2 == 0:
            y = swiglu(h, ffn_w_gate[j], ffn_w_up[j], ffn_w_down[j])
        else:
            y = moe_swiglu(h, moe_router[j], moe_router_b[j], moe_w_gate[j], moe_w_up[j], moe_w_down[j])
        x = x + gate * rms_norm(y, norm_post_g[layer, 1])
    return x
```

```python
import functools

import jax
import jax.numpy as jnp
from jax import lax
from jax.experimental import pallas as pl
from jax.experimental.pallas import tpu as pltpu

F32 = jnp.float32
BF16 = jnp.bfloat16
I32 = jnp.int32

EPS = 1e-6
NEG = -1e30
FORCE_BONUS = 1e4

LANES = 128
VMEM_LIMIT_BYTES = 56 * 1024 * 1024

CHUNK = 128
A_GROUPS = 8
N_HEADS = 16
N_KV = 4
HPG = N_HEADS // N_KV
HEAD_DIM = 64
L_CMP = 32
D_CMP = 16
L_SEL = 64
SEL_SHIFT = 6
N_SELECT = 16
WINDOW = 512
Q_BLK = 128
N_EXPERTS = 8
MOE_BLOCK = 256

ROW_TILE = 256
TOK_TILE = 256
SEL_CHUNK = 512
SEL_BLKS = SEL_CHUNK // L_SEL
V_ROWS = 72
F_CHUNK = 512
COMBINE_SLOTS = 2 * N_EXPERTS


def _params(*sem):
    return pltpu.CompilerParams(dimension_semantics=sem, vmem_limit_bytes=VMEM_LIMIT_BYTES)


def _resident(shape, index_map):
    return pl.BlockSpec(shape, index_map, pipeline_mode=pl.Buffered(1))


def _split_bf16(a):
    hi = a.astype(BF16)
    lo = (a - hi.astype(F32)).astype(BF16)
    return hi, lo


def _dot(a, b):
    return jnp.dot(a, b, preferred_element_type=F32)


def _dot_nt(a, b):
    return lax.dot_general(a, b, (((1,), (1,)), ((), ())), preferred_element_type=F32)


def _dot_tn(a, b):
    return lax.dot_general(a, b, (((0,), (0,)), ((), ())), preferred_element_type=F32)


def _prenorm(x, g, scale, shift):
    ms = jnp.mean(x * x, axis=-1, keepdims=True)
    return (x * lax.rsqrt(ms + EPS) * g) * (1.0 + scale) + shift


def _post_residual(x, y, g, gate):
    ms = jnp.mean(y * y, axis=-1, keepdims=True)
    return x + gate * (y * lax.rsqrt(ms + EPS) * g)


def _ada_kernel(c_ref, w_ref, b_ref, o_ref):
    c = c_ref[...]
    c_act = jnp.broadcast_to(c * jax.nn.sigmoid(c), (8, c.shape[1]))
    c_hi, c_lo = _split_bf16(c_act)
    w_hi, w_lo = _split_bf16(w_ref[...])
    m = _dot(c_hi, w_hi) + (_dot(c_hi, w_lo) + _dot(c_lo, w_hi))
    o_ref[...] = m[0:1] + b_ref[...]


def _ada(c, w, b):
    n, d, nn = w.shape
    out = pl.pallas_call(
        _ada_kernel,
        grid=(n, nn // d),
        in_specs=[pl.BlockSpec((1, d), lambda i, j: (0, 0)),
                  pl.BlockSpec((None, d, d), lambda i, j: (i, 0, j)),
                  pl.BlockSpec((None, 1, d), lambda i, j: (i, 0, j))],
        out_specs=pl.BlockSpec((None, 1, d), lambda i, j: (i, 0, j)),
        out_shape=jax.ShapeDtypeStruct((n, 1, nn), F32),
        compiler_params=_params("arbitrary", "arbitrary"),
        name="ada",
    )(c, w, b.reshape(n, 1, nn))
    return out.reshape(n, nn)


def _gmlp_kernel(x_ref, vec_ref, win_ref, ln_ref, ws_ref, bst_ref, wout_ref, o_ref, gated_ref):
    x = x_ref[...]
    vec = vec_ref[...]
    h = _prenorm(x, vec[0:1], vec[1:2], vec[2:3]).astype(BF16)
    z = jax.nn.gelu(_dot(h, win_ref[...]))
    width = z.shape[1] // 2
    u = z[:, :width]
    v = z[:, width:]
    mu = jnp.mean(v, axis=-1, keepdims=True)
    vc = v - mu
    var = jnp.mean(vc * vc, axis=-1, keepdims=True)
    ln = ln_ref[...]
    vn = (vc * lax.rsqrt(var + EPS) * ln[0:1] + ln[1:2]).astype(BF16)
    causal = (lax.broadcasted_iota(I32, (CHUNK, CHUNK), 0)
              >= lax.broadcasted_iota(I32, (CHUNK, CHUNK), 1))
    gw = width // A_GROUPS
    bst = bst_ref[...]
    for g in range(A_GROUPS):
        wg = jnp.where(causal, ws_ref[g], 0.0).astype(BF16)
        for ck in range(x.shape[0] // CHUNK):
            rows = slice(ck * CHUNK, (ck + 1) * CHUNK)
            cols = slice(g * gw, (g + 1) * gw)
            mixed = _dot(wg, vn[rows, cols]) + bst[:, g:g + 1]
            gated_ref[rows, cols] = (u[rows, cols] * mixed).astype(BF16)
    y = _dot(gated_ref[...], wout_ref[...])
    o_ref[...] = _post_residual(x, y, vec[3:4], vec[4:5])


def _gmlp_layer(x, vec, w_in, ln, ws, bst, w_out):
    s, d = x.shape
    e2 = w_in.shape[1]
    return pl.pallas_call(
        _gmlp_kernel,
        grid=(s // ROW_TILE,),
        in_specs=[pl.BlockSpec((ROW_TILE, d), lambda i: (i, 0)),
                  _resident((8, d), lambda i: (0, 0)),
                  _resident((d, e2), lambda i: (0, 0)),
                  _resident((2, e2 // 2), lambda i: (0, 0)),
                  _resident(ws.shape, lambda i: (0, 0, 0)),
                  _resident(bst.shape, lambda i: (0, 0)),
                  _resident((e2 // 2, d), lambda i: (0, 0))],
        out_specs=pl.BlockSpec((ROW_TILE, d), lambda i: (i, 0)),
        out_shape=jax.ShapeDtypeStruct((s, d), F32),
        scratch_shapes=[pltpu.VMEM((ROW_TILE, e2 // 2), BF16)],
        compiler_params=_params("arbitrary"),
        name="gmlp",
    )(x, vec, w_in, ln, ws, bst, w_out)


def _swiglu_kernel(x_ref, vec_ref, wg_ref, wu_ref, wd_ref, o_ref):
    x = x_ref[...]
    vec = vec_ref[...]
    h = _prenorm(x, vec[0:1], vec[1:2], vec[2:3]).astype(BF16)
    g = _dot(h, wg_ref[...])
    a = (g * jax.nn.sigmoid(g) * _dot(h, wu_ref[...])).astype(BF16)
    y = _dot(a, wd_ref[...])
    o_ref[...] = _post_residual(x, y, vec[3:4], vec[4:5])


def _swiglu_layer(x, vec, w_gate, w_up, w_down):
    s, d = x.shape
    f = w_gate.shape[1]
    return pl.pallas_call(
        _swiglu_kernel,
        grid=(s // ROW_TILE,),
        in_specs=[pl.BlockSpec((ROW_TILE, d), lambda i: (i, 0)),
                  _resident((8, d), lambda i: (0, 0)),
                  _resident((d, f), lambda i: (0, 0)),
                  _resident((d, f), lambda i: (0, 0)),
                  _resident((f, d), lambda i: (0, 0))],
        out_specs=pl.BlockSpec((ROW_TILE, d), lambda i: (i, 0)),
        out_shape=jax.ShapeDtypeStruct((s, d), F32),
        compiler_params=_params("arbitrary"),
        name="swiglu",
    )(x, vec, w_gate, w_up, w_down)


def _router_kernel(x_ref, vec_ref, wrt_ref, br_ref, h_ref, idx_ref, gate_ref):
    x = x_ref[...]
    vec = vec_ref[...]
    h = _prenorm(x, vec[0:1], vec[1:2], vec[2:3])
    h_ref[...] = h.astype(BF16)
    h_hi, h_lo = _split_bf16(h)
    w_hi, w_lo = _split_bf16(wrt_ref[...])
    logit = _dot_nt(w_hi, h_hi) + (_dot_nt(w_hi, h_lo) + _dot_nt(w_lo, h_hi)) + br_ref[...]
    ne = logit.shape[0]
    eidx = lax.broadcasted_iota(I32, logit.shape, 0)
    m1 = jnp.max(logit, axis=0, keepdims=True)
    i1 = jnp.min(jnp.where(logit == m1, eidx, ne), axis=0, keepdims=True)
    rest = jnp.where(eidx == i1, -jnp.inf, logit)
    m2 = jnp.max(rest, axis=0, keepdims=True)
    i2 = jnp.min(jnp.where(rest == m2, eidx, ne), axis=0, keepdims=True)
    e21 = jnp.exp(m2 - m1)
    g1 = 1.0 / (1.0 + e21)
    g2 = e21 * g1
    pad_i = jnp.zeros((6, i1.shape[1]), I32)
    pad_f = jnp.zeros((6, i1.shape[1]), F32)
    idx_ref[...] = jnp.concatenate([i1, i2, pad_i], axis=0)
    gate_ref[...] = jnp.concatenate([g1, g2, pad_f], axis=0)


def _router(x, vec, w_rt, b_r):
    s, d = x.shape
    ne = w_rt.shape[0]
    return pl.pallas_call(
        _router_kernel,
        grid=(s // ROW_TILE,),
        in_specs=[pl.BlockSpec((ROW_TILE, d), lambda i: (i, 0)),
                  _resident((8, d), lambda i: (0, 0)),
                  _resident((ne, d), lambda i: (0, 0)),
                  _resident((ne, 1), lambda i: (0, 0))],
        out_specs=[pl.BlockSpec((ROW_TILE, d), lambda i: (i, 0)),
                   pl.BlockSpec((8, ROW_TILE), lambda i: (0, i)),
                   pl.BlockSpec((8, ROW_TILE), lambda i: (0, i))],
        out_shape=[jax.ShapeDtypeStruct((s, d), BF16),
                   jax.ShapeDtypeStruct((8, s), I32),
                   jax.ShapeDtypeStruct((8, s), F32)],
        compiler_params=_params("arbitrary"),
        name="moe_router",
    )(x, vec, w_rt, b_r)


def _rank_kernel(idx_ref, rank_ref, start_ref, count_ref, carry_ref):
    @pl.when(pl.program_id(0) == 0)
    def _():
        carry_ref[...] = jnp.zeros_like(carry_ref)

    idx = idx_ref[...]
    tt = idx.shape[1]
    eidx = lax.broadcasted_iota(I32, (N_EXPERTS, tt), 0)
    hit1 = eidx == idx[0:1]
    hit2 = eidx == idx[1:2]
    member = jnp.where(hit1 | hit2, 1.0, 0.0)
    before = (lax.broadcasted_iota(I32, (tt, tt), 0) < lax.broadcasted_iota(I32, (tt, tt), 1))
    carry = carry_ref[...]
    cum = _dot(member.astype(BF16), jnp.where(before, 1.0, 0.0).astype(BF16)) + carry[:, 0:1]
    r1 = jnp.sum(jnp.where(hit1, cum, 0.0), axis=0, keepdims=True)
    r2 = jnp.sum(jnp.where(hit2, cum, 0.0), axis=0, keepdims=True)
    rank_ref[...] = jnp.concatenate([r1, r2, jnp.zeros((6, tt), F32)], axis=0).astype(I32)
    tile_count = jnp.broadcast_to(jnp.sum(member, axis=1, keepdims=True), carry.shape)
    start_ref[...] = carry
    count_ref[...] = tile_count
    carry_ref[...] = carry + tile_count


def _ranks(idx):
    s = idx.shape[1]
    nt = s // TOK_TILE
    return pl.pallas_call(
        _rank_kernel,
        grid=(nt,),
        in_specs=[pl.BlockSpec((8, TOK_TILE), lambda i: (0, i))],
        out_specs=[pl.BlockSpec((8, TOK_TILE), lambda i: (0, i)),
                   pl.BlockSpec((None, N_EXPERTS, LANES), lambda i: (i, 0, 0)),
                   pl.BlockSpec((None, N_EXPERTS, LANES), lambda i: (i, 0, 0))],
        out_shape=[jax.ShapeDtypeStruct((8, s), I32),
                   jax.ShapeDtypeStruct((nt, N_EXPERTS, LANES), F32),
                   jax.ShapeDtypeStruct((nt, N_EXPERTS, LANES), F32)],
        scratch_shapes=[pltpu.VMEM((N_EXPERTS, LANES), F32)],
        compiler_params=_params("arbitrary"),
        name="moe_ranks",
    )(idx)


def _match(idx, rank, expert, rows):
    r1 = jnp.where(idx[0:1] == expert, rank[0:1], -1)
    r2 = jnp.where(idx[1:2] == expert, rank[1:2], -1)
    return rows == r1, rows == r2


def _dispatch_kernel(be_ref, lb_ref, tlo_ref, thi_ref, idx_ref, rank_ref, h_ref, o_ref, acc_ref):
    b = pl.program_id(0)
    expert = be_ref[b]
    rows = lax.broadcasted_iota(I32, (MOE_BLOCK, TOK_TILE), 0) + lb_ref[b]
    acc_ref[...] = jnp.zeros_like(acc_ref)

    def body(t, carry):
        off = pl.multiple_of(t * TOK_TILE, TOK_TILE)
        m1, m2 = _match(idx_ref[:, pl.ds(off, TOK_TILE)], rank_ref[:, pl.ds(off, TOK_TILE)], expert, rows)
        onehot = jnp.where(m1 | m2, 1.0, 0.0).astype(BF16)
        acc_ref[...] += _dot(onehot, h_ref[pl.ds(off, TOK_TILE), :])
        return carry

    lax.fori_loop(tlo_ref[b], thi_ref[b], body, 0)
    o_ref[...] = acc_ref[...].astype(BF16)


def _dispatch(blk_e, blk_lb, blk_tlo, blk_thi, idx, rank, h):
    s, d = h.shape
    n_blk = blk_e.shape[0]
    grid_spec = pltpu.PrefetchScalarGridSpec(
        num_scalar_prefetch=4,
        grid=(n_blk,),
        in_specs=[_resident((8, s), lambda b, *_: (0, 0)),
                  _resident((8, s), lambda b, *_: (0, 0)),
                  _resident((s, d), lambda b, *_: (0, 0))],
        out_specs=pl.BlockSpec((MOE_BLOCK, d), lambda b, *_: (b, 0)),
        scratch_shapes=[pltpu.VMEM((MOE_BLOCK, d), F32)],
    )
    return pl.pallas_call(
        _dispatch_kernel,
        grid_spec=grid_spec,
        out_shape=jax.ShapeDtypeStruct((n_blk * MOE_BLOCK, d), BF16),
        compiler_params=_params("arbitrary"),
        name="moe_dispatch",
    )(blk_e, blk_lb, blk_tlo, blk_thi, idx, rank, h)


def _expert_kernel(be_ref, used_ref, x_ref, wg_ref, wu_ref, wd_ref, o_ref):
    b = pl.program_id(0)

    @pl.when(b < used_ref[0])
    def _():
        x = x_ref[...]
        f = wg_ref.shape[1]
        y = jnp.zeros(o_ref.shape, F32)
        for c in range(f // F_CHUNK):
            cols = slice(c * F_CHUNK, (c + 1) * F_CHUNK)
            g = _dot(x, wg_ref[:, cols])
            a = (g * jax.nn.sigmoid(g) * _dot(x, wu_ref[:, cols])).astype(BF16)
            y = y + _dot(a, wd_ref[cols, :])
        o_ref[...] = y.astype(BF16)

    @pl.when(b >= used_ref[0])
    def _():
        o_ref[...] = jnp.zeros_like(o_ref)


def _experts(blk_e, n_used, x_buf, w_gate, w_up, w_down):
    n_rows, d = x_buf.shape
    f = w_gate.shape[2]
    grid_spec = pltpu.PrefetchScalarGridSpec(
        num_scalar_prefetch=2,
        grid=(n_rows // MOE_BLOCK,),
        in_specs=[pl.BlockSpec((MOE_BLOCK, d), lambda b, be, nu: (b, 0)),
                  _resident((None, d, f), lambda b, be, nu: (be[b], 0, 0)),
                  _resident((None, d, f), lambda b, be, nu: (be[b], 0, 0)),
                  _resident((None, f, d), lambda b, be, nu: (be[b], 0, 0))],
        out_specs=pl.BlockSpec((MOE_BLOCK, d), lambda b, be, nu: (b, 0)),
    )
    return pl.pallas_call(
        _expert_kernel,
        grid_spec=grid_spec,
        out_shape=jax.ShapeDtypeStruct((n_rows, d), BF16),
        compiler_params=_params("arbitrary"),
        name="moe_experts",
    )(blk_e, n_used, x_buf, w_gate, w_up, w_down)


def _combine_kernel(sb_ref, se_ref, slb_ref, sv_ref, idx_ref, rank_ref, gate_ref, y_ref, x_ref, vec_ref,
                    o_ref, acc_ref):
    t = pl.program_id(0)
    j = pl.program_id(1)
    slot = t * COMBINE_SLOTS + j

    @pl.when(j == 0)
    def _():
        acc_ref[...] = jnp.zeros_like(acc_ref)

    @pl.when(sv_ref[slot] > 0)
    def _():
        rows = lax.broadcasted_iota(I32, (MOE_BLOCK, TOK_TILE), 0) + slb_ref[slot]
        m1, m2 = _match(idx_ref[...], rank_ref[...], se_ref[slot], rows)
        gates = gate_ref[...]
        row_gate = jnp.sum(jnp.where(m1, gates[0:1], 0.0) + jnp.where(m2, gates[1:2], 0.0),
                           axis=1, keepdims=True)
        onehot_t = jnp.where(m1 | m2, 1.0, 0.0).T.astype(BF16)
        scaled = (y_ref[...].astype(F32) * row_gate).astype(BF16)
        acc_ref[...] += _dot(onehot_t, scaled)

    @pl.when(j == COMBINE_SLOTS - 1)
    def _():
        vec = vec_ref[...]
        o_ref[...] = _post_residual(x_ref[...], acc_ref[...], vec[3:4], vec[4:5])


def _combine(slot_blk, slot_e, slot_lb, slot_valid, idx, rank, gates, y_buf, x, vec):
    s, d = x.shape
    grid_spec = pltpu.PrefetchScalarGridSpec(
        num_scalar_prefetch=4,
        grid=(s // TOK_TILE, COMBINE_SLOTS),
        in_specs=[pl.BlockSpec((8, TOK_TILE), lambda t, j, *_: (0, t)),
                  pl.BlockSpec((8, TOK_TILE), lambda t, j, *_: (0, t)),
                  pl.BlockSpec((8, TOK_TILE), lambda t, j, *_: (0, t)),
                  pl.BlockSpec((MOE_BLOCK, d), lambda t, j, sb, *_: (sb[t * COMBINE_SLOTS + j], 0)),
                  pl.BlockSpec((TOK_TILE, d), lambda t, j, *_: (t, 0)),
                  _resident((8, d), lambda t, j, *_: (0, 0))],
        out_specs=pl.BlockSpec((TOK_TILE, d), lambda t, j, *_: (t, 0)),
        scratch_shapes=[pltpu.VMEM((TOK_TILE, d), F32)],
    )
    return pl.pallas_call(
        _combine_kernel,
        grid_spec=grid_spec,
        out_shape=jax.ShapeDtypeStruct((s, d), F32),
        compiler_params=_params("arbitrary", "arbitrary"),
        name="moe_combine",
    )(slot_blk, slot_e, slot_lb, slot_valid, idx, rank, gates, y_buf, x, vec)


def _moe_layer(x, vec, w_rt, b_r, w_gate, w_up, w_down):
    s, d = x.shape
    nt = s // TOK_TILE
    h, idx, gates = _router(x, vec, w_rt, b_r)
    rank, tile_start, tile_count = _ranks(idx)

    tile_start = tile_start[:, :, 0].astype(I32)
    tile_count = tile_count[:, :, 0].astype(I32)
    tile_end = tile_start + tile_count
    counts = tile_end[-1]
    padded = (counts + MOE_BLOCK - 1) // MOE_BLOCK * MOE_BLOCK
    pend = jnp.cumsum(padded)
    pstart = pend - padded
    n_rows = -(-(2 * s) // MOE_BLOCK) * MOE_BLOCK + N_EXPERTS * MOE_BLOCK
    n_blk = n_rows // MOE_BLOCK
    blk_row = jnp.arange(n_blk, dtype=I32) * MOE_BLOCK
    blk_e = jnp.minimum(jnp.sum(blk_row[:, None] >= pend[None, :], axis=1), N_EXPERTS - 1).astype(I32)
    blk_lb = blk_row - pstart[blk_e]
    te = tile_end[:, blk_e]
    ts = tile_start[:, blk_e]
    blk_tlo = jnp.sum(te <= blk_lb[None, :], axis=0).astype(I32)
    blk_thi = jnp.sum(ts < (blk_lb + MOE_BLOCK)[None, :], axis=0).astype(I32)
    n_used = (pend[-1] // MOE_BLOCK).astype(I32).reshape(1)

    x_buf = _dispatch(blk_e, blk_lb.astype(I32), blk_tlo, blk_thi, idx, rank, h)
    y_buf = _experts(blk_e, n_used, x_buf, w_gate, w_up, w_down)

    first = (pstart[None, :] + tile_start) // MOE_BLOCK
    last = (pstart[None, :] + jnp.maximum(tile_end - 1, tile_start)) // MOE_BLOCK
    has = tile_count > 0
    slot_blk = jnp.stack([first, last], axis=2).reshape(nt, COMBINE_SLOTS)
    slot_valid = jnp.stack([has, has & (last != first)], axis=2).reshape(nt, COMBINE_SLOTS)
    slot_e = jnp.broadcast_to(jnp.repeat(jnp.arange(N_EXPERTS, dtype=I32), 2)[None, :], (nt, COMBINE_SLOTS))
    slot_lb = slot_blk * MOE_BLOCK - pstart[slot_e]
    flat_valid = slot_valid.reshape(-1)
    pos = jnp.arange(flat_valid.shape[0], dtype=I32)
    src = lax.cummax(jnp.where(flat_valid, pos, 0))
    slot_blk = slot_blk.reshape(-1)[src]
    return _combine(slot_blk.astype(I32), slot_e.reshape(-1), slot_lb.reshape(-1).astype(I32),
                    flat_valid.astype(I32), idx, rank, gates, y_buf, x, vec)


def _kvproj_kernel(x_ref, vec_ref, wkv_ref, wvt_ref, raw_ref, ks_ref, kw_ref, vst_ref, vwt_ref):
    x = x_ref[...]
    vec = vec_ref[...]
    h = _prenorm(x, vec[0:1], vec[1:2], vec[2:3]).astype(BF16)
    kv = _dot(h, wkv_ref[...])
    vt = _dot_nt(wvt_ref[...], h)
    tm = x.shape[0]
    gd = N_KV * HEAD_DIM
    key_blk = jnp.right_shift(pl.program_id(0) * tm + lax.broadcasted_iota(I32, (tm, SEL_BLKS), 0), SEL_SHIFT)
    ind = jnp.where((key_blk & (SEL_BLKS - 1)) == lax.broadcasted_iota(I32, (tm, SEL_BLKS), 1), 1.0, 0.0)
    pad_s = jnp.zeros((tm, LANES - HEAD_DIM - SEL_BLKS), F32)
    pad_w = jnp.zeros((tm, LANES - HEAD_DIM), F32)
    ones_row = jnp.concatenate([jnp.ones((1, tm), F32), jnp.zeros((V_ROWS - HEAD_DIM - 1, tm), F32)], axis=0)
    for g in range(N_KV):
        c = g * HEAD_DIM
        raw_ref[0, g] = kv[:, c:c + HEAD_DIM]
        raw_ref[1, g] = kv[:, gd + c:gd + c + HEAD_DIM]
        ks = kv[:, 2 * gd + c:2 * gd + c + HEAD_DIM]
        kw = kv[:, 4 * gd + c:4 * gd + c + HEAD_DIM]
        ks_ref[g] = jnp.concatenate([ks, ind, pad_s], axis=1).astype(BF16)
        kw_ref[g] = jnp.concatenate([kw, pad_w], axis=1).astype(BF16)
        vst_ref[g] = jnp.concatenate([vt[c:c + HEAD_DIM], ones_row], axis=0).astype(BF16)
        vwt_ref[g] = jnp.concatenate([vt[gd + c:gd + c + HEAD_DIM], ones_row], axis=0).astype(BF16)


def _kvproj(x, vec, w_kv, w_vt):
    s, d = x.shape
    nkv = w_kv.shape[1]
    return pl.pallas_call(
        _kvproj_kernel,
        grid=(s // ROW_TILE,),
        in_specs=[pl.BlockSpec((ROW_TILE, d), lambda i: (i, 0)),
                  _resident((8, d), lambda i: (0, 0)),
                  _resident((d, nkv), lambda i: (0, 0)),
                  _resident(w_vt.shape, lambda i: (0, 0))],
        out_specs=[pl.BlockSpec((2, N_KV, ROW_TILE, HEAD_DIM), lambda i: (0, 0, i, 0)),
                   pl.BlockSpec((N_KV, ROW_TILE, LANES), lambda i: (0, i, 0)),
                   pl.BlockSpec((N_KV, ROW_TILE, LANES), lambda i: (0, i, 0)),
                   pl.BlockSpec((N_KV, V_ROWS, ROW_TILE), lambda i: (0, 0, i)),
                   pl.BlockSpec((N_KV, V_ROWS, ROW_TILE), lambda i: (0, 0, i))],
        out_shape=[jax.ShapeDtypeStruct((2, N_KV, s, HEAD_DIM), F32),
                   jax.ShapeDtypeStruct((N_KV, s, LANES), BF16),
                   jax.ShapeDtypeStruct((N_KV, s, LANES), BF16),
                   jax.ShapeDtypeStruct((N_KV, V_ROWS, s), BF16),
                   jax.ShapeDtypeStruct((N_KV, V_ROWS, s), BF16)],
        compiler_params=_params("arbitrary"),
        name="nsa_kvproj",
    )(x, vec, w_kv, w_vt)


def _compress_kernel(raw_ref, pos_ref, w1_ref, b1_ref, w2_ref, b2_ref, n_ref, t_ref):
    raw = raw_ref[...]
    nc = raw.shape[0]
    first = _dot((raw + pos_ref[0]).astype(BF16), w1_ref[0])
    second = _dot((raw + pos_ref[1]).astype(BF16), w1_ref[1])
    hid = jax.nn.gelu(first + pltpu.roll(second, nc - 1, 0) + b1_ref[...])
    out = _dot(hid.astype(BF16), w2_ref[...]) + b2_ref[...]
    n_ref[...] = out.astype(BF16)
    row = lax.broadcasted_iota(I32, (LANES, nc), 0)
    t_ref[...] = jnp.where(row == HEAD_DIM, 1.0, out.T).astype(BF16)


def _compress(raw, pos, w1, b1, w2, b2):
    _, g, nc, width = raw.shape
    hid = w1.shape[-1]
    return pl.pallas_call(
        _compress_kernel,
        grid=(2, g),
        in_specs=[pl.BlockSpec((None, None, nc, width), lambda j, k: (j, k, 0, 0)),
                  pl.BlockSpec((None, 2, 1, width), lambda j, k: (j, 0, 0, 0)),
                  pl.BlockSpec((None, 2, width, hid), lambda j, k: (j, 0, 0, 0)),
                  pl.BlockSpec((None, 1, hid), lambda j, k: (j, 0, 0)),
                  pl.BlockSpec((None, hid, LANES), lambda j, k: (j, 0, 0)),
                  pl.BlockSpec((None, 1, LANES), lambda j, k: (j, 0, 0))],
        out_specs=[pl.BlockSpec((None, None, nc, LANES), lambda j, k: (j, k, 0, 0)),
                   pl.BlockSpec((None, None, LANES, nc), lambda j, k: (j, k, 0, 0))],
        out_shape=[jax.ShapeDtypeStruct((2, g, nc, LANES), BF16),
                   jax.ShapeDtypeStruct((2, g, LANES, nc), BF16)],
        compiler_params=_params("arbitrary", "arbitrary"),
        name="nsa_compress",
    )(raw, pos, w1, b1, w2, b2)


def _qproj_kernel(x_ref, vec_ref, wt_ref, q_ref, gate_ref):
    x = x_ref[...]
    vec = vec_ref[...]
    h = _prenorm(x, vec[0:1], vec[1:2], vec[2:3]).astype(BF16)
    pt = _dot_nt(wt_ref[...], h)
    nq = N_HEADS * HEAD_DIM
    q_ref[...] = (pt[:nq] * (HEAD_DIM ** -0.5)).astype(BF16)
    gates = jax.nn.sigmoid(pt[nq:nq + 3 * N_HEADS])
    per = 3 * HPG
    pad = jnp.zeros((16 - per, x.shape[0]), F32)
    for g in range(N_KV):
        gate_ref[g] = jnp.concatenate([gates[g * per:(g + 1) * per], pad], axis=0)


def _qproj(x, vec, w_t):
    s, d = x.shape
    return pl.pallas_call(
        _qproj_kernel,
        grid=(s // ROW_TILE,),
        in_specs=[pl.BlockSpec((ROW_TILE, d), lambda i: (i, 0)),
                  _resident((8, d), lambda i: (0, 0)),
                  _resident(w_t.shape, lambda i: (0, 0))],
        out_specs=[pl.BlockSpec((N_HEADS * HEAD_DIM, ROW_TILE), lambda i: (0, i)),
                   pl.BlockSpec((N_KV, 16, ROW_TILE), lambda i: (0, 0, i))],
        out_shape=[jax.ShapeDtypeStruct((N_HEADS * HEAD_DIM, s), BF16),
                   jax.ShapeDtypeStruct((N_KV, 16, s), F32)],
        compiler_params=_params("arbitrary"),
        name="nsa_qproj",
    )(x, vec, w_t)


def _attn_kernel(q_ref, gate_ref, kc_ref, vct_ref, ov_ref, ks_ref, vst_ref, kw_ref, vwt_ref,
                 o_ref, bias_ref):
    i = pl.program_id(1)
    nq = HPG * Q_BLK
    nc = kc_ref.shape[0]
    nb = ov_ref.shape[0]
    qb = q_ref[...]
    q4 = jnp.concatenate([qb[h * HEAD_DIM:(h + 1) * HEAD_DIM] for h in range(HPG)], axis=1)
    q_plain = jnp.concatenate([q4, jnp.zeros((LANES - HEAD_DIM, nq), BF16)], axis=0)
    t1 = i * Q_BLK + lax.broadcasted_iota(I32, (1, Q_BLK), 1)
    t4 = jnp.concatenate([t1] * HPG, axis=1)

    sc = _dot(kc_ref[...], q_plain)
    cmp_end = lax.broadcasted_iota(I32, (nc, nq), 0) * D_CMP + (L_CMP - 1)
    valid_c = cmp_end <= t4
    sc = jnp.where(valid_c, sc, NEG)
    mc = jnp.max(sc, axis=0, keepdims=True)
    ec = jnp.where(valid_c, jnp.exp(sc - mc), 0.0)
    lc = jnp.sum(ec, axis=0, keepdims=True)
    pc = ec * jnp.where(lc > 0.0, 1.0 / lc, 0.0)
    o_cmp = _dot(vct_ref[0:HEAD_DIM, :], pc.astype(BF16))

    psum = pc[:, 0:Q_BLK]
    for h in range(1, HPG):
        psum = psum + pc[:, h * Q_BLK:(h + 1) * Q_BLK]
    p_hi, p_lo = _split_bf16(psum)
    imp = _dot(ov_ref[...], p_hi) + _dot(ov_ref[...], p_lo)
    j_io = lax.broadcasted_iota(I32, (nb, Q_BLK), 0)
    jt = jnp.right_shift(t1, SEL_SHIFT)
    forced = (j_io == 0) | (j_io == jt) | (j_io == jt - 1)
    w0 = jnp.where(j_io * L_SEL <= t1, imp + jnp.where(forced, FORCE_BONUS, 0.0), NEG)

    def pick(_, carry):
        w, sel = carry
        m = jnp.max(w, axis=0, keepdims=True)
        first = jnp.min(jnp.where(w == m, j_io, nb), axis=0, keepdims=True)
        chosen = (j_io == first) & (m > 0.5 * NEG)
        return jnp.where(chosen, NEG, w), jnp.where(chosen, 1.0, sel)

    _, sel = lax.fori_loop(0, min(N_SELECT, nb), pick, (w0, jnp.zeros((nb, Q_BLK), F32)))
    bias1 = jnp.where(sel > 0.0, 0.0, NEG)
    bias_ref[...] = jnp.concatenate([bias1] * HPG, axis=1)

    vpad = jnp.zeros((LANES - HEAD_DIM - 2 * SEL_BLKS, nq), BF16)
    bpad = jnp.zeros((SEL_BLKS, nq), F32)

    def sel_chunk(c, carry, causal):
        m_run, acc = carry
        koff = pl.multiple_of(c * SEL_CHUNK, SEL_CHUNK)
        brow = bias_ref[pl.ds(pl.multiple_of(c * SEL_BLKS, SEL_BLKS), SEL_BLKS), :]
        b16 = jnp.concatenate([brow, bpad], axis=0).astype(BF16)
        qa = jnp.concatenate([q4, b16, vpad], axis=0)
        s = _dot(ks_ref[pl.ds(koff, SEL_CHUNK), :], qa)
        if causal:
            kpos = koff + lax.broadcasted_iota(I32, (SEL_CHUNK, nq), 0)
            s = jnp.where(kpos <= t4, s, NEG)
        m_new = jnp.maximum(m_run, jnp.max(s, axis=0, keepdims=True))
        p = jnp.exp(s - m_new).astype(BF16)
        pv = _dot(vst_ref[:, pl.ds(koff, SEL_CHUNK)], p)
        return m_new, jnp.exp(m_run - m_new) * acc + pv

    n_chunks = i // (SEL_CHUNK // Q_BLK) + 1
    init = (jnp.full((1, nq), NEG, F32), jnp.zeros((V_ROWS, nq), F32))
    carry = lax.fori_loop(0, n_chunks - 1, functools.partial(sel_chunk, causal=False), init)
    _, acc_s = sel_chunk(n_chunks - 1, carry, causal=True)
    o_sel = acc_s[0:HEAD_DIM] / acc_s[HEAD_DIM:HEAD_DIM + 1]

    span = WINDOW + Q_BLK
    wstart = pl.multiple_of(jnp.maximum(i * Q_BLK - WINDOW, 0), Q_BLK)
    sw = _dot(kw_ref[pl.ds(wstart, span), :], q_plain)
    wpos = wstart + lax.broadcasted_iota(I32, (span, nq), 0)
    sw = jnp.where((wpos <= t4) & (wpos > t4 - WINDOW), sw, NEG)
    pw = jnp.exp(sw - jnp.max(sw, axis=0, keepdims=True)).astype(BF16)
    acc_w = _dot(vwt_ref[:, pl.ds(wstart, span)], pw)
    o_win = acc_w[0:HEAD_DIM] / acc_w[HEAD_DIM:HEAD_DIM + 1]

    gates = gate_ref[...]
    outs = []
    for h in range(HPG):
        cols = slice(h * Q_BLK, (h + 1) * Q_BLK)
        outs.append(o_cmp[:, cols] * gates[3 * h:3 * h + 1]
                    + o_sel[:, cols] * gates[3 * h + 1:3 * h + 2]
                    + o_win[:, cols] * gates[3 * h + 2:3 * h + 3])
    o_ref[...] = jnp.concatenate(outs, axis=0).astype(BF16)


def _attention(q_t, gate_t, kc, vct, overlap_t, ks, vst, kw, vwt):
    nqd, s = q_t.shape
    nc = kc.shape[1]
    nb = overlap_t.shape[0]
    rows = HPG * HEAD_DIM
    return pl.pallas_call(
        _attn_kernel,
        grid=(N_KV, s // Q_BLK),
        in_specs=[pl.BlockSpec((rows, Q_BLK), lambda g, i: (g, i)),
                  pl.BlockSpec((None, 16, Q_BLK), lambda g, i: (g, 0, i)),
                  _resident((None, nc, LANES), lambda g, i: (g, 0, 0)),
                  _resident((None, LANES, nc), lambda g, i: (g, 0, 0)),
                  _resident((nb, nc), lambda g, i: (0, 0)),
                  _resident((None, s, LANES), lambda g, i: (g, 0, 0)),
                  _resident((None, V_ROWS, s), lambda g, i: (g, 0, 0)),
                  _resident((None, s, LANES), lambda g, i: (g, 0, 0)),
                  _resident((None, V_ROWS, s), lambda g, i: (g, 0, 0))],
        out_specs=pl.BlockSpec((rows, Q_BLK), lambda g, i: (g, i)),
        out_shape=jax.ShapeDtypeStruct((nqd, s), BF16),
        scratch_shapes=[pltpu.VMEM((nb, HPG * Q_BLK), F32)],
        compiler_params=_params("arbitrary", "arbitrary"),
        name="nsa_attention",
    )(q_t, gate_t, kc, vct, overlap_t, ks, vst, kw, vwt)


def _outproj_kernel(ot_ref, x_ref, vec_ref, w_ref, o_ref):
    vec = vec_ref[...]
    y = _dot_tn(ot_ref[...], w_ref[...])
    o_ref[...] = _post_residual(x_ref[...], y, vec[3:4], vec[4:5])


def _outproj(o_t, x, vec, w_out):
    s, d = x.shape
    nqd = o_t.shape[0]
    return pl.pallas_call(
        _outproj_kernel,
        grid=(s // ROW_TILE,),
        in_specs=[pl.BlockSpec((nqd, ROW_TILE), lambda i: (0, i)),
                  pl.BlockSpec((ROW_TILE, d), lambda i: (i, 0)),
                  _resident((8, d), lambda i: (0, 0)),
                  _resident((nqd, d), lambda i: (0, 0))],
        out_specs=pl.BlockSpec((ROW_TILE, d), lambda i: (i, 0)),
        out_shape=jax.ShapeDtypeStruct((s, d), F32),
        compiler_params=_params("arbitrary"),
        name="nsa_outproj",
    )(o_t, x, vec, w_out)


def _vec(pre_g, mod, post_g):
    d = pre_g.shape[0]
    shift, scale, gate = mod[:d], mod[d:2 * d], mod[2 * d:3 * d]
    z = jnp.zeros((d,), F32)
    return jnp.stack([pre_g, scale, shift, post_g, gate, z, z, z])


def _overlap_t(s):
    nb, nc = s // L_SEL, s // D_CMP
    ci = jnp.arange(nc)[None, :] * D_CMP
    sj = jnp.arange(nb)[:, None] * L_SEL
    return ((ci < sj + L_SEL) & (ci + L_CMP > sj)).astype(BF16)


def kernel(x, c, ada_w, ada_b, norm_pre_g, norm_post_g, a_w_in, a_ln_g, a_ln_b, a_ws, a_bs, a_w_out, kv_norm_g, kv_ada_w, kv_ada_b, w_kv, cmp_pos, cmp_w1, cmp_b1, cmp_w2, cmp_b2, b_w_in, b_w_out, ffn_w_gate, ffn_w_up, ffn_w_down, moe_router, moe_router_b, moe_w_gate, moe_w_up, moe_w_down):
    batch, s, d = x.shape
    assert batch == 1 and s % SEL_CHUNK == 0 and s >= WINDOW + Q_BLK
    depth = ada_w.shape[0]
    n_a = depth // 2
    xs = x.reshape(s, d)

    mods = _ada(c, ada_w.reshape(depth * 2, d, 3 * d), ada_b.reshape(depth * 2, 3 * d)).reshape(depth, 2, 3 * d)
    kv_mod = _ada(c, kv_ada_w.reshape(1, d, 2 * d), kv_ada_b.reshape(1, 2 * d))[0]
    overlap_t = _overlap_t(s)
    shared = None

    for layer in range(depth):
        vec = _vec(norm_pre_g[layer, 0], mods[layer, 0], norm_post_g[layer, 0])
        if layer < n_a:
            xs = _gmlp_layer(xs, vec, a_w_in[layer].astype(BF16),
                             jnp.stack([a_ln_g[layer], a_ln_b[layer]]),
                             a_ws[layer], a_bs[layer].T, a_w_out[layer].astype(BF16))
        else:
            if shared is None:
                kv_vec = _vec(kv_norm_g, jnp.concatenate([kv_mod, jnp.zeros((d,), F32)]), jnp.zeros((d,), F32))
                gd = N_KV * HEAD_DIM
                w_vt = jnp.concatenate([w_kv[:, 3 * gd:4 * gd], w_kv[:, 5 * gd:6 * gd]], axis=1).T
                raw, ks, kw, vst, vwt = _kvproj(xs, kv_vec, w_kv.astype(BF16), w_vt.astype(BF16))
                nc = s // D_CMP
                width = D_CMP * HEAD_DIM
                hid = cmp_w1.shape[-1]
                w2p = jnp.pad(cmp_w2, ((0, 0), (0, 0), (0, LANES - HEAD_DIM))).astype(BF16)
                b2p = jnp.pad(cmp_b2, ((0, 0), (0, LANES - HEAD_DIM))).reshape(2, 1, LANES)
                cmp_n, cmp_t = _compress(raw.reshape(2, N_KV, nc, width),
                                         cmp_pos.reshape(2, 2, 1, width),
                                         cmp_w1.reshape(2, 2, width, hid).astype(BF16),
                                         cmp_b1.reshape(2, 1, hid), w2p, b2p)
                shared = (cmp_n[0], cmp_t[1], ks, vst, kw, vwt)
            i = layer - n_a
            nq = N_HEADS * HEAD_DIM
            w_t = b_w_in[i].T.astype(BF16)
            q_t, gate_t = _qproj(xs, vec, w_t)
            kc, vct, ks, vst, kw, vwt = shared
            o_t = _attention(q_t, gate_t, kc, vct, overlap_t, ks, vst, kw, vwt)
            xs = _outproj(o_t, xs, vec, b_w_out[i].astype(BF16))

        vec = _vec(norm_pre_g[layer, 1], mods[layer, 1], norm_post_g[layer, 1])
        j = layer // 2
        if layer % 2 == 0:
            xs = _swiglu_layer(xs, vec, ffn_w_gate[j].astype(BF16), ffn_w_up[j].astype(BF16),
                               ffn_w_down[j].astype(BF16))
        else:
            xs = _moe_layer(xs, vec, moe_router[j].T, moe_router_b[j].reshape(N_EXPERTS, 1),
                            moe_w_gate[j].astype(BF16), moe_w_up[j].astype(BF16), moe_w_down[j].astype(BF16))
    return xs.reshape(batch, s, d)
```

```python
import jax
import jax.numpy as jnp
from jax import lax
from jax.experimental import pallas as pl
from jax.experimental.pallas import tpu as pltpu

F32 = jnp.float32
BF16 = jnp.bfloat16
I32 = jnp.int32

EPS = 1e-6
NEG = -1e30

LANES = 128
VMEM_LIMIT_BYTES = 56 * 1024 * 1024

CHUNK = 128
A_GROUPS = 8
N_HEADS = 16
N_KV = 4
HPG = N_HEADS // N_KV
HEAD_DIM = 64
L_CMP = 32
D_CMP = 16
L_SEL = 64
SEL_SHIFT = 6
N_SELECT = 16
N_FORCED = 3
WINDOW = 512
Q_BLK = 128
N_EXPERTS = 8
MOE_BLOCK = 256

ROW_TILE = 256
TOK_TILE = 256
SEL_CHUNK = 512
SEL_BLKS = SEL_CHUNK // L_SEL
ATT_GROUPS = 2
PS_PAD = 8
V_ROWS = 72
F_CHUNK = 512
WIN_PART_ROWS = 128
WIN_PARTS = TOK_TILE // WIN_PART_ROWS + 1
WIN_ROWS = WIN_PARTS * WIN_PART_ROWS
LOG2E = 1.4426950408889634


def _params(*sem):
    return pltpu.CompilerParams(dimension_semantics=sem, vmem_limit_bytes=VMEM_LIMIT_BYTES)


def _resident(shape, index_map):
    return pl.BlockSpec(shape, index_map, pipeline_mode=pl.Buffered(1))


def _split_bf16(a):
    hi = a.astype(BF16)
    lo = (a - hi.astype(F32)).astype(BF16)
    return hi, lo


def _dot(a, b):
    return jnp.dot(a, b, preferred_element_type=F32)


def _dot_nt(a, b):
    return lax.dot_general(a, b, (((1,), (1,)), ((), ())), preferred_element_type=F32)


def _dot_tn(a, b):
    return lax.dot_general(a, b, (((0,), (0,)), ((), ())), preferred_element_type=F32)


def _prenorm(x, g, scale, shift):
    ms = jnp.mean(x * x, axis=-1, keepdims=True)
    return (x * lax.rsqrt(ms + EPS) * g) * (1.0 + scale) + shift


def _post_residual(x, y, g, gate):
    ms = jnp.mean(y * y, axis=-1, keepdims=True)
    return x + gate * (y * lax.rsqrt(ms + EPS) * g)


def _ada_kernel(c_ref, w_ref, b_ref, o_ref):
    c = c_ref[...]
    c_act = jnp.broadcast_to(c * jax.nn.sigmoid(c), (8, c.shape[1]))
    c_hi, c_lo = _split_bf16(c_act)
    w_hi, w_lo = _split_bf16(w_ref[...])
    m = _dot(c_hi, w_hi) + (_dot(c_hi, w_lo) + _dot(c_lo, w_hi))
    o_ref[...] = m[0:1] + b_ref[...]


def _ada(c, w, b):
    n, d, nn = w.shape
    out = pl.pallas_call(
        _ada_kernel,
        grid=(n, nn // d),
        in_specs=[pl.BlockSpec((1, d), lambda i, j: (0, 0)),
                  pl.BlockSpec((None, d, d), lambda i, j: (i, 0, j)),
                  pl.BlockSpec((None, 1, d), lambda i, j: (i, 0, j))],
        out_specs=pl.BlockSpec((None, 1, d), lambda i, j: (i, 0, j)),
        out_shape=jax.ShapeDtypeStruct((n, 1, nn), F32),
        compiler_params=_params("arbitrary", "arbitrary"),
        name="ada",
    )(c, w, b.reshape(n, 1, nn))
    return out.reshape(n, nn)


def _gmlp_kernel(x_ref, vec_ref, win_ref, ln_ref, ws_ref, bst_ref, wout_ref, o_ref, gated_ref):
    x = x_ref[...]
    vec = vec_ref[...]
    h = _prenorm(x, vec[0:1], vec[1:2], vec[2:3]).astype(BF16)
    z = jax.nn.gelu(_dot(h, win_ref[...]))
    width = z.shape[1] // 2
    u = z[:, :width]
    v = z[:, width:]
    mu = jnp.mean(v, axis=-1, keepdims=True)
    vc = v - mu
    var = jnp.mean(vc * vc, axis=-1, keepdims=True)
    ln = ln_ref[...]
    vn = (vc * lax.rsqrt(var + EPS) * ln[0:1] + ln[1:2]).astype(BF16)
    causal = (lax.broadcasted_iota(I32, (CHUNK, CHUNK), 0)
              >= lax.broadcasted_iota(I32, (CHUNK, CHUNK), 1))
    gw = width // A_GROUPS
    bst = bst_ref[...]
    for g in range(A_GROUPS):
        wg = jnp.where(causal, ws_ref[g], 0.0).astype(BF16)
        for ck in range(x.shape[0] // CHUNK):
            rows = slice(ck * CHUNK, (ck + 1) * CHUNK)
            cols = slice(g * gw, (g + 1) * gw)
            mixed = _dot(wg, vn[rows, cols]) + bst[:, g:g + 1]
            gated_ref[rows, cols] = (u[rows, cols] * mixed).astype(BF16)
    y = _dot(gated_ref[...], wout_ref[...])
    o_ref[...] = _post_residual(x, y, vec[3:4], vec[4:5])


def _gmlp_layer(x, vec, w_in, ln, ws, bst, w_out):
    s, d = x.shape
    e2 = w_in.shape[1]
    return pl.pallas_call(
        _gmlp_kernel,
        grid=(s // ROW_TILE,),
        in_specs=[pl.BlockSpec((ROW_TILE, d), lambda i: (i, 0)),
                  _resident((8, d), lambda i: (0, 0)),
                  _resident((d, e2), lambda i: (0, 0)),
                  _resident((2, e2 // 2), lambda i: (0, 0)),
                  _resident(ws.shape, lambda i: (0, 0, 0)),
                  _resident(bst.shape, lambda i: (0, 0)),
                  _resident((e2 // 2, d), lambda i: (0, 0))],
        out_specs=pl.BlockSpec((ROW_TILE, d), lambda i: (i, 0)),
        out_shape=jax.ShapeDtypeStruct((s, d), F32),
        scratch_shapes=[pltpu.VMEM((ROW_TILE, e2 // 2), BF16)],
        compiler_params=_params("arbitrary"),
        name="gmlp",
    )(x, vec, w_in, ln, ws, bst, w_out)


def _swiglu_kernel(x_ref, vec_ref, wg_ref, wu_ref, wd_ref, o_ref):
    x = x_ref[...]
    vec = vec_ref[...]
    h = _prenorm(x, vec[0:1], vec[1:2], vec[2:3]).astype(BF16)
    g = _dot(h, wg_ref[...])
    a = (g * jax.nn.sigmoid(g) * _dot(h, wu_ref[...])).astype(BF16)
    y = _dot(a, wd_ref[...])
    o_ref[...] = _post_residual(x, y, vec[3:4], vec[4:5])


def _swiglu_layer(x, vec, w_gate, w_up, w_down):
    s, d = x.shape
    f = w_gate.shape[1]
    return pl.pallas_call(
        _swiglu_kernel,
        grid=(s // ROW_TILE,),
        in_specs=[pl.BlockSpec((ROW_TILE, d), lambda i: (i, 0)),
                  _resident((8, d), lambda i: (0, 0)),
                  _resident((d, f), lambda i: (0, 0)),
                  _resident((d, f), lambda i: (0, 0)),
                  _resident((f, d), lambda i: (0, 0))],
        out_specs=pl.BlockSpec((ROW_TILE, d), lambda i: (i, 0)),
        out_shape=jax.ShapeDtypeStruct((s, d), F32),
        compiler_params=_params("arbitrary"),
        name="swiglu",
    )(x, vec, w_gate, w_up, w_down)


def _router_kernel(x_ref, vec_ref, wrt_ref, br_ref, h_ref, idx_ref, gate_ref):
    x = x_ref[...]
    vec = vec_ref[...]
    h = _prenorm(x, vec[0:1], vec[1:2], vec[2:3])
    h_ref[...] = h.astype(BF16)
    h_hi, h_lo = _split_bf16(h)
    w_hi, w_lo = _split_bf16(wrt_ref[...])
    logit = _dot_nt(w_hi, h_hi) + (_dot_nt(w_hi, h_lo) + _dot_nt(w_lo, h_hi)) + br_ref[...]
    ne = logit.shape[0]
    eidx = lax.broadcasted_iota(I32, logit.shape, 0)
    m1 = jnp.max(logit, axis=0, keepdims=True)
    i1 = jnp.min(jnp.where(logit == m1, eidx, ne), axis=0, keepdims=True)
    rest = jnp.where(eidx == i1, -jnp.inf, logit)
    m2 = jnp.max(rest, axis=0, keepdims=True)
    i2 = jnp.min(jnp.where(rest == m2, eidx, ne), axis=0, keepdims=True)
    e21 = jnp.exp(m2 - m1)
    g1 = 1.0 / (1.0 + e21)
    g2 = e21 * g1
    pad_i = jnp.zeros((6, i1.shape[1]), I32)
    pad_f = jnp.zeros((6, i1.shape[1]), F32)
    idx_ref[...] = jnp.concatenate([i1, i2, pad_i], axis=0)
    gate_ref[...] = jnp.concatenate([g1, g2, pad_f], axis=0)


def _router(x, vec, w_rt, b_r):
    s, d = x.shape
    ne = w_rt.shape[0]
    return pl.pallas_call(
        _router_kernel,
        grid=(s // ROW_TILE,),
        in_specs=[pl.BlockSpec((ROW_TILE, d), lambda i: (i, 0)),
                  _resident((8, d), lambda i: (0, 0)),
                  _resident((ne, d), lambda i: (0, 0)),
                  _resident((ne, 1), lambda i: (0, 0))],
        out_specs=[pl.BlockSpec((ROW_TILE, d), lambda i: (i, 0)),
                   pl.BlockSpec((8, ROW_TILE), lambda i: (0, i)),
                   pl.BlockSpec((8, ROW_TILE), lambda i: (0, i))],
        out_shape=[jax.ShapeDtypeStruct((s, d), BF16),
                   jax.ShapeDtypeStruct((8, s), I32),
                   jax.ShapeDtypeStruct((8, s), F32)],
        compiler_params=_params("arbitrary"),
        name="moe_router",
    )(x, vec, w_rt, b_r)


def _rank_kernel(idx_ref, rank_ref, start_ref, count_ref, carry_ref):
    @pl.when(pl.program_id(0) == 0)
    def _():
        carry_ref[...] = jnp.zeros_like(carry_ref)

    idx = idx_ref[...]
    tt = idx.shape[1]
    eidx = lax.broadcasted_iota(I32, (N_EXPERTS, tt), 0)
    hit1 = eidx == idx[0:1]
    hit2 = eidx == idx[1:2]
    member = jnp.where(hit1 | hit2, 1.0, 0.0)
    before = (lax.broadcasted_iota(I32, (tt, tt), 0) < lax.broadcasted_iota(I32, (tt, tt), 1))
    carry = carry_ref[...]
    cum = _dot(member.astype(BF16), jnp.where(before, 1.0, 0.0).astype(BF16)) + carry[:, 0:1]
    r1 = jnp.sum(jnp.where(hit1, cum, 0.0), axis=0, keepdims=True)
    r2 = jnp.sum(jnp.where(hit2, cum, 0.0), axis=0, keepdims=True)
    rank_ref[...] = jnp.concatenate([r1, r2, jnp.zeros((6, tt), F32)], axis=0).astype(I32)
    tile_count = jnp.broadcast_to(jnp.sum(member, axis=1, keepdims=True), carry.shape)
    start_ref[...] = carry
    count_ref[...] = tile_count
    carry_ref[...] = carry + tile_count


def _ranks(idx):
    s = idx.shape[1]
    nt = s // TOK_TILE
    return pl.pallas_call(
        _rank_kernel,
        grid=(nt,),
        in_specs=[pl.BlockSpec((8, TOK_TILE), lambda i: (0, i))],
        out_specs=[pl.BlockSpec((8, TOK_TILE), lambda i: (0, i)),
                   pl.BlockSpec((None, N_EXPERTS, LANES), lambda i: (i, 0, 0)),
                   pl.BlockSpec((None, N_EXPERTS, LANES), lambda i: (i, 0, 0))],
        out_shape=[jax.ShapeDtypeStruct((8, s), I32),
                   jax.ShapeDtypeStruct((nt, N_EXPERTS, LANES), F32),
                   jax.ShapeDtypeStruct((nt, N_EXPERTS, LANES), F32)],
        scratch_shapes=[pltpu.VMEM((N_EXPERTS, LANES), F32)],
        compiler_params=_params("arbitrary"),
        name="moe_ranks",
    )(idx)


def _match(idx, rank, expert, rows):
    r1 = jnp.where(idx[0:1] == expert, rank[0:1], -1)
    r2 = jnp.where(idx[1:2] == expert, rank[1:2], -1)
    return rows == r1, rows == r2


def _dispatch_kernel(be_ref, lb_ref, tlo_ref, thi_ref, idx_ref, rank_ref, gate_ref, h_ref, o_ref, rg_ref,
                     acc_ref, gacc_ref):
    b = pl.program_id(0)
    expert = be_ref[b]
    rows = lax.broadcasted_iota(I32, (MOE_BLOCK, TOK_TILE), 0) + lb_ref[b]
    acc_ref[...] = jnp.zeros_like(acc_ref)
    gacc_ref[...] = jnp.zeros_like(gacc_ref)

    def body(t, carry):
        off = pl.multiple_of(t * TOK_TILE, TOK_TILE)
        m1, m2 = _match(idx_ref[:, pl.ds(off, TOK_TILE)], rank_ref[:, pl.ds(off, TOK_TILE)], expert, rows)
        gates = gate_ref[:, pl.ds(off, TOK_TILE)]
        gacc_ref[...] += jnp.sum(jnp.where(m1, gates[0:1], 0.0) + jnp.where(m2, gates[1:2], 0.0),
                                 axis=1, keepdims=True)
        onehot = jnp.where(m1 | m2, 1.0, 0.0).astype(BF16)
        acc_ref[...] += _dot(onehot, h_ref[pl.ds(off, TOK_TILE), :])
        return carry

    lax.fori_loop(tlo_ref[b], thi_ref[b], body, 0)
    o_ref[...] = acc_ref[...].astype(BF16)
    rg_ref[...] = gacc_ref[...]


def _dispatch(blk_e, blk_lb, blk_tlo, blk_thi, idx, rank, gates, h):
    s, d = h.shape
    n_blk = blk_e.shape[0]
    grid_spec = pltpu.PrefetchScalarGridSpec(
        num_scalar_prefetch=4,
        grid=(n_blk,),
        in_specs=[_resident((8, s), lambda b, *_: (0, 0)),
                  _resident((8, s), lambda b, *_: (0, 0)),
                  _resident((8, s), lambda b, *_: (0, 0)),
                  _resident((s, d), lambda b, *_: (0, 0))],
        out_specs=[pl.BlockSpec((MOE_BLOCK, d), lambda b, *_: (b, 0)),
                   pl.BlockSpec((MOE_BLOCK, 1), lambda b, *_: (b, 0))],
        scratch_shapes=[pltpu.VMEM((MOE_BLOCK, d), F32), pltpu.VMEM((MOE_BLOCK, 1), F32)],
    )
    return pl.pallas_call(
        _dispatch_kernel,
        grid_spec=grid_spec,
        out_shape=[jax.ShapeDtypeStruct((n_blk * MOE_BLOCK, d), BF16),
                   jax.ShapeDtypeStruct((n_blk * MOE_BLOCK, 1), F32)],
        compiler_params=_params("arbitrary"),
        name="moe_dispatch",
    )(blk_e, blk_lb, blk_tlo, blk_thi, idx, rank, gates, h)


def _expert_kernel(be_ref, used_ref, x_ref, rg_ref, wg_ref, wu_ref, wd_ref, o_ref):
    b = pl.program_id(0)

    @pl.when(b < used_ref[0])
    def _():
        x = x_ref[...]
        f = wg_ref.shape[1]
        y = jnp.zeros(o_ref.shape, F32)
        for c in range(f // F_CHUNK):
            cols = slice(c * F_CHUNK, (c + 1) * F_CHUNK)
            g = _dot(x, wg_ref[:, cols])
            a = (g * jax.nn.sigmoid(g) * _dot(x, wu_ref[:, cols])).astype(BF16)
            y = y + _dot(a, wd_ref[cols, :])
        o_ref[...] = (y * rg_ref[...]).astype(BF16)

    @pl.when(b >= used_ref[0])
    def _():
        o_ref[...] = jnp.zeros_like(o_ref)


def _experts(blk_e, n_used, x_buf, row_gate, w_gate, w_up, w_down):
    n_rows, d = x_buf.shape
    f = w_gate.shape[2]
    assert f % F_CHUNK == 0
    grid_spec = pltpu.PrefetchScalarGridSpec(
        num_scalar_prefetch=2,
        grid=(n_rows // MOE_BLOCK,),
        in_specs=[pl.BlockSpec((MOE_BLOCK, d), lambda b, be, nu: (b, 0)),
                  pl.BlockSpec((MOE_BLOCK, 1), lambda b, be, nu: (b, 0)),
                  _resident((None, d, f), lambda b, be, nu: (be[b], 0, 0)),
                  _resident((None, d, f), lambda b, be, nu: (be[b], 0, 0)),
                  _resident((None, f, d), lambda b, be, nu: (be[b], 0, 0))],
        out_specs=pl.BlockSpec((MOE_BLOCK, d), lambda b, be, nu: (b, 0)),
    )
    return pl.pallas_call(
        _expert_kernel,
        grid_spec=grid_spec,
        out_shape=jax.ShapeDtypeStruct((n_rows, d), BF16),
        compiler_params=_params("arbitrary"),
        name="moe_experts",
    )(blk_e, n_used, x_buf, row_gate, w_gate, w_up, w_down)


def _combine_kernel(win_ref, lb_ref, idx_ref, rank_ref, *refs):
    y_refs = refs[:N_EXPERTS * WIN_PARTS]
    x_ref, vec_ref, o_ref = refs[N_EXPERTS * WIN_PARTS:]
    t = pl.program_id(0)
    idx = idx_ref[...]
    rank = rank_ref[...]
    acc = jnp.zeros(o_ref.shape, F32)
    for e in range(N_EXPERTS):
        rows = lax.broadcasted_iota(I32, (WIN_ROWS, TOK_TILE), 0) + lb_ref[t * N_EXPERTS + e]
        m1, m2 = _match(idx, rank, e, rows)
        onehot = jnp.where(m1 | m2, 1.0, 0.0).astype(BF16)
        y = jnp.concatenate([y_refs[e * WIN_PARTS + k][...] for k in range(WIN_PARTS)], axis=0)
        acc = acc + _dot_tn(onehot, y)
    vec = vec_ref[...]
    o_ref[...] = _post_residual(x_ref[...], acc, vec[3:4], vec[4:5])


def _combine(win_start, win_lb, idx, rank, y_buf, x, vec):
    s, d = x.shape

    def y_spec(e, k):
        return pl.BlockSpec((WIN_PART_ROWS, d), lambda t, ws, lb: (ws[t * N_EXPERTS + e] + k, 0))

    grid_spec = pltpu.PrefetchScalarGridSpec(
        num_scalar_prefetch=2,
        grid=(s // TOK_TILE,),
        in_specs=[pl.BlockSpec((8, TOK_TILE), lambda t, *_: (0, t)),
                  pl.BlockSpec((8, TOK_TILE), lambda t, *_: (0, t))]
                 + [y_spec(e, k) for e in range(N_EXPERTS) for k in range(WIN_PARTS)]
                 + [pl.BlockSpec((TOK_TILE, d), lambda t, *_: (t, 0)),
                    _resident((8, d), lambda t, *_: (0, 0))],
        out_specs=pl.BlockSpec((TOK_TILE, d), lambda t, *_: (t, 0)),
    )
    return pl.pallas_call(
        _combine_kernel,
        grid_spec=grid_spec,
        out_shape=jax.ShapeDtypeStruct((s, d), F32),
        compiler_params=_params("arbitrary"),
        name="moe_combine",
    )(win_start, win_lb, idx, rank, *([y_buf] * (N_EXPERTS * WIN_PARTS)), x, vec)


def _moe_layer(x, vec, w_rt, b_r, w_gate, w_up, w_down, w_first):
    s, d = x.shape
    nt = s // TOK_TILE
    h, idx, gates = _router(x, vec, w_rt, b_r)
    rank, tile_start, tile_count = _ranks(idx)

    tile_start = tile_start[:, :, 0].astype(I32)
    tile_count = tile_count[:, :, 0].astype(I32)
    tile_end = tile_start + tile_count
    counts = tile_end[-1]
    padded = (counts + MOE_BLOCK - 1) // MOE_BLOCK * MOE_BLOCK
    pend = jnp.cumsum(padded)
    pstart = pend - padded
    n_rows = -(-(2 * s) // MOE_BLOCK) * MOE_BLOCK + N_EXPERTS * MOE_BLOCK
    n_blk = n_rows // MOE_BLOCK
    blk_row = jnp.arange(n_blk, dtype=I32) * MOE_BLOCK
    blk_e = jnp.minimum(jnp.sum(blk_row[:, None] >= pend[None, :], axis=1), N_EXPERTS - 1).astype(I32)
    blk_lb = blk_row - pstart[blk_e]
    te = tile_end[:, blk_e]
    ts = tile_start[:, blk_e]
    blk_tlo = jnp.sum(te <= blk_lb[None, :], axis=0).astype(I32)
    blk_thi = jnp.sum(ts < (blk_lb + MOE_BLOCK)[None, :], axis=0).astype(I32)
    n_used = (pend[-1] // MOE_BLOCK).astype(I32).reshape(1)

    x_buf, row_gate = _dispatch(blk_e, blk_lb.astype(I32), blk_tlo, blk_thi, idx, rank, gates, h)
    y_buf = _experts(blk_e + w_first, n_used, x_buf, row_gate, w_gate, w_up, w_down)

    win_start = jnp.minimum((pstart[None, :] + tile_start) // WIN_PART_ROWS, n_rows // WIN_PART_ROWS - WIN_PARTS)
    win_lb = win_start * WIN_PART_ROWS - pstart[None, :]
    return _combine(win_start.reshape(-1).astype(I32), win_lb.reshape(-1).astype(I32), idx, rank, y_buf, x, vec)


def _kvproj_kernel(x_ref, vec_ref, wkv_ref, wvt_ref, raw_ref, ks_ref, kw_ref, vst_ref, vwt_ref):
    x = x_ref[...]
    vec = vec_ref[...]
    h = _prenorm(x, vec[0:1], vec[1:2], vec[2:3]).astype(BF16)
    kv = _dot(h, wkv_ref[...])
    vt = _dot_nt(wvt_ref[...], h)
    tm = x.shape[0]
    gd = N_KV * HEAD_DIM
    key_blk = jnp.right_shift(pl.program_id(0) * tm + lax.broadcasted_iota(I32, (tm, SEL_BLKS), 0), SEL_SHIFT)
    ind = jnp.where((key_blk & (SEL_BLKS - 1)) == lax.broadcasted_iota(I32, (tm, SEL_BLKS), 1), 1.0, 0.0)
    pad_s = jnp.zeros((tm, LANES - HEAD_DIM - SEL_BLKS), F32)
    pad_w = jnp.zeros((tm, LANES - HEAD_DIM), F32)
    ones_row = jnp.concatenate([jnp.ones((1, tm), F32), jnp.zeros((V_ROWS - HEAD_DIM - 1, tm), F32)], axis=0)
    for g in range(N_KV):
        c = g * HEAD_DIM
        raw_ref[0, g] = kv[:, c:c + HEAD_DIM]
        raw_ref[1, g] = kv[:, gd + c:gd + c + HEAD_DIM]
        ks = kv[:, 2 * gd + c:2 * gd + c + HEAD_DIM]
        kw = kv[:, 4 * gd + c:4 * gd + c + HEAD_DIM]
        ks_ref[g] = jnp.concatenate([ks, ind, pad_s], axis=1).astype(BF16)
        kw_ref[g] = jnp.concatenate([kw, pad_w], axis=1).astype(BF16)
        vst_ref[g] = jnp.concatenate([vt[c:c + HEAD_DIM], ones_row], axis=0).astype(BF16)
        vwt_ref[g] = jnp.concatenate([vt[gd + c:gd + c + HEAD_DIM], ones_row], axis=0).astype(BF16)


def _kvproj(x, vec, w_kv, w_vt):
    s, d = x.shape
    nkv = w_kv.shape[1]
    return pl.pallas_call(
        _kvproj_kernel,
        grid=(s // ROW_TILE,),
        in_specs=[pl.BlockSpec((ROW_TILE, d), lambda i: (i, 0)),
                  _resident((8, d), lambda i: (0, 0)),
                  _resident((d, nkv), lambda i: (0, 0)),
                  _resident(w_vt.shape, lambda i: (0, 0))],
        out_specs=[pl.BlockSpec((2, N_KV, ROW_TILE, HEAD_DIM), lambda i: (0, 0, i, 0)),
                   pl.BlockSpec((N_KV, ROW_TILE, LANES), lambda i: (0, i, 0)),
                   pl.BlockSpec((N_KV, ROW_TILE, LANES), lambda i: (0, i, 0)),
                   pl.BlockSpec((N_KV, V_ROWS, ROW_TILE), lambda i: (0, 0, i)),
                   pl.BlockSpec((N_KV, V_ROWS, ROW_TILE), lambda i: (0, 0, i))],
        out_shape=[jax.ShapeDtypeStruct((2, N_KV, s, HEAD_DIM), F32),
                   jax.ShapeDtypeStruct((N_KV, s, LANES), BF16),
                   jax.ShapeDtypeStruct((N_KV, s, LANES), BF16),
                   jax.ShapeDtypeStruct((N_KV, V_ROWS, s), BF16),
                   jax.ShapeDtypeStruct((N_KV, V_ROWS, s), BF16)],
        compiler_params=_params("arbitrary"),
        name="nsa_kvproj",
    )(x, vec, w_kv, w_vt)


def _compress_kernel(raw_ref, pos_ref, w1_ref, b1_ref, w2_ref, b2_ref, n_ref, t_ref):
    raw = raw_ref[...]
    nc = raw.shape[0]
    first = _dot((raw + pos_ref[0]).astype(BF16), w1_ref[0])
    second = _dot((raw + pos_ref[1]).astype(BF16), w1_ref[1])
    hid = jax.nn.gelu(first + pltpu.roll(second, nc - 1, 0) + b1_ref[...])
    out = _dot(hid.astype(BF16), w2_ref[...]) + b2_ref[...]
    n_ref[...] = out.astype(BF16)
    row = lax.broadcasted_iota(I32, (LANES, nc), 0)
    t_ref[...] = jnp.where(row == HEAD_DIM, 1.0, out.T).astype(BF16)


def _compress(raw, pos, w1, b1, w2, b2):
    _, g, nc, width = raw.shape
    hid = w1.shape[-1]
    return pl.pallas_call(
        _compress_kernel,
        grid=(2, g),
        in_specs=[pl.BlockSpec((None, None, nc, width), lambda j, k: (j, k, 0, 0)),
                  pl.BlockSpec((None, 2, 1, width), lambda j, k: (j, 0, 0, 0)),
                  pl.BlockSpec((None, 2, width, hid), lambda j, k: (j, 0, 0, 0)),
                  pl.BlockSpec((None, 1, hid), lambda j, k: (j, 0, 0)),
                  pl.BlockSpec((None, hid, LANES), lambda j, k: (j, 0, 0)),
                  pl.BlockSpec((None, 1, LANES), lambda j, k: (j, 0, 0))],
        out_specs=[pl.BlockSpec((None, None, nc, LANES), lambda j, k: (j, k, 0, 0)),
                   pl.BlockSpec((None, None, LANES, nc), lambda j, k: (j, k, 0, 0))],
        out_shape=[jax.ShapeDtypeStruct((2, g, nc, LANES), BF16),
                   jax.ShapeDtypeStruct((2, g, LANES, nc), BF16)],
        compiler_params=_params("arbitrary", "arbitrary"),
        name="nsa_compress",
    )(raw, pos, w1, b1, w2, b2)


def _qproj_kernel(x_ref, vec_ref, wt_ref, q_ref, gate_ref):
    x = x_ref[...]
    vec = vec_ref[...]
    h = _prenorm(x, vec[0:1], vec[1:2], vec[2:3]).astype(BF16)
    pt = _dot_nt(wt_ref[...], h)
    nq = N_HEADS * HEAD_DIM
    q_ref[...] = (pt[:nq] * (HEAD_DIM ** -0.5 * LOG2E)).astype(BF16)
    gates = jax.nn.sigmoid(pt[nq:nq + 3 * N_HEADS])
    per = 3 * HPG
    pad = jnp.zeros((16 - per, x.shape[0]), F32)
    for g in range(N_KV):
        gate_ref[g] = jnp.concatenate([gates[g * per:(g + 1) * per], pad], axis=0)


def _qproj(x, vec, w_t):
    s, d = x.shape
    return pl.pallas_call(
        _qproj_kernel,
        grid=(s // ROW_TILE,),
        in_specs=[pl.BlockSpec((ROW_TILE, d), lambda i: (i, 0)),
                  _resident((8, d), lambda i: (0, 0)),
                  _resident(w_t.shape, lambda i: (0, 0))],
        out_specs=[pl.BlockSpec((N_HEADS * HEAD_DIM, ROW_TILE), lambda i: (0, i)),
                   pl.BlockSpec((N_KV, 16, ROW_TILE), lambda i: (0, 0, i))],
        out_shape=[jax.ShapeDtypeStruct((N_HEADS * HEAD_DIM, s), BF16),
                   jax.ShapeDtypeStruct((N_KV, 16, s), F32)],
        compiler_params=_params("arbitrary"),
        name="nsa_qproj",
    )(x, vec, w_t)


def _attn_kernel(q_ref, gate_ref, cmask_ref, wmask_ref, kc_ref, vct_ref, ks_ref, vst_ref, kw_ref, vwt_ref,
                 o_ref, bias_ref, ps_ref, s_ref, p_ref):
    i = pl.program_id(1)
    nq = HPG * Q_BLK
    nc = kc_ref.shape[1]
    nb = bias_ref.shape[1]
    rows = HPG * HEAD_DIM
    t1 = i * Q_BLK + lax.broadcasted_iota(I32, (1, Q_BLK), 1)
    t4 = jnp.concatenate([t1] * HPG, axis=1)
    j_io = lax.broadcasted_iota(I32, (nb, Q_BLK), 0)
    jt = jnp.right_shift(t1, SEL_SHIFT)
    forced = (j_io == 0) | (j_io == jt) | (j_io == jt - 1)
    cand_off = jnp.where((j_io * L_SEL <= t1) & jnp.logical_not(forced), 0.0, NEG)
    past_off = jnp.where(j_io * L_SEL < i * Q_BLK, 0.0, NEG)
    doff = pl.multiple_of(i * Q_BLK, Q_BLK)
    cmp_mask = cmask_ref[pl.ds(pl.multiple_of(nc - i * (Q_BLK // D_CMP), 8), nc), :]
    cmp_mask = jnp.concatenate([cmp_mask] * HPG, axis=1)
    span = WINDOW + Q_BLK
    wstart = pl.multiple_of(jnp.maximum(i * Q_BLK - WINDOW, 0), Q_BLK)
    win_mask = wmask_ref[pl.ds(pl.multiple_of(WINDOW - jnp.minimum(i * Q_BLK, WINDOW), Q_BLK), span), :]
    win_mask = jnp.concatenate([win_mask] * HPG, axis=1)
    diag_mask = jnp.concatenate([wmask_ref[WINDOW:WINDOW + Q_BLK, :]] * HPG, axis=1)
    vpad = jnp.zeros((LANES - HEAD_DIM - 2 * SEL_BLKS, nq), BF16)
    bpad = jnp.zeros((SEL_BLKS, nq), F32)

    def key_rows(c):
        return pl.ds(pl.multiple_of(c * SEL_CHUNK, SEL_CHUNK), SEL_CHUNK)

    def head(g):
        qb = q_ref[g * rows:(g + 1) * rows, :]
        q4 = jnp.concatenate([qb[h * HEAD_DIM:(h + 1) * HEAD_DIM] for h in range(HPG)], axis=1)
        q_plain = jnp.concatenate([q4, jnp.zeros((LANES - HEAD_DIM, nq), BF16)], axis=0)

        sc = _dot(kc_ref[g], q_plain) + cmp_mask
        ec = jnp.exp2(sc - jnp.max(sc, axis=0, keepdims=True))
        rl = jnp.where(t4 >= L_CMP - 1, 1.0 / jnp.sum(ec, axis=0, keepdims=True), 0.0)
        o_cmp = _dot(vct_ref[g, 0:HEAD_DIM, :], ec.astype(BF16)) * rl

        pc = ec * rl
        psum = pc[:, 0:Q_BLK]
        for h in range(1, HPG):
            psum = psum + pc[:, h * Q_BLK:(h + 1) * Q_BLK]
        ps_ref[g, 0:PS_PAD, :] = jnp.zeros((PS_PAD, Q_BLK), F32)
        ps_ref[g, PS_PAD:PS_PAD + nc, :] = psum
        ratio = L_SEL // D_CMP
        imp = jnp.zeros((nb, Q_BLK), F32)
        for k in range(1 - L_CMP // D_CMP, ratio):
            imp = imp + ps_ref[g, pl.ds(PS_PAD + k, nb, stride=ratio), :]

        w = imp + cand_off
        for _ in range(N_SELECT - N_FORCED):
            m = jnp.max(w, axis=0, keepdims=True)
            first = jnp.min(jnp.where(w == m, j_io, nb), axis=0, keepdims=True)
            first = jnp.where(m > 0.5 * NEG, first, -1)
            w = jnp.where(j_io == first, NEG, w)
        bias1 = jnp.where(w < 0.5 * NEG, past_off, NEG)
        bias_ref[g] = jnp.concatenate([bias1] * HPG, axis=1)

        sd = _dot(ks_ref[g, pl.ds(doff, Q_BLK), :], q_plain) + diag_mask
        m0 = jnp.max(sd, axis=0, keepdims=True)
        acc0 = _dot(vst_ref[g, :, pl.ds(doff, Q_BLK)], jnp.exp2(sd - m0).astype(BF16))

        sw = _dot(kw_ref[g, pl.ds(wstart, span), :], q_plain) + win_mask
        pw = jnp.exp2(sw - jnp.max(sw, axis=0, keepdims=True)).astype(BF16)
        acc_w = _dot(vwt_ref[g, :, pl.ds(wstart, span)], pw)
        o_win = acc_w[0:HEAD_DIM] / acc_w[HEAD_DIM:HEAD_DIM + 1]
        return q4, o_cmp, o_win, m0, acc0

    def scores(g, q4, c, slot):
        brow = bias_ref[g, pl.ds(pl.multiple_of(c * SEL_BLKS, SEL_BLKS), SEL_BLKS), :]
        b16 = jnp.concatenate([brow, bpad], axis=0).astype(BF16)
        qa = jnp.concatenate([q4, b16, vpad], axis=0)
        s_ref[g, slot] = _dot(ks_ref[g, key_rows(c), :], qa)

    def softmax(g, slot, m_run):
        s = s_ref[g, slot]
        m_new = jnp.maximum(m_run, jnp.max(s, axis=0, keepdims=True))
        p_ref[g, slot] = jnp.exp2(s - m_new).astype(BF16)
        return m_new, jnp.exp2(m_run - m_new)

    def values(g, c, slot, acc, alpha):
        return alpha * acc + _dot(vst_ref[g, :, key_rows(c)], p_ref[g, slot])

    heads = [head(g) for g in range(ATT_GROUPS)]

    def pair(k, carries):
        out = []
        for g in range(ATT_GROUPS):
            q4 = heads[g][0]
            m_run, alpha0, acc = carries[g]
            acc = values(g, 2 * k, 0, acc, alpha0)
            scores(g, q4, 2 * k + 2, 0)
            m_run, alpha1 = softmax(g, 1, m_run)
            acc = values(g, 2 * k + 1, 1, acc, alpha1)
            scores(g, q4, 2 * k + 3, 1)
            m_run, alpha0 = softmax(g, 0, m_run)
            out.append((m_run, alpha0, acc))
        return tuple(out)

    n_pairs = jnp.maximum((i * Q_BLK + 2 * SEL_CHUNK - 1) // (2 * SEL_CHUNK), 1)
    init = []
    for g in range(ATT_GROUPS):
        q4, _, _, m0, acc0 = heads[g]
        scores(g, q4, 0, 0)
        scores(g, q4, 1, 1)
        m_run, alpha0 = softmax(g, 0, m0)
        init.append((m_run, alpha0, acc0))
    carries = lax.fori_loop(0, n_pairs - 1, pair, tuple(init))
    last = 2 * (n_pairs - 1)
    for g in range(ATT_GROUPS):
        _, o_cmp, o_win, _, _ = heads[g]
        m_run, alpha0, acc_s = carries[g]
        acc_s = values(g, last, 0, acc_s, alpha0)
        m_run, alpha1 = softmax(g, 1, m_run)
        acc_s = values(g, last + 1, 1, acc_s, alpha1)
        o_sel = acc_s[0:HEAD_DIM] / acc_s[HEAD_DIM:HEAD_DIM + 1]
        gates = gate_ref[g]
        outs = []
        for h in range(HPG):
            cols = slice(h * Q_BLK, (h + 1) * Q_BLK)
            outs.append(o_cmp[:, cols] * gates[3 * h:3 * h + 1]
                        + o_sel[:, cols] * gates[3 * h + 1:3 * h + 2]
                        + o_win[:, cols] * gates[3 * h + 2:3 * h + 3])
        o_ref[g * rows:(g + 1) * rows, :] = jnp.concatenate(outs, axis=0).astype(BF16)


def _attention(q_t, gate_t, kc, vct, ks, vst, kw, vwt):
    nqd, s = q_t.shape
    nc = kc.shape[1]
    nb = s // L_SEL
    nq = HPG * Q_BLK
    rows = ATT_GROUPS * HPG * HEAD_DIM
    assert (s // SEL_CHUNK) % 2 == 0 and s >= WINDOW + Q_BLK and N_KV % ATT_GROUPS == 0 and nb >= N_SELECT
    qq = jnp.arange(Q_BLK)[None, :]
    rc = jnp.arange(2 * nc)[:, None]
    cmp_mask = jnp.where(D_CMP * (rc - nc) + L_CMP - 1 <= qq, 0.0, NEG).astype(F32)
    rw = jnp.arange(2 * WINDOW + Q_BLK)[:, None]
    win_mask = jnp.where((qq < rw) & (rw <= qq + WINDOW), 0.0, NEG).astype(F32)
    return pl.pallas_call(
        _attn_kernel,
        grid=(N_KV // ATT_GROUPS, s // Q_BLK),
        in_specs=[pl.BlockSpec((rows, Q_BLK), lambda g, i: (g, i)),
                  pl.BlockSpec((ATT_GROUPS, 16, Q_BLK), lambda g, i: (g, 0, i)),
                  _resident(cmp_mask.shape, lambda g, i: (0, 0)),
                  _resident(win_mask.shape, lambda g, i: (0, 0)),
                  _resident((ATT_GROUPS, nc, LANES), lambda g, i: (g, 0, 0)),
                  _resident((ATT_GROUPS, LANES, nc), lambda g, i: (g, 0, 0)),
                  _resident((ATT_GROUPS, s, LANES), lambda g, i: (g, 0, 0)),
                  _resident((ATT_GROUPS, V_ROWS, s), lambda g, i: (g, 0, 0)),
                  _resident((ATT_GROUPS, s, LANES), lambda g, i: (g, 0, 0)),
                  _resident((ATT_GROUPS, V_ROWS, s), lambda g, i: (g, 0, 0))],
        out_specs=pl.BlockSpec((rows, Q_BLK), lambda g, i: (g, i)),
        out_shape=jax.ShapeDtypeStruct((nqd, s), BF16),
        scratch_shapes=[pltpu.VMEM((ATT_GROUPS, nb, nq), F32),
                        pltpu.VMEM((ATT_GROUPS, PS_PAD + nc, Q_BLK), F32),
                        pltpu.VMEM((ATT_GROUPS, 2, SEL_CHUNK, nq), F32),
                        pltpu.VMEM((ATT_GROUPS, 2, SEL_CHUNK, nq), BF16)],
        compiler_params=_params("arbitrary", "arbitrary"),
        name="nsa_attention",
    )(q_t, gate_t, cmp_mask, win_mask, kc, vct, ks, vst, kw, vwt)


def _outproj_kernel(ot_ref, x_ref, vec_ref, w_ref, o_ref):
    vec = vec_ref[...]
    y = _dot_tn(ot_ref[...], w_ref[...])
    o_ref[...] = _post_residual(x_ref[...], y, vec[3:4], vec[4:5])


def _outproj(o_t, x, vec, w_out):
    s, d = x.shape
    nqd = o_t.shape[0]
    return pl.pallas_call(
        _outproj_kernel,
        grid=(s // ROW_TILE,),
        in_specs=[pl.BlockSpec((nqd, ROW_TILE), lambda i: (0, i)),
                  pl.BlockSpec((ROW_TILE, d), lambda i: (i, 0)),
                  _resident((8, d), lambda i: (0, 0)),
                  _resident((nqd, d), lambda i: (0, 0))],
        out_specs=pl.BlockSpec((ROW_TILE, d), lambda i: (i, 0)),
        out_shape=jax.ShapeDtypeStruct((s, d), F32),
        compiler_params=_params("arbitrary"),
        name="nsa_outproj",
    )(o_t, x, vec, w_out)


def _vec(pre_g, mod, post_g):
    d = pre_g.shape[0]
    shift, scale, gate = mod[:d], mod[d:2 * d], mod[2 * d:3 * d]
    z = jnp.zeros((d,), F32)
    return jnp.stack([pre_g, scale, shift, post_g, gate, z, z, z])


def kernel(x, c, ada_w, ada_b, norm_pre_g, norm_post_g, a_w_in, a_ln_g, a_ln_b, a_ws, a_bs, a_w_out, kv_norm_g, kv_ada_w, kv_ada_b, w_kv, cmp_pos, cmp_w1, cmp_b1, cmp_w2, cmp_b2, b_w_in, b_w_out, ffn_w_gate, ffn_w_up, ffn_w_down, moe_router, moe_router_b, moe_w_gate, moe_w_up, moe_w_down):
    batch, s, d = x.shape
    assert batch == 1 and s % SEL_CHUNK == 0 and s >= WINDOW + Q_BLK
    depth = ada_w.shape[0]
    n_a = depth // 2
    xs = x.reshape(s, d)

    mods = _ada(c, ada_w.reshape(depth * 2, d, 3 * d), ada_b.reshape(depth * 2, 3 * d)).reshape(depth, 2, 3 * d)
    kv_mod = _ada(c, kv_ada_w.reshape(1, d, 2 * d), kv_ada_b.reshape(1, 2 * d))[0]
    shared = None
    n_moe, _, _, f_moe = moe_w_gate.shape
    moe_wg = moe_w_gate.astype(BF16).reshape(n_moe * N_EXPERTS, d, f_moe)
    moe_wu = moe_w_up.astype(BF16).reshape(n_moe * N_EXPERTS, d, f_moe)
    moe_wd = moe_w_down.astype(BF16).reshape(n_moe * N_EXPERTS, f_moe, d)

    for layer in range(depth):
        vec = _vec(norm_pre_g[layer, 0], mods[layer, 0], norm_post_g[layer, 0])
        if layer < n_a:
            xs = _gmlp_layer(xs, vec, a_w_in[layer].astype(BF16),
                             jnp.stack([a_ln_g[layer], a_ln_b[layer]]),
                             a_ws[layer], a_bs[layer].T, a_w_out[layer].astype(BF16))
        else:
            if shared is None:
                kv_vec = _vec(kv_norm_g, jnp.concatenate([kv_mod, jnp.zeros((d,), F32)]), jnp.zeros((d,), F32))
                gd = N_KV * HEAD_DIM
                w_vt = jnp.concatenate([w_kv[:, 3 * gd:4 * gd], w_kv[:, 5 * gd:6 * gd]], axis=1).T
                raw, ks, kw, vst, vwt = _kvproj(xs, kv_vec, w_kv.astype(BF16), w_vt.astype(BF16))
                nc = s // D_CMP
                width = D_CMP * HEAD_DIM
                hid = cmp_w1.shape[-1]
                w2p = jnp.pad(cmp_w2, ((0, 0), (0, 0), (0, LANES - HEAD_DIM))).astype(BF16)
                b2p = jnp.pad(cmp_b2, ((0, 0), (0, LANES - HEAD_DIM))).reshape(2, 1, LANES)
                cmp_n, cmp_t = _compress(raw.reshape(2, N_KV, nc, width),
                                         cmp_pos.reshape(2, 2, 1, width),
                                         cmp_w1.reshape(2, 2, width, hid).astype(BF16),
                                         cmp_b1.reshape(2, 1, hid), w2p, b2p)
                shared = (cmp_n[0], cmp_t[1], ks, vst, kw, vwt)
            i = layer - n_a
            nq = N_HEADS * HEAD_DIM
            w_t = b_w_in[i].T.astype(BF16)
            q_t, gate_t = _qproj(xs, vec, w_t)
            kc, vct, ks, vst, kw, vwt = shared
            o_t = _attention(q_t, gate_t, kc, vct, ks, vst, kw, vwt)
            xs = _outproj(o_t, xs, vec, b_w_out[i].astype(BF16))

        vec = _vec(norm_pre_g[layer, 1], mods[layer, 1], norm_post_g[layer, 1])
        j = layer // 2
        if layer % 2 == 0:
            xs = _swiglu_layer(xs, vec, ffn_w_gate[j].astype(BF16), ffn_w_up[j].astype(BF16),
                               ffn_w_down[j].astype(BF16))
        else:
            xs = _moe_layer(xs, vec, moe_router[j].T, moe_router_b[j].reshape(N_EXPERTS, 1),
                            moe_wg, moe_wu, moe_wd, j * N_EXPERTS)
    return xs.reshape(batch, s, d)
```

```python
import jax
import jax.numpy as jnp
from jax import lax
from jax.experimental import pallas as pl
from jax.experimental.pallas import tpu as pltpu

F32 = jnp.float32
BF16 = jnp.bfloat16
I32 = jnp.int32

EPS = 1e-6
NEG = -1e30

LANES = 128
VMEM_LIMIT_BYTES = 56 * 1024 * 1024

CHUNK = 128
A_GROUPS = 8
N_HEADS = 16
N_KV = 4
HPG = N_HEADS // N_KV
HEAD_DIM = 64
L_CMP = 32
D_CMP = 16
L_SEL = 64
SEL_SHIFT = 6
N_SELECT = 16
N_FORCED = 3
WINDOW = 512
Q_BLK = 128
N_EXPERTS = 8
MOE_BLOCK = 256

ROW_TILE = 256
TOK_TILE = 256
SEL_CHUNK = 512
SEL_BLKS = SEL_CHUNK // L_SEL
CMP_CHUNK = 128
ATT_GROUPS = 2
PS_PAD = 8
V_ROWS = 72
F_CHUNK = 512
WIN_PART_ROWS = 128
WIN_PARTS = TOK_TILE // WIN_PART_ROWS + 1
WIN_ROWS = WIN_PARTS * WIN_PART_ROWS
LOG2E = 1.4426950408889634


def _params(*sem):
    return pltpu.CompilerParams(dimension_semantics=sem, vmem_limit_bytes=VMEM_LIMIT_BYTES)


def _resident(shape, index_map):
    return pl.BlockSpec(shape, index_map, pipeline_mode=pl.Buffered(1))


def _split_bf16(a):
    hi = a.astype(BF16)
    lo = (a - hi.astype(F32)).astype(BF16)
    return hi, lo


def _dot(a, b):
    return jnp.dot(a, b, preferred_element_type=F32)


def _dot_nt(a, b):
    return lax.dot_general(a, b, (((1,), (1,)), ((), ())), preferred_element_type=F32)


def _dot_tn(a, b):
    return lax.dot_general(a, b, (((0,), (0,)), ((), ())), preferred_element_type=F32)


def _prenorm(x, g, scale, shift):
    ms = jnp.mean(x * x, axis=-1, keepdims=True)
    return (x * lax.rsqrt(ms + EPS) * g) * (1.0 + scale) + shift


def _post_residual(x, y, g, gate):
    ms = jnp.mean(y * y, axis=-1, keepdims=True)
    return x + gate * (y * lax.rsqrt(ms + EPS) * g)


def _ada_kernel(c_ref, w_ref, b_ref, o_ref):
    c = c_ref[...]
    c_act = jnp.broadcast_to(c * jax.nn.sigmoid(c), (8, c.shape[1]))
    c_hi, c_lo = _split_bf16(c_act)
    w_hi, w_lo = _split_bf16(w_ref[...])
    m = _dot(c_hi, w_hi) + (_dot(c_hi, w_lo) + _dot(c_lo, w_hi))
    o_ref[...] = m[0:1] + b_ref[...]


def _ada(c, w, b):
    n, d, nn = w.shape
    out = pl.pallas_call(
        _ada_kernel,
        grid=(n, nn // d),
        in_specs=[pl.BlockSpec((1, d), lambda i, j: (0, 0)),
                  pl.BlockSpec((None, d, d), lambda i, j: (i, 0, j)),
                  pl.BlockSpec((None, 1, d), lambda i, j: (i, 0, j))],
        out_specs=pl.BlockSpec((None, 1, d), lambda i, j: (i, 0, j)),
        out_shape=jax.ShapeDtypeStruct((n, 1, nn), F32),
        compiler_params=_params("arbitrary", "arbitrary"),
        name="ada",
    )(c, w, b.reshape(n, 1, nn))
    return out.reshape(n, nn)


def _gmlp_kernel(x_ref, vec_ref, win_ref, ln_ref, ws_ref, bst_ref, wout_ref, o_ref, gated_ref):
    x = x_ref[...]
    vec = vec_ref[...]
    h = _prenorm(x, vec[0:1], vec[1:2], vec[2:3]).astype(BF16)
    z = jax.nn.gelu(_dot(h, win_ref[...]))
    width = z.shape[1] // 2
    u = z[:, :width]
    v = z[:, width:]
    mu = jnp.mean(v, axis=-1, keepdims=True)
    vc = v - mu
    var = jnp.mean(vc * vc, axis=-1, keepdims=True)
    ln = ln_ref[...]
    vn = (vc * lax.rsqrt(var + EPS) * ln[0:1] + ln[1:2]).astype(BF16)
    causal = (lax.broadcasted_iota(I32, (CHUNK, CHUNK), 0)
              >= lax.broadcasted_iota(I32, (CHUNK, CHUNK), 1))
    gw = width // A_GROUPS
    bst = bst_ref[...]
    for g in range(A_GROUPS):
        wg = jnp.where(causal, ws_ref[g], 0.0).astype(BF16)
        for ck in range(x.shape[0] // CHUNK):
            rows = slice(ck * CHUNK, (ck + 1) * CHUNK)
            cols = slice(g * gw, (g + 1) * gw)
            mixed = _dot(wg, vn[rows, cols]) + bst[:, g:g + 1]
            gated_ref[rows, cols] = (u[rows, cols] * mixed).astype(BF16)
    y = _dot(gated_ref[...], wout_ref[...])
    o_ref[...] = _post_residual(x, y, vec[3:4], vec[4:5])


def _gmlp_layer(x, vec, w_in, ln, ws, bst, w_out):
    s, d = x.shape
    e2 = w_in.shape[1]
    return pl.pallas_call(
        _gmlp_kernel,
        grid=(s // ROW_TILE,),
        in_specs=[pl.BlockSpec((ROW_TILE, d), lambda i: (i, 0)),
                  _resident((8, d), lambda i: (0, 0)),
                  _resident((d, e2), lambda i: (0, 0)),
                  _resident((2, e2 // 2), lambda i: (0, 0)),
                  _resident(ws.shape, lambda i: (0, 0, 0)),
                  _resident(bst.shape, lambda i: (0, 0)),
                  _resident((e2 // 2, d), lambda i: (0, 0))],
        out_specs=pl.BlockSpec((ROW_TILE, d), lambda i: (i, 0)),
        out_shape=jax.ShapeDtypeStruct((s, d), F32),
        scratch_shapes=[pltpu.VMEM((ROW_TILE, e2 // 2), BF16)],
        compiler_params=_params("arbitrary"),
        name="gmlp",
    )(x, vec, w_in, ln, ws, bst, w_out)


def _swiglu_kernel(x_ref, vec_ref, wg_ref, wu_ref, wd_ref, o_ref):
    x = x_ref[...]
    vec = vec_ref[...]
    h = _prenorm(x, vec[0:1], vec[1:2], vec[2:3]).astype(BF16)
    g = _dot(h, wg_ref[...])
    a = (g * jax.nn.sigmoid(g) * _dot(h, wu_ref[...])).astype(BF16)
    y = _dot(a, wd_ref[...])
    o_ref[...] = _post_residual(x, y, vec[3:4], vec[4:5])


def _swiglu_layer(x, vec, w_gate, w_up, w_down):
    s, d = x.shape
    f = w_gate.shape[1]
    return pl.pallas_call(
        _swiglu_kernel,
        grid=(s // ROW_TILE,),
        in_specs=[pl.BlockSpec((ROW_TILE, d), lambda i: (i, 0)),
                  _resident((8, d), lambda i: (0, 0)),
                  _resident((d, f), lambda i: (0, 0)),
                  _resident((d, f), lambda i: (0, 0)),
                  _resident((f, d), lambda i: (0, 0))],
        out_specs=pl.BlockSpec((ROW_TILE, d), lambda i: (i, 0)),
        out_shape=jax.ShapeDtypeStruct((s, d), F32),
        compiler_params=_params("arbitrary"),
        name="swiglu",
    )(x, vec, w_gate, w_up, w_down)


def _router_kernel(x_ref, vec_ref, wrt_ref, br_ref, h_ref, idx_ref, gate_ref):
    x = x_ref[...]
    vec = vec_ref[...]
    h = _prenorm(x, vec[0:1], vec[1:2], vec[2:3])
    h_ref[...] = h.astype(BF16)
    h_hi, h_lo = _split_bf16(h)
    w_hi, w_lo = _split_bf16(wrt_ref[...])
    logit = _dot_nt(w_hi, h_hi) + (_dot_nt(w_hi, h_lo) + _dot_nt(w_lo, h_hi)) + br_ref[...]
    ne = logit.shape[0]
    eidx = lax.broadcasted_iota(I32, logit.shape, 0)
    m1 = jnp.max(logit, axis=0, keepdims=True)
    i1 = jnp.min(jnp.where(logit == m1, eidx, ne), axis=0, keepdims=True)
    rest = jnp.where(eidx == i1, -jnp.inf, logit)
    m2 = jnp.max(rest, axis=0, keepdims=True)
    i2 = jnp.min(jnp.where(rest == m2, eidx, ne), axis=0, keepdims=True)
    e21 = jnp.exp(m2 - m1)
    g1 = 1.0 / (1.0 + e21)
    g2 = e21 * g1
    pad_i = jnp.zeros((6, i1.shape[1]), I32)
    pad_f = jnp.zeros((6, i1.shape[1]), F32)
    idx_ref[...] = jnp.concatenate([i1, i2, pad_i], axis=0)
    gate_ref[...] = jnp.concatenate([g1, g2, pad_f], axis=0)


def _router(x, vec, w_rt, b_r):
    s, d = x.shape
    ne = w_rt.shape[0]
    return pl.pallas_call(
        _router_kernel,
        grid=(s // ROW_TILE,),
        in_specs=[pl.BlockSpec((ROW_TILE, d), lambda i: (i, 0)),
                  _resident((8, d), lambda i: (0, 0)),
                  _resident((ne, d), lambda i: (0, 0)),
                  _resident((ne, 1), lambda i: (0, 0))],
        out_specs=[pl.BlockSpec((ROW_TILE, d), lambda i: (i, 0)),
                   pl.BlockSpec((8, ROW_TILE), lambda i: (0, i)),
                   pl.BlockSpec((8, ROW_TILE), lambda i: (0, i))],
        out_shape=[jax.ShapeDtypeStruct((s, d), BF16),
                   jax.ShapeDtypeStruct((8, s), I32),
                   jax.ShapeDtypeStruct((8, s), F32)],
        compiler_params=_params("arbitrary"),
        name="moe_router",
    )(x, vec, w_rt, b_r)


def _rank_kernel(idx_ref, rank_ref, start_ref, count_ref, carry_ref):
    @pl.when(pl.program_id(0) == 0)
    def _():
        carry_ref[...] = jnp.zeros_like(carry_ref)

    idx = idx_ref[...]
    tt = idx.shape[1]
    eidx = lax.broadcasted_iota(I32, (N_EXPERTS, tt), 0)
    hit1 = eidx == idx[0:1]
    hit2 = eidx == idx[1:2]
    member = jnp.where(hit1 | hit2, 1.0, 0.0)
    before = (lax.broadcasted_iota(I32, (tt, tt), 0) < lax.broadcasted_iota(I32, (tt, tt), 1))
    carry = carry_ref[...]
    cum = _dot(member.astype(BF16), jnp.where(before, 1.0, 0.0).astype(BF16)) + carry[:, 0:1]
    r1 = jnp.sum(jnp.where(hit1, cum, 0.0), axis=0, keepdims=True)
    r2 = jnp.sum(jnp.where(hit2, cum, 0.0), axis=0, keepdims=True)
    rank_ref[...] = jnp.concatenate([r1, r2, jnp.zeros((6, tt), F32)], axis=0).astype(I32)
    tile_count = jnp.broadcast_to(jnp.sum(member, axis=1, keepdims=True), carry.shape)
    start_ref[...] = carry
    count_ref[...] = tile_count
    carry_ref[...] = carry + tile_count


def _ranks(idx):
    s = idx.shape[1]
    nt = s // TOK_TILE
    return pl.pallas_call(
        _rank_kernel,
        grid=(nt,),
        in_specs=[pl.BlockSpec((8, TOK_TILE), lambda i: (0, i))],
        out_specs=[pl.BlockSpec((8, TOK_TILE), lambda i: (0, i)),
                   pl.BlockSpec((None, N_EXPERTS, LANES), lambda i: (i, 0, 0)),
                   pl.BlockSpec((None, N_EXPERTS, LANES), lambda i: (i, 0, 0))],
        out_shape=[jax.ShapeDtypeStruct((8, s), I32),
                   jax.ShapeDtypeStruct((nt, N_EXPERTS, LANES), F32),
                   jax.ShapeDtypeStruct((nt, N_EXPERTS, LANES), F32)],
        scratch_shapes=[pltpu.VMEM((N_EXPERTS, LANES), F32)],
        compiler_params=_params("arbitrary"),
        name="moe_ranks",
    )(idx)


def _match(idx, rank, expert, rows):
    r1 = jnp.where(idx[0:1] == expert, rank[0:1], -1)
    r2 = jnp.where(idx[1:2] == expert, rank[1:2], -1)
    return rows == r1, rows == r2


def _dispatch_kernel(be_ref, lb_ref, tlo_ref, thi_ref, idx_ref, rank_ref, gate_ref, h_ref, o_ref, rg_ref,
                     acc_ref, gacc_ref):
    b = pl.program_id(0)
    expert = be_ref[b]
    rows = lax.broadcasted_iota(I32, (MOE_BLOCK, TOK_TILE), 0) + lb_ref[b]
    acc_ref[...] = jnp.zeros_like(acc_ref)
    gacc_ref[...] = jnp.zeros_like(gacc_ref)

    def body(t, carry):
        off = pl.multiple_of(t * TOK_TILE, TOK_TILE)
        m1, m2 = _match(idx_ref[:, pl.ds(off, TOK_TILE)], rank_ref[:, pl.ds(off, TOK_TILE)], expert, rows)
        gates = gate_ref[:, pl.ds(off, TOK_TILE)]
        gacc_ref[...] += jnp.sum(jnp.where(m1, gates[0:1], 0.0) + jnp.where(m2, gates[1:2], 0.0),
                                 axis=1, keepdims=True)
        onehot = jnp.where(m1 | m2, 1.0, 0.0).astype(BF16)
        acc_ref[...] += _dot(onehot, h_ref[pl.ds(off, TOK_TILE), :])
        return carry

    lax.fori_loop(tlo_ref[b], thi_ref[b], body, 0)
    o_ref[...] = acc_ref[...].astype(BF16)
    rg_ref[...] = gacc_ref[...]


def _dispatch(blk_e, blk_lb, blk_tlo, blk_thi, idx, rank, gates, h):
    s, d = h.shape
    n_blk = blk_e.shape[0]
    grid_spec = pltpu.PrefetchScalarGridSpec(
        num_scalar_prefetch=4,
        grid=(n_blk,),
        in_specs=[_resident((8, s), lambda b, *_: (0, 0)),
                  _resident((8, s), lambda b, *_: (0, 0)),
                  _resident((8, s), lambda b, *_: (0, 0)),
                  _resident((s, d), lambda b, *_: (0, 0))],
        out_specs=[pl.BlockSpec((MOE_BLOCK, d), lambda b, *_: (b, 0)),
                   pl.BlockSpec((MOE_BLOCK, 1), lambda b, *_: (b, 0))],
        scratch_shapes=[pltpu.VMEM((MOE_BLOCK, d), F32), pltpu.VMEM((MOE_BLOCK, 1), F32)],
    )
    return pl.pallas_call(
        _dispatch_kernel,
        grid_spec=grid_spec,
        out_shape=[jax.ShapeDtypeStruct((n_blk * MOE_BLOCK, d), BF16),
                   jax.ShapeDtypeStruct((n_blk * MOE_BLOCK, 1), F32)],
        compiler_params=_params("arbitrary"),
        name="moe_dispatch",
    )(blk_e, blk_lb, blk_tlo, blk_thi, idx, rank, gates, h)


def _expert_kernel(be_ref, used_ref, x_ref, rg_ref, wg_ref, wu_ref, wd_ref, o_ref):
    b = pl.program_id(0)

    @pl.when(b < used_ref[0])
    def _():
        x = x_ref[...]
        f = wg_ref.shape[1]
        y = jnp.zeros(o_ref.shape, F32)
        for c in range(f // F_CHUNK):
            cols = slice(c * F_CHUNK, (c + 1) * F_CHUNK)
            g = _dot(x, wg_ref[:, cols])
            a = (g * jax.nn.sigmoid(g) * _dot(x, wu_ref[:, cols])).astype(BF16)
            y = y + _dot(a, wd_ref[cols, :])
        o_ref[...] = (y * rg_ref[...]).astype(BF16)

    @pl.when(b >= used_ref[0])
    def _():
        o_ref[...] = jnp.zeros_like(o_ref)


def _experts(blk_e, n_used, x_buf, row_gate, w_gate, w_up, w_down):
    n_rows, d = x_buf.shape
    f = w_gate.shape[2]
    assert f % F_CHUNK == 0
    grid_spec = pltpu.PrefetchScalarGridSpec(
        num_scalar_prefetch=2,
        grid=(n_rows // MOE_BLOCK,),
        in_specs=[pl.BlockSpec((MOE_BLOCK, d), lambda b, be, nu: (b, 0)),
                  pl.BlockSpec((MOE_BLOCK, 1), lambda b, be, nu: (b, 0)),
                  _resident((None, d, f), lambda b, be, nu: (be[b], 0, 0)),
                  _resident((None, d, f), lambda b, be, nu: (be[b], 0, 0)),
                  _resident((None, f, d), lambda b, be, nu: (be[b], 0, 0))],
        out_specs=pl.BlockSpec((MOE_BLOCK, d), lambda b, be, nu: (b, 0)),
    )
    return pl.pallas_call(
        _expert_kernel,
        grid_spec=grid_spec,
        out_shape=jax.ShapeDtypeStruct((n_rows, d), BF16),
        compiler_params=_params("arbitrary"),
        name="moe_experts",
    )(blk_e, n_used, x_buf, row_gate, w_gate, w_up, w_down)


def _combine_kernel(win_ref, lb_ref, idx_ref, rank_ref, *refs):
    y_refs = refs[:N_EXPERTS * WIN_PARTS]
    x_ref, vec_ref, o_ref = refs[N_EXPERTS * WIN_PARTS:]
    t = pl.program_id(0)
    idx = idx_ref[...]
    rank = rank_ref[...]
    acc = jnp.zeros(o_ref.shape, F32)
    for e in range(N_EXPERTS):
        rows = lax.broadcasted_iota(I32, (WIN_ROWS, TOK_TILE), 0) + lb_ref[t * N_EXPERTS + e]
        m1, m2 = _match(idx, rank, e, rows)
        onehot = jnp.where(m1 | m2, 1.0, 0.0).astype(BF16)
        y = jnp.concatenate([y_refs[e * WIN_PARTS + k][...] for k in range(WIN_PARTS)], axis=0)
        acc = acc + _dot_tn(onehot, y)
    vec = vec_ref[...]
    o_ref[...] = _post_residual(x_ref[...], acc, vec[3:4], vec[4:5])


def _combine(win_start, win_lb, idx, rank, y_buf, x, vec):
    s, d = x.shape

    def y_spec(e, k):
        return pl.BlockSpec((WIN_PART_ROWS, d), lambda t, ws, lb: (ws[t * N_EXPERTS + e] + k, 0))

    grid_spec = pltpu.PrefetchScalarGridSpec(
        num_scalar_prefetch=2,
        grid=(s // TOK_TILE,),
        in_specs=[pl.BlockSpec((8, TOK_TILE), lambda t, *_: (0, t)),
                  pl.BlockSpec((8, TOK_TILE), lambda t, *_: (0, t))]
                 + [y_spec(e, k) for e in range(N_EXPERTS) for k in range(WIN_PARTS)]
                 + [pl.BlockSpec((TOK_TILE, d), lambda t, *_: (t, 0)),
                    _resident((8, d), lambda t, *_: (0, 0))],
        out_specs=pl.BlockSpec((TOK_TILE, d), lambda t, *_: (t, 0)),
    )
    return pl.pallas_call(
        _combine_kernel,
        grid_spec=grid_spec,
        out_shape=jax.ShapeDtypeStruct((s, d), F32),
        compiler_params=_params("arbitrary"),
        name="moe_combine",
    )(win_start, win_lb, idx, rank, *([y_buf] * (N_EXPERTS * WIN_PARTS)), x, vec)


def _moe_layer(x, vec, w_rt, b_r, w_gate, w_up, w_down, w_first):
    s, d = x.shape
    nt = s // TOK_TILE
    h, idx, gates = _router(x, vec, w_rt, b_r)
    rank, tile_start, tile_count = _ranks(idx)

    tile_start = tile_start[:, :, 0].astype(I32)
    tile_count = tile_count[:, :, 0].astype(I32)
    tile_end = tile_start + tile_count
    counts = tile_end[-1]
    padded = (counts + MOE_BLOCK - 1) // MOE_BLOCK * MOE_BLOCK
    pend = jnp.cumsum(padded)
    pstart = pend - padded
    n_rows = -(-(2 * s) // MOE_BLOCK) * MOE_BLOCK + N_EXPERTS * MOE_BLOCK
    n_blk = n_rows // MOE_BLOCK
    blk_row = jnp.arange(n_blk, dtype=I32) * MOE_BLOCK
    blk_e = jnp.minimum(jnp.sum(blk_row[:, None] >= pend[None, :], axis=1), N_EXPERTS - 1).astype(I32)
    blk_lb = blk_row - pstart[blk_e]
    te = tile_end[:, blk_e]
    ts = tile_start[:, blk_e]
    blk_tlo = jnp.sum(te <= blk_lb[None, :], axis=0).astype(I32)
    blk_thi = jnp.sum(ts < (blk_lb + MOE_BLOCK)[None, :], axis=0).astype(I32)
    n_used = (pend[-1] // MOE_BLOCK).astype(I32).reshape(1)

    x_buf, row_gate = _dispatch(blk_e, blk_lb.astype(I32), blk_tlo, blk_thi, idx, rank, gates, h)
    y_buf = _experts(blk_e + w_first, n_used, x_buf, row_gate, w_gate, w_up, w_down)

    win_start = jnp.minimum((pstart[None, :] + tile_start) // WIN_PART_ROWS, n_rows // WIN_PART_ROWS - WIN_PARTS)
    win_lb = win_start * WIN_PART_ROWS - pstart[None, :]
    return _combine(win_start.reshape(-1).astype(I32), win_lb.reshape(-1).astype(I32), idx, rank, y_buf, x, vec)


def _kvproj_kernel(x_ref, vec_ref, wkv_ref, wvt_ref, raw_ref, ks_ref, kw_ref, vst_ref, vwt_ref):
    x = x_ref[...]
    vec = vec_ref[...]
    h = _prenorm(x, vec[0:1], vec[1:2], vec[2:3]).astype(BF16)
    kv = _dot(h, wkv_ref[...])
    vt = _dot_nt(wvt_ref[...], h)
    tm = x.shape[0]
    gd = N_KV * HEAD_DIM
    key_blk = jnp.right_shift(pl.program_id(0) * tm + lax.broadcasted_iota(I32, (tm, SEL_BLKS), 0), SEL_SHIFT)
    ind = jnp.where((key_blk & (SEL_BLKS - 1)) == lax.broadcasted_iota(I32, (tm, SEL_BLKS), 1), 1.0, 0.0)
    pad_s = jnp.zeros((tm, LANES - HEAD_DIM - SEL_BLKS), F32)
    pad_w = jnp.zeros((tm, LANES - HEAD_DIM), F32)
    ones_row = jnp.concatenate([jnp.ones((1, tm), F32), jnp.zeros((V_ROWS - HEAD_DIM - 1, tm), F32)], axis=0)
    for g in range(N_KV):
        c = g * HEAD_DIM
        raw_ref[0, g] = kv[:, c:c + HEAD_DIM]
        raw_ref[1, g] = kv[:, gd + c:gd + c + HEAD_DIM]
        ks = kv[:, 2 * gd + c:2 * gd + c + HEAD_DIM]
        kw = kv[:, 4 * gd + c:4 * gd + c + HEAD_DIM]
        ks_ref[g] = jnp.concatenate([ks, ind, pad_s], axis=1).astype(BF16)
        kw_ref[g] = jnp.concatenate([kw, pad_w], axis=1).astype(BF16)
        vst_ref[g] = jnp.concatenate([vt[c:c + HEAD_DIM], ones_row], axis=0).astype(BF16)
        vwt_ref[g] = jnp.concatenate([vt[gd + c:gd + c + HEAD_DIM], ones_row], axis=0).astype(BF16)


def _kvproj(x, vec, w_kv, w_vt):
    s, d = x.shape
    nkv = w_kv.shape[1]
    return pl.pallas_call(
        _kvproj_kernel,
        grid=(s // ROW_TILE,),
        in_specs=[pl.BlockSpec((ROW_TILE, d), lambda i: (i, 0)),
                  _resident((8, d), lambda i: (0, 0)),
                  _resident((d, nkv), lambda i: (0, 0)),
                  _resident(w_vt.shape, lambda i: (0, 0))],
        out_specs=[pl.BlockSpec((2, N_KV, ROW_TILE, HEAD_DIM), lambda i: (0, 0, i, 0)),
                   pl.BlockSpec((N_KV, ROW_TILE, LANES), lambda i: (0, i, 0)),
                   pl.BlockSpec((N_KV, ROW_TILE, LANES), lambda i: (0, i, 0)),
                   pl.BlockSpec((N_KV, V_ROWS, ROW_TILE), lambda i: (0, 0, i)),
                   pl.BlockSpec((N_KV, V_ROWS, ROW_TILE), lambda i: (0, 0, i))],
        out_shape=[jax.ShapeDtypeStruct((2, N_KV, s, HEAD_DIM), F32),
                   jax.ShapeDtypeStruct((N_KV, s, LANES), BF16),
                   jax.ShapeDtypeStruct((N_KV, s, LANES), BF16),
                   jax.ShapeDtypeStruct((N_KV, V_ROWS, s), BF16),
                   jax.ShapeDtypeStruct((N_KV, V_ROWS, s), BF16)],
        compiler_params=_params("arbitrary"),
        name="nsa_kvproj",
    )(x, vec, w_kv, w_vt)


def _compress_kernel(raw_ref, pos_ref, w1_ref, b1_ref, w2_ref, b2_ref, n_ref, t_ref):
    raw = raw_ref[...]
    nc = raw.shape[0]
    first = _dot((raw + pos_ref[0]).astype(BF16), w1_ref[0])
    second = _dot((raw + pos_ref[1]).astype(BF16), w1_ref[1])
    hid = jax.nn.gelu(first + pltpu.roll(second, nc - 1, 0) + b1_ref[...])
    out = _dot(hid.astype(BF16), w2_ref[...]) + b2_ref[...]
    n_ref[...] = out.astype(BF16)
    row = lax.broadcasted_iota(I32, (LANES, nc), 0)
    t_ref[...] = jnp.where(row == HEAD_DIM, 1.0, out.T).astype(BF16)


def _compress(raw, pos, w1, b1, w2, b2):
    _, g, nc, width = raw.shape
    hid = w1.shape[-1]
    return pl.pallas_call(
        _compress_kernel,
        grid=(2, g),
        in_specs=[pl.BlockSpec((None, None, nc, width), lambda j, k: (j, k, 0, 0)),
                  pl.BlockSpec((None, 2, 1, width), lambda j, k: (j, 0, 0, 0)),
                  pl.BlockSpec((None, 2, width, hid), lambda j, k: (j, 0, 0, 0)),
                  pl.BlockSpec((None, 1, hid), lambda j, k: (j, 0, 0)),
                  pl.BlockSpec((None, hid, LANES), lambda j, k: (j, 0, 0)),
                  pl.BlockSpec((None, 1, LANES), lambda j, k: (j, 0, 0))],
        out_specs=[pl.BlockSpec((None, None, nc, LANES), lambda j, k: (j, k, 0, 0)),
                   pl.BlockSpec((None, None, LANES, nc), lambda j, k: (j, k, 0, 0))],
        out_shape=[jax.ShapeDtypeStruct((2, g, nc, LANES), BF16),
                   jax.ShapeDtypeStruct((2, g, LANES, nc), BF16)],
        compiler_params=_params("arbitrary", "arbitrary"),
        name="nsa_compress",
    )(raw, pos, w1, b1, w2, b2)


def _qproj_kernel(x_ref, vec_ref, wt_ref, q_ref, gate_ref):
    x = x_ref[...]
    vec = vec_ref[...]
    h = _prenorm(x, vec[0:1], vec[1:2], vec[2:3]).astype(BF16)
    pt = _dot_nt(wt_ref[...], h)
    nq = N_HEADS * HEAD_DIM
    q_ref[...] = (pt[:nq] * (HEAD_DIM ** -0.5 * LOG2E)).astype(BF16)
    gates = jax.nn.sigmoid(pt[nq:nq + 3 * N_HEADS])
    per = 3 * HPG
    pad = jnp.zeros((16 - per, x.shape[0]), F32)
    for g in range(N_KV):
        gate_ref[g] = jnp.concatenate([gates[g * per:(g + 1) * per], pad], axis=0)


def _qproj(x, vec, w_t):
    s, d = x.shape
    return pl.pallas_call(
        _qproj_kernel,
        grid=(s // ROW_TILE,),
        in_specs=[pl.BlockSpec((ROW_TILE, d), lambda i: (i, 0)),
                  _resident((8, d), lambda i: (0, 0)),
                  _resident(w_t.shape, lambda i: (0, 0))],
        out_specs=[pl.BlockSpec((N_HEADS * HEAD_DIM, ROW_TILE), lambda i: (0, i)),
                   pl.BlockSpec((N_KV, 16, ROW_TILE), lambda i: (0, 0, i))],
        out_shape=[jax.ShapeDtypeStruct((N_HEADS * HEAD_DIM, s), BF16),
                   jax.ShapeDtypeStruct((N_KV, 16, s), F32)],
        compiler_params=_params("arbitrary"),
        name="nsa_qproj",
    )(x, vec, w_t)


def _attn_kernel(q_ref, gate_ref, cmask_ref, wmask_ref, kc_ref, vct_ref, ks_ref, vst_ref, kw_ref, vwt_ref,
                 o_ref, bias_ref, ps_ref, sc_ref, *sp_refs):
    s_refs = [[sp_refs[2 * g + k] for k in range(2)] for g in range(ATT_GROUPS)]
    p_refs = [[sp_refs[2 * ATT_GROUPS + 2 * g + k] for k in range(2)] for g in range(ATT_GROUPS)]
    i = pl.program_id(1)
    nq = HPG * Q_BLK
    nc = kc_ref.shape[1]
    nb = bias_ref.shape[1]
    rows = HPG * HEAD_DIM
    t1 = i * Q_BLK + lax.broadcasted_iota(I32, (1, Q_BLK), 1)
    t4 = jnp.concatenate([t1] * HPG, axis=1)
    j_io = lax.broadcasted_iota(I32, (nb, Q_BLK), 0)
    jt = jnp.right_shift(t1, SEL_SHIFT)
    forced = (j_io == 0) | (j_io == jt) | (j_io == jt - 1)
    cand_off = jnp.where((j_io * L_SEL <= t1) & jnp.logical_not(forced), 0.0, NEG)
    past_off = jnp.where(j_io * L_SEL < i * Q_BLK, 0.0, NEG)
    doff = pl.multiple_of(i * Q_BLK, Q_BLK)
    cmask_off = nc - i * (Q_BLK // D_CMP)
    span = WINDOW + Q_BLK
    wstart = pl.multiple_of(jnp.maximum(i * Q_BLK - WINDOW, 0), Q_BLK)
    win_mask = wmask_ref[pl.ds(pl.multiple_of(WINDOW - jnp.minimum(i * Q_BLK, WINDOW), Q_BLK), span), :]
    win_mask = jnp.concatenate([win_mask] * HPG, axis=1)
    diag_mask = jnp.concatenate([wmask_ref[WINDOW:WINDOW + Q_BLK, :]] * HPG, axis=1)
    vpad = jnp.zeros((LANES - HEAD_DIM - 2 * SEL_BLKS, nq), BF16)
    bpad = jnp.zeros((SEL_BLKS, nq), F32)

    def aligned(x, m):
        return x if isinstance(x, int) else pl.multiple_of(x, m)

    def head(g):
        qb = q_ref[g * rows:(g + 1) * rows, :]
        q4 = jnp.concatenate([qb[h * HEAD_DIM:(h + 1) * HEAD_DIM] for h in range(HPG)], axis=1)
        q_plain = jnp.concatenate([q4, jnp.zeros((LANES - HEAD_DIM, nq), BF16)], axis=0)

        def compressed(n_rows):
            chunks = [slice(r0, r0 + CMP_CHUNK) for r0 in range(0, n_rows, CMP_CHUNK)]
            m = jnp.full((1, nq), NEG, F32)
            for rs in chunks:
                mask = cmask_ref[pl.ds(pl.multiple_of(cmask_off + rs.start, 8), CMP_CHUNK), :]
                s = _dot(kc_ref[g, rs, :], q_plain) + jnp.concatenate([mask] * HPG, axis=1)
                sc_ref[g, rs, :] = s
                m = jnp.maximum(m, jnp.max(s, axis=0, keepdims=True))
            l = jnp.zeros((1, nq), F32)
            for rs in chunks:
                e = jnp.exp2(sc_ref[g, rs, :] - m)
                sc_ref[g, rs, :] = e
                l = l + jnp.sum(e, axis=0, keepdims=True)
            rl = jnp.where(t4 >= L_CMP - 1, 1.0 / l, 0.0)
            o = jnp.zeros((HEAD_DIM, nq), F32)
            ps_ref[g, 0:PS_PAD, :] = jnp.zeros((PS_PAD, Q_BLK), F32)
            for rs in chunks:
                pc = sc_ref[g, rs, :] * rl
                o = o + _dot(vct_ref[g, 0:HEAD_DIM, rs], pc.astype(BF16))
                psum = pc[:, 0:Q_BLK]
                for h in range(1, HPG):
                    psum = psum + pc[:, h * Q_BLK:(h + 1) * Q_BLK]
                ps_ref[g, PS_PAD + rs.start:PS_PAD + rs.stop, :] = psum
            if n_rows < nc:
                ps_ref[g, PS_PAD + n_rows:PS_PAD + nc, :] = jnp.zeros((nc - n_rows, Q_BLK), F32)
            ratio = L_SEL // D_CMP
            imp = jnp.zeros((nb, Q_BLK), F32)
            for k in range(1 - L_CMP // D_CMP, ratio):
                imp = imp + ps_ref[g, pl.ds(PS_PAD + k, nb, stride=ratio), :]
            return o, imp

        if nc % (2 * CMP_CHUNK) == 0:
            last_complete = ((i + 1) * Q_BLK - L_CMP) // D_CMP
            o_cmp, imp = lax.cond(last_complete < nc // 2,
                                  lambda: compressed(nc // 2), lambda: compressed(nc))
        else:
            o_cmp, imp = compressed(nc)

        return q4, q_plain, o_cmp, imp

    def select(imps):
        ws = [imp + cand_off for imp in imps]
        for _ in range(N_SELECT - N_FORCED):
            for g in range(ATT_GROUPS):
                w = ws[g]
                m = jnp.max(w, axis=0, keepdims=True)
                first = jnp.min(jnp.where(w == m, j_io, nb), axis=0, keepdims=True)
                first = jnp.where(m > 0.5 * NEG, first, -1)
                ws[g] = jnp.where(j_io == first, NEG, w)
        for g in range(ATT_GROUPS):
            bias1 = jnp.where(ws[g] < 0.5 * NEG, past_off, NEG)
            bias_ref[g] = jnp.concatenate([bias1] * HPG, axis=1)

    def local(g, q_plain):
        sd = _dot(ks_ref[g, pl.ds(doff, Q_BLK), :], q_plain) + diag_mask
        m0 = jnp.max(sd, axis=0, keepdims=True)
        acc0 = _dot(vst_ref[g, :, pl.ds(doff, Q_BLK)], jnp.exp2(sd - m0).astype(BF16))

        sw = _dot(kw_ref[g, pl.ds(wstart, span), :], q_plain) + win_mask
        pw = jnp.exp2(sw - jnp.max(sw, axis=0, keepdims=True)).astype(BF16)
        acc_w = _dot(vwt_ref[g, :, pl.ds(wstart, span)], pw)
        o_win = acc_w[0:HEAD_DIM] / acc_w[HEAD_DIM:HEAD_DIM + 1]
        return o_win, m0, acc0

    def query_operand(g, q4, c):
        brow = bias_ref[g, pl.ds(aligned(c * SEL_BLKS, SEL_BLKS), SEL_BLKS), :]
        b16 = jnp.concatenate([brow, bpad], axis=0).astype(BF16)
        return jnp.concatenate([q4, b16, vpad], axis=0)

    def phase(g, q4, c, slot, state, do_scores=True, do_softmax=True, do_values=True):
        m_run, alpha, acc, cmax = state
        other = 1 - slot
        if do_values:
            voff = aligned(c * SEL_CHUNK, SEL_CHUNK)
            acc = alpha * acc + _dot(vst_ref[g, :, pl.ds(voff, SEL_CHUNK)], p_refs[g][slot][...])
        if do_scores:
            koff = aligned((c + 2) * SEL_CHUNK, SEL_CHUNK)
            s = _dot(ks_ref[g, pl.ds(koff, SEL_CHUNK), :], query_operand(g, q4, c + 2))
            s_refs[g][slot][...] = s.astype(BF16)
            new_max = jnp.max(s, axis=0, keepdims=True)
        if do_softmax:
            m_new = jnp.maximum(m_run, cmax[other])
            p_refs[g][other][...] = jnp.exp2(s_refs[g][other][...] - m_new.astype(BF16))
            m_run, alpha = m_new, jnp.exp2(m_run - m_new)
        if do_scores:
            cmax = (new_max, cmax[1]) if slot == 0 else (cmax[0], new_max)
        return m_run, alpha, acc, cmax

    pre = [head(g) for g in range(ATT_GROUPS)]
    select([p[3] for p in pre])
    heads = []
    for g in range(ATT_GROUPS):
        q4, q_plain, o_cmp, _ = pre[g]
        o_win, m0, acc0 = local(g, q_plain)
        heads.append((q4, o_cmp, o_win, m0, acc0))

    def pair(k, carries):
        out = []
        for g in range(ATT_GROUPS):
            q4 = heads[g][0]
            state = phase(g, q4, 2 * k, 0, carries[g])
            out.append(phase(g, q4, 2 * k + 1, 1, state))
        return tuple(out)

    n_pairs = jnp.maximum((i * Q_BLK + 2 * SEL_CHUNK - 1) // (2 * SEL_CHUNK), 1)
    init = []
    for g in range(ATT_GROUPS):
        q4, _, _, m0, acc0 = heads[g]
        neg_row = jnp.full((1, nq), NEG, F32)
        state = (m0, jnp.ones((1, nq), F32), acc0, (neg_row, neg_row))
        state = phase(g, q4, -2, 0, state, do_softmax=False, do_values=False)
        init.append(phase(g, q4, -1, 1, state, do_values=False))
    carries = lax.fori_loop(0, n_pairs - 1, pair, tuple(init))
    last = 2 * (n_pairs - 1)
    for g in range(ATT_GROUPS):
        q4, o_cmp, o_win, _, _ = heads[g]
        state = phase(g, q4, last, 0, carries[g], do_scores=False)
        _, _, acc_s, _ = phase(g, q4, last + 1, 1, state, do_scores=False, do_softmax=False)
        o_sel = acc_s[0:HEAD_DIM] / acc_s[HEAD_DIM:HEAD_DIM + 1]
        gates = gate_ref[g]
        outs = []
        for h in range(HPG):
            cols = slice(h * Q_BLK, (h + 1) * Q_BLK)
            outs.append(o_cmp[:, cols] * gates[3 * h:3 * h + 1]
                        + o_sel[:, cols] * gates[3 * h + 1:3 * h + 2]
                        + o_win[:, cols] * gates[3 * h + 2:3 * h + 3])
        o_ref[g * rows:(g + 1) * rows, :] = jnp.concatenate(outs, axis=0).astype(BF16)


def _attention(q_t, gate_t, kc, vct, ks, vst, kw, vwt):
    nqd, s = q_t.shape
    nc = kc.shape[1]
    nb = s // L_SEL
    nq = HPG * Q_BLK
    rows = ATT_GROUPS * HPG * HEAD_DIM
    assert (s // SEL_CHUNK) % 2 == 0 and s >= WINDOW + Q_BLK and N_KV % ATT_GROUPS == 0 and nb >= N_SELECT
    qq = jnp.arange(Q_BLK)[None, :]
    rc = jnp.arange(2 * nc)[:, None]
    cmp_mask = jnp.where(D_CMP * (rc - nc) + L_CMP - 1 <= qq, 0.0, NEG).astype(F32)
    rw = jnp.arange(2 * WINDOW + Q_BLK)[:, None]
    win_mask = jnp.where((qq < rw) & (rw <= qq + WINDOW), 0.0, NEG).astype(F32)
    return pl.pallas_call(
        _attn_kernel,
        grid=(N_KV // ATT_GROUPS, s // Q_BLK),
        in_specs=[pl.BlockSpec((rows, Q_BLK), lambda g, i: (g, i)),
                  pl.BlockSpec((ATT_GROUPS, 16, Q_BLK), lambda g, i: (g, 0, i)),
                  _resident(cmp_mask.shape, lambda g, i: (0, 0)),
                  _resident(win_mask.shape, lambda g, i: (0, 0)),
                  _resident((ATT_GROUPS, nc, LANES), lambda g, i: (g, 0, 0)),
                  _resident((ATT_GROUPS, LANES, nc), lambda g, i: (g, 0, 0)),
                  _resident((ATT_GROUPS, s, LANES), lambda g, i: (g, 0, 0)),
                  _resident((ATT_GROUPS, V_ROWS, s), lambda g, i: (g, 0, 0)),
                  _resident((ATT_GROUPS, s, LANES), lambda g, i: (g, 0, 0)),
                  _resident((ATT_GROUPS, V_ROWS, s), lambda g, i: (g, 0, 0))],
        out_specs=pl.BlockSpec((rows, Q_BLK), lambda g, i: (g, i)),
        out_shape=jax.ShapeDtypeStruct((nqd, s), BF16),
        scratch_shapes=[pltpu.VMEM((ATT_GROUPS, nb, nq), F32),
                        pltpu.VMEM((ATT_GROUPS, PS_PAD + nc, Q_BLK), F32),
                        pltpu.VMEM((ATT_GROUPS, nc, nq), F32),
                        *[pltpu.VMEM((SEL_CHUNK, nq), BF16) for _ in range(2 * ATT_GROUPS)],
                        *[pltpu.VMEM((SEL_CHUNK, nq), BF16) for _ in range(2 * ATT_GROUPS)]],
        compiler_params=_params("arbitrary", "arbitrary"),
        name="nsa_attention",
    )(q_t, gate_t, cmp_mask, win_mask, kc, vct, ks, vst, kw, vwt)


def _outproj_kernel(ot_ref, x_ref, vec_ref, w_ref, o_ref):
    vec = vec_ref[...]
    y = _dot_tn(ot_ref[...], w_ref[...])
    o_ref[...] = _post_residual(x_ref[...], y, vec[3:4], vec[4:5])


def _outproj(o_t, x, vec, w_out):
    s, d = x.shape
    nqd = o_t.shape[0]
    return pl.pallas_call(
        _outproj_kernel,
        grid=(s // ROW_TILE,),
        in_specs=[pl.BlockSpec((nqd, ROW_TILE), lambda i: (0, i)),
                  pl.BlockSpec((ROW_TILE, d), lambda i: (i, 0)),
                  _resident((8, d), lambda i: (0, 0)),
                  _resident((nqd, d), lambda i: (0, 0))],
        out_specs=pl.BlockSpec((ROW_TILE, d), lambda i: (i, 0)),
        out_shape=jax.ShapeDtypeStruct((s, d), F32),
        compiler_params=_params("arbitrary"),
        name="nsa_outproj",
    )(o_t, x, vec, w_out)


def _vec(pre_g, mod, post_g):
    d = pre_g.shape[0]
    shift, scale, gate = mod[:d], mod[d:2 * d], mod[2 * d:3 * d]
    z = jnp.zeros((d,), F32)
    return jnp.stack([pre_g, scale, shift, post_g, gate, z, z, z])


def kernel(x, c, ada_w, ada_b, norm_pre_g, norm_post_g, a_w_in, a_ln_g, a_ln_b, a_ws, a_bs, a_w_out, kv_norm_g, kv_ada_w, kv_ada_b, w_kv, cmp_pos, cmp_w1, cmp_b1, cmp_w2, cmp_b2, b_w_in, b_w_out, ffn_w_gate, ffn_w_up, ffn_w_down, moe_router, moe_router_b, moe_w_gate, moe_w_up, moe_w_down):
    batch, s, d = x.shape
    assert batch == 1 and s % SEL_CHUNK == 0 and s >= WINDOW + Q_BLK
    depth = ada_w.shape[0]
    n_a = depth // 2
    xs = x.reshape(s, d)

    mods = _ada(c, ada_w.reshape(depth * 2, d, 3 * d), ada_b.reshape(depth * 2, 3 * d)).reshape(depth, 2, 3 * d)
    kv_mod = _ada(c, kv_ada_w.reshape(1, d, 2 * d), kv_ada_b.reshape(1, 2 * d))[0]
    shared = None
    n_moe, _, _, f_moe = moe_w_gate.shape
    moe_wg = moe_w_gate.astype(BF16).reshape(n_moe * N_EXPERTS, d, f_moe)
    moe_wu = moe_w_up.astype(BF16).reshape(n_moe * N_EXPERTS, d, f_moe)
    moe_wd = moe_w_down.astype(BF16).reshape(n_moe * N_EXPERTS, f_moe, d)

    for layer in range(depth):
        vec = _vec(norm_pre_g[layer, 0], mods[layer, 0], norm_post_g[layer, 0])
        if layer < n_a:
            xs = _gmlp_layer(xs, vec, a_w_in[layer].astype(BF16),
                             jnp.stack([a_ln_g[layer], a_ln_b[layer]]),
                             a_ws[layer], a_bs[layer].T, a_w_out[layer].astype(BF16))
        else:
            if shared is None:
                kv_vec = _vec(kv_norm_g, jnp.concatenate([kv_mod, jnp.zeros((d,), F32)]), jnp.zeros((d,), F32))
                gd = N_KV * HEAD_DIM
                w_vt = jnp.concatenate([w_kv[:, 3 * gd:4 * gd], w_kv[:, 5 * gd:6 * gd]], axis=1).T
                raw, ks, kw, vst, vwt = _kvproj(xs, kv_vec, w_kv.astype(BF16), w_vt.astype(BF16))
                nc = s // D_CMP
                width = D_CMP * HEAD_DIM
                hid = cmp_w1.shape[-1]
                w2p = jnp.pad(cmp_w2, ((0, 0), (0, 0), (0, LANES - HEAD_DIM))).astype(BF16)
                b2p = jnp.pad(cmp_b2, ((0, 0), (0, LANES - HEAD_DIM))).reshape(2, 1, LANES)
                cmp_n, cmp_t = _compress(raw.reshape(2, N_KV, nc, width),
                                         cmp_pos.reshape(2, 2, 1, width),
                                         cmp_w1.reshape(2, 2, width, hid).astype(BF16),
                                         cmp_b1.reshape(2, 1, hid), w2p, b2p)
                shared = (cmp_n[0], cmp_t[1], ks, vst, kw, vwt)
            i = layer - n_a
            nq = N_HEADS * HEAD_DIM
            w_t = b_w_in[i].T.astype(BF16)
            q_t, gate_t = _qproj(xs, vec, w_t)
            kc, vct, ks, vst, kw, vwt = shared
            o_t = _attention(q_t, gate_t, kc, vct, ks, vst, kw, vwt)
            xs = _outproj(o_t, xs, vec, b_w_out[i].astype(BF16))

        vec = _vec(norm_pre_g[layer, 1], mods[layer, 1], norm_post_g[layer, 1])
        j = layer // 2
        if layer % 2 == 0:
            xs = _swiglu_layer(xs, vec, ffn_w_gate[j].astype(BF16), ffn_w_up[j].astype(BF16),
                               ffn_w_down[j].astype(BF16))
        else:
            xs = _moe_layer(xs, vec, moe_router[j].T, moe_router_b[j].reshape(N_EXPERTS, 1),
                            moe_wg, moe_wu, moe_wd, j * N_EXPERTS)
    return xs.reshape(batch, s, d)
```

```python
import jax
import jax.numpy as jnp
from jax import lax
from jax.experimental import pallas as pl
from jax.experimental.pallas import tpu as pltpu

F32 = jnp.float32
BF16 = jnp.bfloat16
I32 = jnp.int32

EPS = 1e-6
NEG = -1e30

LANES = 128
VMEM_LIMIT_BYTES = 56 * 1024 * 1024

CHUNK = 128
A_GROUPS = 8
N_HEADS = 16
N_KV = 4
HPG = N_HEADS // N_KV
HEAD_DIM = 64
L_CMP = 32
D_CMP = 16
L_SEL = 64
SEL_SHIFT = 6
N_SELECT = 16
N_FORCED = 3
WINDOW = 512
Q_BLK = 128
N_EXPERTS = 8
MOE_BLOCK = 256

ROW_TILE = 256
TOK_TILE = 256
SEL_CHUNK = 512
SEL_BLKS = SEL_CHUNK // L_SEL
CMP_CHUNK = 128
ATT_GROUPS = 2
PS_PAD = 8
V_ROWS = 72
F_CHUNK = 512
WIN_PART_ROWS = 128
WIN_PARTS = TOK_TILE // WIN_PART_ROWS + 1
WIN_ROWS = WIN_PARTS * WIN_PART_ROWS
LOG2E = 1.4426950408889634


def _params(*sem):
    return pltpu.CompilerParams(dimension_semantics=sem, vmem_limit_bytes=VMEM_LIMIT_BYTES)


def _resident(shape, index_map):
    return pl.BlockSpec(shape, index_map, pipeline_mode=pl.Buffered(1))


def _split_bf16(a):
    hi = a.astype(BF16)
    lo = (a - hi.astype(F32)).astype(BF16)
    return hi, lo


def _dot(a, b):
    return jnp.dot(a, b, preferred_element_type=F32)


def _dot_nt(a, b):
    return lax.dot_general(a, b, (((1,), (1,)), ((), ())), preferred_element_type=F32)


def _dot_tn(a, b):
    return lax.dot_general(a, b, (((0,), (0,)), ((), ())), preferred_element_type=F32)


def _prenorm(x, g, scale, shift):
    ms = jnp.mean(x * x, axis=-1, keepdims=True)
    return (x * lax.rsqrt(ms + EPS) * g) * (1.0 + scale) + shift


def _post_residual(x, y, g, gate):
    ms = jnp.mean(y * y, axis=-1, keepdims=True)
    return x + gate * (y * lax.rsqrt(ms + EPS) * g)


def _ada_kernel(c_ref, w_ref, b_ref, o_ref):
    c = c_ref[...]
    c_act = jnp.broadcast_to(c * jax.nn.sigmoid(c), (8, c.shape[1]))
    c_hi, c_lo = _split_bf16(c_act)
    w_hi, w_lo = _split_bf16(w_ref[...])
    m = _dot(c_hi, w_hi) + (_dot(c_hi, w_lo) + _dot(c_lo, w_hi))
    o_ref[...] = m[0:1] + b_ref[...]


def _ada(c, w, b):
    n, d, nn = w.shape
    out = pl.pallas_call(
        _ada_kernel,
        grid=(n, nn // d),
        in_specs=[pl.BlockSpec((1, d), lambda i, j: (0, 0)),
                  pl.BlockSpec((None, d, d), lambda i, j: (i, 0, j)),
                  pl.BlockSpec((None, 1, d), lambda i, j: (i, 0, j))],
        out_specs=pl.BlockSpec((None, 1, d), lambda i, j: (i, 0, j)),
        out_shape=jax.ShapeDtypeStruct((n, 1, nn), F32),
        compiler_params=_params("arbitrary", "arbitrary"),
        name="ada",
    )(c, w, b.reshape(n, 1, nn))
    return out.reshape(n, nn)


def _gmlp_kernel(x_ref, vec_ref, win_ref, ln_ref, ws_ref, bst_ref, wout_ref, o_ref, gated_ref):
    x = x_ref[...]
    vec = vec_ref[...]
    h = _prenorm(x, vec[0:1], vec[1:2], vec[2:3]).astype(BF16)
    z = jax.nn.gelu(_dot(h, win_ref[...]))
    width = z.shape[1] // 2
    u = z[:, :width]
    v = z[:, width:]
    mu = jnp.mean(v, axis=-1, keepdims=True)
    vc = v - mu
    var = jnp.mean(vc * vc, axis=-1, keepdims=True)
    ln = ln_ref[...]
    vn = (vc * lax.rsqrt(var + EPS) * ln[0:1] + ln[1:2]).astype(BF16)
    causal = (lax.broadcasted_iota(I32, (CHUNK, CHUNK), 0)
              >= lax.broadcasted_iota(I32, (CHUNK, CHUNK), 1))
    gw = width // A_GROUPS
    bst = bst_ref[...]
    for g in range(A_GROUPS):
        wg = jnp.where(causal, ws_ref[g], 0.0).astype(BF16)
        for ck in range(x.shape[0] // CHUNK):
            rows = slice(ck * CHUNK, (ck + 1) * CHUNK)
            cols = slice(g * gw, (g + 1) * gw)
            mixed = _dot(wg, vn[rows, cols]) + bst[:, g:g + 1]
            gated_ref[rows, cols] = (u[rows, cols] * mixed).astype(BF16)
    y = _dot(gated_ref[...], wout_ref[...])
    o_ref[...] = _post_residual(x, y, vec[3:4], vec[4:5])


def _gmlp_layer(x, vec, w_in, ln, ws, bst, w_out):
    s, d = x.shape
    e2 = w_in.shape[1]
    return pl.pallas_call(
        _gmlp_kernel,
        grid=(s // ROW_TILE,),
        in_specs=[pl.BlockSpec((ROW_TILE, d), lambda i: (i, 0)),
                  _resident((8, d), lambda i: (0, 0)),
                  _resident((d, e2), lambda i: (0, 0)),
                  _resident((2, e2 // 2), lambda i: (0, 0)),
                  _resident(ws.shape, lambda i: (0, 0, 0)),
                  _resident(bst.shape, lambda i: (0, 0)),
                  _resident((e2 // 2, d), lambda i: (0, 0))],
        out_specs=pl.BlockSpec((ROW_TILE, d), lambda i: (i, 0)),
        out_shape=jax.ShapeDtypeStruct((s, d), F32),
        scratch_shapes=[pltpu.VMEM((ROW_TILE, e2 // 2), BF16)],
        compiler_params=_params("arbitrary"),
        name="gmlp",
    )(x, vec, w_in, ln, ws, bst, w_out)


def _swiglu_kernel(x_ref, vec_ref, wg_ref, wu_ref, wd_ref, o_ref):
    x = x_ref[...]
    vec = vec_ref[...]
    h = _prenorm(x, vec[0:1], vec[1:2], vec[2:3]).astype(BF16)
    g = _dot(h, wg_ref[...])
    a = (g * jax.nn.sigmoid(g) * _dot(h, wu_ref[...])).astype(BF16)
    y = _dot(a, wd_ref[...])
    o_ref[...] = _post_residual(x, y, vec[3:4], vec[4:5])


def _swiglu_layer(x, vec, w_gate, w_up, w_down):
    s, d = x.shape
    f = w_gate.shape[1]
    return pl.pallas_call(
        _swiglu_kernel,
        grid=(s // ROW_TILE,),
        in_specs=[pl.BlockSpec((ROW_TILE, d), lambda i: (i, 0)),
                  _resident((8, d), lambda i: (0, 0)),
                  _resident((d, f), lambda i: (0, 0)),
                  _resident((d, f), lambda i: (0, 0)),
                  _resident((f, d), lambda i: (0, 0))],
        out_specs=pl.BlockSpec((ROW_TILE, d), lambda i: (i, 0)),
        out_shape=jax.ShapeDtypeStruct((s, d), F32),
        compiler_params=_params("arbitrary"),
        name="swiglu",
    )(x, vec, w_gate, w_up, w_down)


def _router_kernel(x_ref, vec_ref, wrt_ref, br_ref, h_ref, idx_ref, gate_ref):
    x = x_ref[...]
    vec = vec_ref[...]
    h = _prenorm(x, vec[0:1], vec[1:2], vec[2:3])
    h_ref[...] = h.astype(BF16)
    h_hi, h_lo = _split_bf16(h)
    w_hi, w_lo = _split_bf16(wrt_ref[...])
    logit = _dot_nt(w_hi, h_hi) + (_dot_nt(w_hi, h_lo) + _dot_nt(w_lo, h_hi)) + br_ref[...]
    ne = logit.shape[0]
    eidx = lax.broadcasted_iota(I32, logit.shape, 0)
    m1 = jnp.max(logit, axis=0, keepdims=True)
    i1 = jnp.min(jnp.where(logit == m1, eidx, ne), axis=0, keepdims=True)
    rest = jnp.where(eidx == i1, -jnp.inf, logit)
    m2 = jnp.max(rest, axis=0, keepdims=True)
    i2 = jnp.min(jnp.where(rest == m2, eidx, ne), axis=0, keepdims=True)
    e21 = jnp.exp(m2 - m1)
    g1 = 1.0 / (1.0 + e21)
    g2 = e21 * g1
    pad_i = jnp.zeros((6, i1.shape[1]), I32)
    pad_f = jnp.zeros((6, i1.shape[1]), F32)
    idx_ref[...] = jnp.concatenate([i1, i2, pad_i], axis=0)
    gate_ref[...] = jnp.concatenate([g1, g2, pad_f], axis=0)


def _router(x, vec, w_rt, b_r):
    s, d = x.shape
    ne = w_rt.shape[0]
    return pl.pallas_call(
        _router_kernel,
        grid=(s // ROW_TILE,),
        in_specs=[pl.BlockSpec((ROW_TILE, d), lambda i: (i, 0)),
                  _resident((8, d), lambda i: (0, 0)),
                  _resident((ne, d), lambda i: (0, 0)),
                  _resident((ne, 1), lambda i: (0, 0))],
        out_specs=[pl.BlockSpec((ROW_TILE, d), lambda i: (i, 0)),
                   pl.BlockSpec((8, ROW_TILE), lambda i: (0, i)),
                   pl.BlockSpec((8, ROW_TILE), lambda i: (0, i))],
        out_shape=[jax.ShapeDtypeStruct((s, d), BF16),
                   jax.ShapeDtypeStruct((8, s), I32),
                   jax.ShapeDtypeStruct((8, s), F32)],
        compiler_params=_params("arbitrary"),
        name="moe_router",
    )(x, vec, w_rt, b_r)


def _rank_kernel(idx_ref, rank_ref, start_ref, count_ref, carry_ref):
    @pl.when(pl.program_id(0) == 0)
    def _():
        carry_ref[...] = jnp.zeros_like(carry_ref)

    idx = idx_ref[...]
    tt = idx.shape[1]
    eidx = lax.broadcasted_iota(I32, (N_EXPERTS, tt), 0)
    hit1 = eidx == idx[0:1]
    hit2 = eidx == idx[1:2]
    member = jnp.where(hit1 | hit2, 1.0, 0.0)
    before = (lax.broadcasted_iota(I32, (tt, tt), 0) < lax.broadcasted_iota(I32, (tt, tt), 1))
    carry = carry_ref[...]
    cum = _dot(member.astype(BF16), jnp.where(before, 1.0, 0.0).astype(BF16)) + carry[:, 0:1]
    r1 = jnp.sum(jnp.where(hit1, cum, 0.0), axis=0, keepdims=True)
    r2 = jnp.sum(jnp.where(hit2, cum, 0.0), axis=0, keepdims=True)
    rank_ref[...] = jnp.concatenate([r1, r2, jnp.zeros((6, tt), F32)], axis=0).astype(I32)
    tile_count = jnp.broadcast_to(jnp.sum(member, axis=1, keepdims=True), carry.shape)
    start_ref[...] = carry
    count_ref[...] = tile_count
    carry_ref[...] = carry + tile_count


def _ranks(idx):
    s = idx.shape[1]
    nt = s // TOK_TILE
    return pl.pallas_call(
        _rank_kernel,
        grid=(nt,),
        in_specs=[pl.BlockSpec((8, TOK_TILE), lambda i: (0, i))],
        out_specs=[pl.BlockSpec((8, TOK_TILE), lambda i: (0, i)),
                   pl.BlockSpec((None, N_EXPERTS, LANES), lambda i: (i, 0, 0)),
                   pl.BlockSpec((None, N_EXPERTS, LANES), lambda i: (i, 0, 0))],
        out_shape=[jax.ShapeDtypeStruct((8, s), I32),
                   jax.ShapeDtypeStruct((nt, N_EXPERTS, LANES), F32),
                   jax.ShapeDtypeStruct((nt, N_EXPERTS, LANES), F32)],
        scratch_shapes=[pltpu.VMEM((N_EXPERTS, LANES), F32)],
        compiler_params=_params("arbitrary"),
        name="moe_ranks",
    )(idx)


def _match(idx, rank, expert, rows):
    r1 = jnp.where(idx[0:1] == expert, rank[0:1], -1)
    r2 = jnp.where(idx[1:2] == expert, rank[1:2], -1)
    return rows == r1, rows == r2


def _dispatch_kernel(be_ref, lb_ref, tlo_ref, thi_ref, idx_ref, rank_ref, gate_ref, h_ref, o_ref, rg_ref,
                     acc_ref, gacc_ref):
    b = pl.program_id(0)
    expert = be_ref[b]
    rows = lax.broadcasted_iota(I32, (MOE_BLOCK, TOK_TILE), 0) + lb_ref[b]
    acc_ref[...] = jnp.zeros_like(acc_ref)
    gacc_ref[...] = jnp.zeros_like(gacc_ref)

    def body(t, carry):
        off = pl.multiple_of(t * TOK_TILE, TOK_TILE)
        m1, m2 = _match(idx_ref[:, pl.ds(off, TOK_TILE)], rank_ref[:, pl.ds(off, TOK_TILE)], expert, rows)
        gates = gate_ref[:, pl.ds(off, TOK_TILE)]
        gacc_ref[...] += jnp.sum(jnp.where(m1, gates[0:1], 0.0) + jnp.where(m2, gates[1:2], 0.0),
                                 axis=1, keepdims=True)
        onehot = jnp.where(m1 | m2, 1.0, 0.0).astype(BF16)
        acc_ref[...] += _dot(onehot, h_ref[pl.ds(off, TOK_TILE), :])
        return carry

    lax.fori_loop(tlo_ref[b], thi_ref[b], body, 0)
    o_ref[...] = acc_ref[...].astype(BF16)
    rg_ref[...] = gacc_ref[...]


def _dispatch(blk_e, blk_lb, blk_tlo, blk_thi, idx, rank, gates, h):
    s, d = h.shape
    n_blk = blk_e.shape[0]
    grid_spec = pltpu.PrefetchScalarGridSpec(
        num_scalar_prefetch=4,
        grid=(n_blk,),
        in_specs=[_resident((8, s), lambda b, *_: (0, 0)),
                  _resident((8, s), lambda b, *_: (0, 0)),
                  _resident((8, s), lambda b, *_: (0, 0)),
                  _resident((s, d), lambda b, *_: (0, 0))],
        out_specs=[pl.BlockSpec((MOE_BLOCK, d), lambda b, *_: (b, 0)),
                   pl.BlockSpec((MOE_BLOCK, 1), lambda b, *_: (b, 0))],
        scratch_shapes=[pltpu.VMEM((MOE_BLOCK, d), F32), pltpu.VMEM((MOE_BLOCK, 1), F32)],
    )
    return pl.pallas_call(
        _dispatch_kernel,
        grid_spec=grid_spec,
        out_shape=[jax.ShapeDtypeStruct((n_blk * MOE_BLOCK, d), BF16),
                   jax.ShapeDtypeStruct((n_blk * MOE_BLOCK, 1), F32)],
        compiler_params=_params("arbitrary"),
        name="moe_dispatch",
    )(blk_e, blk_lb, blk_tlo, blk_thi, idx, rank, gates, h)


def _expert_kernel(be_ref, used_ref, x_ref, rg_ref, wg_ref, wu_ref, wd_ref, o_ref):
    b = pl.program_id(0)

    @pl.when(b < used_ref[0])
    def _():
        x = x_ref[...]
        f = wg_ref.shape[1]
        y = jnp.zeros(o_ref.shape, F32)
        for c in range(f // F_CHUNK):
            cols = slice(c * F_CHUNK, (c + 1) * F_CHUNK)
            g = _dot(x, wg_ref[:, cols])
            a = (g * jax.nn.sigmoid(g) * _dot(x, wu_ref[:, cols])).astype(BF16)
            y = y + _dot(a, wd_ref[cols, :])
        o_ref[...] = (y * rg_ref[...]).astype(BF16)

    @pl.when(b >= used_ref[0])
    def _():
        o_ref[...] = jnp.zeros_like(o_ref)


def _experts(blk_e, n_used, x_buf, row_gate, w_gate, w_up, w_down):
    n_rows, d = x_buf.shape
    f = w_gate.shape[2]
    assert f % F_CHUNK == 0
    grid_spec = pltpu.PrefetchScalarGridSpec(
        num_scalar_prefetch=2,
        grid=(n_rows // MOE_BLOCK,),
        in_specs=[pl.BlockSpec((MOE_BLOCK, d), lambda b, be, nu: (b, 0)),
                  pl.BlockSpec((MOE_BLOCK, 1), lambda b, be, nu: (b, 0)),
                  pl.BlockSpec((None, d, f), lambda b, be, nu: (be[b], 0, 0)),
                  pl.BlockSpec((None, d, f), lambda b, be, nu: (be[b], 0, 0)),
                  pl.BlockSpec((None, f, d), lambda b, be, nu: (be[b], 0, 0))],
        out_specs=pl.BlockSpec((MOE_BLOCK, d), lambda b, be, nu: (b, 0)),
    )
    return pl.pallas_call(
        _expert_kernel,
        grid_spec=grid_spec,
        out_shape=jax.ShapeDtypeStruct((n_rows, d), BF16),
        compiler_params=_params("arbitrary"),
        name="moe_experts",
    )(blk_e, n_used, x_buf, row_gate, w_gate, w_up, w_down)


def _combine_kernel(win_ref, lb_ref, idx_ref, rank_ref, *refs):
    y_refs = refs[:N_EXPERTS * WIN_PARTS]
    x_ref, vec_ref, o_ref = refs[N_EXPERTS * WIN_PARTS:]
    t = pl.program_id(0)
    idx = idx_ref[...]
    rank = rank_ref[...]
    acc = jnp.zeros(o_ref.shape, F32)
    for e in range(N_EXPERTS):
        rows = lax.broadcasted_iota(I32, (WIN_ROWS, TOK_TILE), 0) + lb_ref[t * N_EXPERTS + e]
        m1, m2 = _match(idx, rank, e, rows)
        onehot = jnp.where(m1 | m2, 1.0, 0.0).astype(BF16)
        y = jnp.concatenate([y_refs[e * WIN_PARTS + k][...] for k in range(WIN_PARTS)], axis=0)
        acc = acc + _dot_tn(onehot, y)
    vec = vec_ref[...]
    o_ref[...] = _post_residual(x_ref[...], acc, vec[3:4], vec[4:5])


def _combine(win_start, win_lb, idx, rank, y_buf, x, vec):
    s, d = x.shape

    def y_spec(e, k):
        return pl.BlockSpec((WIN_PART_ROWS, d), lambda t, ws, lb: (ws[t * N_EXPERTS + e] + k, 0))

    grid_spec = pltpu.PrefetchScalarGridSpec(
        num_scalar_prefetch=2,
        grid=(s // TOK_TILE,),
        in_specs=[pl.BlockSpec((8, TOK_TILE), lambda t, *_: (0, t)),
                  pl.BlockSpec((8, TOK_TILE), lambda t, *_: (0, t))]
                 + [y_spec(e, k) for e in range(N_EXPERTS) for k in range(WIN_PARTS)]
                 + [pl.BlockSpec((TOK_TILE, d), lambda t, *_: (t, 0)),
                    _resident((8, d), lambda t, *_: (0, 0))],
        out_specs=pl.BlockSpec((TOK_TILE, d), lambda t, *_: (t, 0)),
    )
    return pl.pallas_call(
        _combine_kernel,
        grid_spec=grid_spec,
        out_shape=jax.ShapeDtypeStruct((s, d), F32),
        compiler_params=_params("arbitrary"),
        name="moe_combine",
    )(win_start, win_lb, idx, rank, *([y_buf] * (N_EXPERTS * WIN_PARTS)), x, vec)


def _moe_layer(x, vec, w_rt, b_r, w_gate, w_up, w_down, w_first):
    s, d = x.shape
    nt = s // TOK_TILE
    h, idx, gates = _router(x, vec, w_rt, b_r)
    rank, tile_start, tile_count = _ranks(idx)

    tile_start = tile_start[:, :, 0].astype(I32)
    tile_count = tile_count[:, :, 0].astype(I32)
    tile_end = tile_start + tile_count
    counts = tile_end[-1]
    padded = (counts + MOE_BLOCK - 1) // MOE_BLOCK * MOE_BLOCK
    pend = jnp.cumsum(padded)
    pstart = pend - padded
    n_rows = -(-(2 * s) // MOE_BLOCK) * MOE_BLOCK + N_EXPERTS * MOE_BLOCK
    n_blk = n_rows // MOE_BLOCK
    blk_row = jnp.arange(n_blk, dtype=I32) * MOE_BLOCK
    blk_e = jnp.minimum(jnp.sum(blk_row[:, None] >= pend[None, :], axis=1), N_EXPERTS - 1).astype(I32)
    blk_lb = blk_row - pstart[blk_e]
    te = tile_end[:, blk_e]
    ts = tile_start[:, blk_e]
    blk_tlo = jnp.sum(te <= blk_lb[None, :], axis=0).astype(I32)
    blk_thi = jnp.sum(ts < (blk_lb + MOE_BLOCK)[None, :], axis=0).astype(I32)
    n_used = (pend[-1] // MOE_BLOCK).astype(I32).reshape(1)

    x_buf, row_gate = _dispatch(blk_e, blk_lb.astype(I32), blk_tlo, blk_thi, idx, rank, gates, h)
    y_buf = _experts(blk_e + w_first, n_used, x_buf, row_gate, w_gate, w_up, w_down)

    win_start = jnp.minimum((pstart[None, :] + tile_start) // WIN_PART_ROWS, n_rows // WIN_PART_ROWS - WIN_PARTS)
    win_lb = win_start * WIN_PART_ROWS - pstart[None, :]
    return _combine(win_start.reshape(-1).astype(I32), win_lb.reshape(-1).astype(I32), idx, rank, y_buf, x, vec)


def _kvproj_kernel(x_ref, vec_ref, wkv_ref, wvt_ref, raw_ref, ks_ref, kw_ref, vst_ref, vwt_ref):
    x = x_ref[...]
    vec = vec_ref[...]
    h = _prenorm(x, vec[0:1], vec[1:2], vec[2:3]).astype(BF16)
    kv = _dot(h, wkv_ref[...])
    vt = _dot_nt(wvt_ref[...], h)
    tm = x.shape[0]
    gd = N_KV * HEAD_DIM
    key_blk = jnp.right_shift(pl.program_id(0) * tm + lax.broadcasted_iota(I32, (tm, SEL_BLKS), 0), SEL_SHIFT)
    ind = jnp.where((key_blk & (SEL_BLKS - 1)) == lax.broadcasted_iota(I32, (tm, SEL_BLKS), 1), 1.0, 0.0)
    pad_s = jnp.zeros((tm, LANES - HEAD_DIM - SEL_BLKS), F32)
    pad_w = jnp.zeros((tm, LANES - HEAD_DIM), F32)
    ones_row = jnp.concatenate([jnp.ones((1, tm), F32), jnp.zeros((V_ROWS - HEAD_DIM - 1, tm), F32)], axis=0)
    for g in range(N_KV):
        c = g * HEAD_DIM
        raw_ref[0, g] = kv[:, c:c + HEAD_DIM]
        raw_ref[1, g] = kv[:, gd + c:gd + c + HEAD_DIM]
        ks = kv[:, 2 * gd + c:2 * gd + c + HEAD_DIM]
        kw = kv[:, 4 * gd + c:4 * gd + c + HEAD_DIM]
        ks_ref[g] = jnp.concatenate([ks, ind, pad_s], axis=1).astype(BF16)
        kw_ref[g] = jnp.concatenate([kw, pad_w], axis=1).astype(BF16)
        vst_ref[g] = jnp.concatenate([vt[c:c + HEAD_DIM], ones_row], axis=0).astype(BF16)
        vwt_ref[g] = jnp.concatenate([vt[gd + c:gd + c + HEAD_DIM], ones_row], axis=0).astype(BF16)


def _kvproj(x, vec, w_kv, w_vt):
    s, d = x.shape
    nkv = w_kv.shape[1]
    return pl.pallas_call(
        _kvproj_kernel,
        grid=(s // ROW_TILE,),
        in_specs=[pl.BlockSpec((ROW_TILE, d), lambda i: (i, 0)),
                  _resident((8, d), lambda i: (0, 0)),
                  _resident((d, nkv), lambda i: (0, 0)),
                  _resident(w_vt.shape, lambda i: (0, 0))],
        out_specs=[pl.BlockSpec((2, N_KV, ROW_TILE, HEAD_DIM), lambda i: (0, 0, i, 0)),
                   pl.BlockSpec((N_KV, ROW_TILE, LANES), lambda i: (0, i, 0)),
                   pl.BlockSpec((N_KV, ROW_TILE, LANES), lambda i: (0, i, 0)),
                   pl.BlockSpec((N_KV, V_ROWS, ROW_TILE), lambda i: (0, 0, i)),
                   pl.BlockSpec((N_KV, V_ROWS, ROW_TILE), lambda i: (0, 0, i))],
        out_shape=[jax.ShapeDtypeStruct((2, N_KV, s, HEAD_DIM), F32),
                   jax.ShapeDtypeStruct((N_KV, s, LANES), BF16),
                   jax.ShapeDtypeStruct((N_KV, s, LANES), BF16),
                   jax.ShapeDtypeStruct((N_KV, V_ROWS, s), BF16),
                   jax.ShapeDtypeStruct((N_KV, V_ROWS, s), BF16)],
        compiler_params=_params("arbitrary"),
        name="nsa_kvproj",
    )(x, vec, w_kv, w_vt)


def _compress_kernel(raw_ref, pos_ref, w1_ref, b1_ref, w2_ref, b2_ref, n_ref, t_ref):
    raw = raw_ref[...]
    nc = raw.shape[0]
    first = _dot((raw + pos_ref[0]).astype(BF16), w1_ref[0])
    second = _dot((raw + pos_ref[1]).astype(BF16), w1_ref[1])
    hid = jax.nn.gelu(first + pltpu.roll(second, nc - 1, 0) + b1_ref[...])
    out = _dot(hid.astype(BF16), w2_ref[...]) + b2_ref[...]
    n_ref[...] = out.astype(BF16)
    row = lax.broadcasted_iota(I32, (LANES, nc), 0)
    t_ref[...] = jnp.where(row == HEAD_DIM, 1.0, out.T).astype(BF16)


def _compress(raw, pos, w1, b1, w2, b2):
    _, g, nc, width = raw.shape
    hid = w1.shape[-1]
    return pl.pallas_call(
        _compress_kernel,
        grid=(2, g),
        in_specs=[pl.BlockSpec((None, None, nc, width), lambda j, k: (j, k, 0, 0)),
                  pl.BlockSpec((None, 2, 1, width), lambda j, k: (j, 0, 0, 0)),
                  pl.BlockSpec((None, 2, width, hid), lambda j, k: (j, 0, 0, 0)),
                  pl.BlockSpec((None, 1, hid), lambda j, k: (j, 0, 0)),
                  pl.BlockSpec((None, hid, LANES), lambda j, k: (j, 0, 0)),
                  pl.BlockSpec((None, 1, LANES), lambda j, k: (j, 0, 0))],
        out_specs=[pl.BlockSpec((None, None, nc, LANES), lambda j, k: (j, k, 0, 0)),
                   pl.BlockSpec((None, None, LANES, nc), lambda j, k: (j, k, 0, 0))],
        out_shape=[jax.ShapeDtypeStruct((2, g, nc, LANES), BF16),
                   jax.ShapeDtypeStruct((2, g, LANES, nc), BF16)],
        compiler_params=_params("arbitrary", "arbitrary"),
        name="nsa_compress",
    )(raw, pos, w1, b1, w2, b2)


def _qproj_kernel(x_ref, vec_ref, wt_ref, q_ref, gate_ref):
    x = x_ref[...]
    vec = vec_ref[...]
    h = _prenorm(x, vec[0:1], vec[1:2], vec[2:3]).astype(BF16)
    pt = _dot_nt(wt_ref[...], h)
    nq = N_HEADS * HEAD_DIM
    q_ref[...] = (pt[:nq] * (HEAD_DIM ** -0.5 * LOG2E)).astype(BF16)
    gates = jax.nn.sigmoid(pt[nq:nq + 3 * N_HEADS])
    per = 3 * HPG
    pad = jnp.zeros((16 - per, x.shape[0]), F32)
    for g in range(N_KV):
        gate_ref[g] = jnp.concatenate([gates[g * per:(g + 1) * per], pad], axis=0)


def _qproj(x, vec, w_t):
    s, d = x.shape
    return pl.pallas_call(
        _qproj_kernel,
        grid=(s // ROW_TILE,),
        in_specs=[pl.BlockSpec((ROW_TILE, d), lambda i: (i, 0)),
                  _resident((8, d), lambda i: (0, 0)),
                  _resident(w_t.shape, lambda i: (0, 0))],
        out_specs=[pl.BlockSpec((N_HEADS * HEAD_DIM, ROW_TILE), lambda i: (0, i)),
                   pl.BlockSpec((N_KV, 16, ROW_TILE), lambda i: (0, 0, i))],
        out_shape=[jax.ShapeDtypeStruct((N_HEADS * HEAD_DIM, s), BF16),
                   jax.ShapeDtypeStruct((N_KV, 16, s), F32)],
        compiler_params=_params("arbitrary"),
        name="nsa_qproj",
    )(x, vec, w_t)


def _attn_kernel(q_ref, gate_ref, cmask_ref, wmask_ref, kc_ref, vct_ref, ks_ref, vst_ref, kw_ref, vwt_ref,
                 o_ref, bias_ref, ps_ref, sc_ref, *sp_refs):
    s_refs = [[sp_refs[2 * g + k] for k in range(2)] for g in range(ATT_GROUPS)]
    p_refs = [[sp_refs[2 * ATT_GROUPS + 2 * g + k] for k in range(2)] for g in range(ATT_GROUPS)]
    i = pl.program_id(1)
    nq = HPG * Q_BLK
    nc = kc_ref.shape[1]
    nb = bias_ref.shape[1]
    rows = HPG * HEAD_DIM
    t1 = i * Q_BLK + lax.broadcasted_iota(I32, (1, Q_BLK), 1)
    t4 = jnp.concatenate([t1] * HPG, axis=1)
    j_io = lax.broadcasted_iota(I32, (nb, Q_BLK), 0)
    jt = jnp.right_shift(t1, SEL_SHIFT)
    forced = (j_io == 0) | (j_io == jt) | (j_io == jt - 1)
    cand_off = jnp.where((j_io * L_SEL <= t1) & jnp.logical_not(forced), 0.0, NEG)
    past_off = jnp.where(j_io * L_SEL < i * Q_BLK, 0.0, NEG)
    doff = pl.multiple_of(i * Q_BLK, Q_BLK)
    cmask_off = nc - i * (Q_BLK // D_CMP)
    span = WINDOW + Q_BLK
    wstart = pl.multiple_of(jnp.maximum(i * Q_BLK - WINDOW, 0), Q_BLK)
    win_mask = wmask_ref[pl.ds(pl.multiple_of(WINDOW - jnp.minimum(i * Q_BLK, WINDOW), Q_BLK), span), :]
    win_mask = jnp.concatenate([win_mask] * HPG, axis=1)
    diag_mask = jnp.concatenate([wmask_ref[WINDOW:WINDOW + Q_BLK, :]] * HPG, axis=1)
    vpad = jnp.zeros((LANES - HEAD_DIM - 2 * SEL_BLKS, nq), BF16)
    bpad = jnp.zeros((SEL_BLKS, nq), F32)

    def aligned(x, m):
        return x if isinstance(x, int) else pl.multiple_of(x, m)

    groups = range(ATT_GROUPS)
    q4s, q_plains = [], []
    for g in groups:
        qb = q_ref[g * rows:(g + 1) * rows, :]
        q4s.append(jnp.concatenate([qb[h * HEAD_DIM:(h + 1) * HEAD_DIM] for h in range(HPG)], axis=1))
        q_plains.append(jnp.concatenate([q4s[g], jnp.zeros((LANES - HEAD_DIM, nq), BF16)], axis=0))

    def compressed(n_rows):
        chunks = [slice(r0, r0 + CMP_CHUNK) for r0 in range(0, n_rows, CMP_CHUNK)]
        ms = [jnp.full((1, nq), NEG, F32) for _ in groups]
        for rs in chunks:
            mask = cmask_ref[pl.ds(pl.multiple_of(cmask_off + rs.start, 8), CMP_CHUNK), :]
            mask = jnp.concatenate([mask] * HPG, axis=1)
            for g in groups:
                s = _dot(kc_ref[g, rs, :], q_plains[g]) + mask
                sc_ref[g, rs, :] = s
                ms[g] = jnp.maximum(ms[g], jnp.max(s, axis=0, keepdims=True))
        ls = [jnp.zeros((1, nq), F32) for _ in groups]
        for rs in chunks:
            for g in groups:
                e = jnp.exp2(sc_ref[g, rs, :] - ms[g])
                sc_ref[g, rs, :] = e
                ls[g] = ls[g] + jnp.sum(e, axis=0, keepdims=True)
        rls = [jnp.where(t4 >= L_CMP - 1, 1.0 / ls[g], 0.0) for g in groups]
        os_ = [jnp.zeros((HEAD_DIM, nq), F32) for _ in groups]
        for g in groups:
            ps_ref[g, 0:PS_PAD, :] = jnp.zeros((PS_PAD, Q_BLK), F32)
            if n_rows < nc:
                ps_ref[g, PS_PAD + n_rows:PS_PAD + nc, :] = jnp.zeros((nc - n_rows, Q_BLK), F32)
        for rs in chunks:
            for g in groups:
                pc = sc_ref[g, rs, :] * rls[g]
                os_[g] = os_[g] + _dot(vct_ref[g, 0:HEAD_DIM, rs], pc.astype(BF16))
                psum = pc[:, 0:Q_BLK]
                for h in range(1, HPG):
                    psum = psum + pc[:, h * Q_BLK:(h + 1) * Q_BLK]
                ps_ref[g, PS_PAD + rs.start:PS_PAD + rs.stop, :] = psum
        ratio = L_SEL // D_CMP
        imps = []
        for g in groups:
            imp = jnp.zeros((nb, Q_BLK), F32)
            for k in range(1 - L_CMP // D_CMP, ratio):
                imp = imp + ps_ref[g, pl.ds(PS_PAD + k, nb, stride=ratio), :]
            imps.append(imp)
        return tuple(os_) + tuple(imps)

    if nc % (2 * CMP_CHUNK) == 0:
        last_complete = ((i + 1) * Q_BLK - L_CMP) // D_CMP
        cmp_out = lax.cond(last_complete < nc // 2, lambda: compressed(nc // 2), lambda: compressed(nc))
    else:
        cmp_out = compressed(nc)
    o_cmps, imps = cmp_out[:ATT_GROUPS], cmp_out[ATT_GROUPS:]

    def select(imps):
        ws = [imp + cand_off for imp in imps]
        for _ in range(N_SELECT - N_FORCED):
            for g in range(ATT_GROUPS):
                w = ws[g]
                m = jnp.max(w, axis=0, keepdims=True)
                first = jnp.min(jnp.where(w == m, j_io, nb), axis=0, keepdims=True)
                first = jnp.where(m > 0.5 * NEG, first, -1)
                ws[g] = jnp.where(j_io == first, NEG, w)
        for g in range(ATT_GROUPS):
            bias1 = jnp.where(ws[g] < 0.5 * NEG, past_off, NEG)
            bias_ref[g] = jnp.concatenate([bias1] * HPG, axis=1)

    def local():
        sds = [_dot(ks_ref[g, pl.ds(doff, Q_BLK), :], q_plains[g]) + diag_mask for g in groups]
        sws = [_dot(kw_ref[g, pl.ds(wstart, span), :], q_plains[g]) + win_mask for g in groups]
        out = []
        for g in groups:
            m0 = jnp.max(sds[g], axis=0, keepdims=True)
            acc0 = _dot(vst_ref[g, :, pl.ds(doff, Q_BLK)], jnp.exp2(sds[g] - m0).astype(BF16))
            pw = jnp.exp2(sws[g] - jnp.max(sws[g], axis=0, keepdims=True)).astype(BF16)
            acc_w = _dot(vwt_ref[g, :, pl.ds(wstart, span)], pw)
            out.append((acc_w[0:HEAD_DIM] / acc_w[HEAD_DIM:HEAD_DIM + 1], m0, acc0))
        return out

    def query_operand(g, q4, c):
        brow = bias_ref[g, pl.ds(aligned(c * SEL_BLKS, SEL_BLKS), SEL_BLKS), :]
        b16 = jnp.concatenate([brow, bpad], axis=0).astype(BF16)
        return jnp.concatenate([q4, b16, vpad], axis=0)

    def phase(g, q4, c, slot, state, do_scores=True, do_softmax=True, do_values=True):
        m_run, alpha, acc, cmax = state
        other = 1 - slot
        if do_values:
            voff = aligned(c * SEL_CHUNK, SEL_CHUNK)
            acc = alpha * acc + _dot(vst_ref[g, :, pl.ds(voff, SEL_CHUNK)], p_refs[g][slot][...])
        if do_scores:
            koff = aligned((c + 2) * SEL_CHUNK, SEL_CHUNK)
            s = _dot(ks_ref[g, pl.ds(koff, SEL_CHUNK), :], query_operand(g, q4, c + 2))
            s_refs[g][slot][...] = s.astype(BF16)
            new_max = jnp.max(s, axis=0, keepdims=True)
        if do_softmax:
            m_new = jnp.maximum(m_run, cmax[other])
            p_refs[g][other][...] = jnp.exp2(s_refs[g][other][...] - m_new.astype(BF16))
            m_run, alpha = m_new, jnp.exp2(m_run - m_new)
        if do_scores:
            cmax = (new_max, cmax[1]) if slot == 0 else (cmax[0], new_max)
        return m_run, alpha, acc, cmax

    select(imps)
    heads = [(q4s[g], o_cmps[g]) + loc for g, loc in enumerate(local())]

    def pair(k, carries):
        out = []
        for g in range(ATT_GROUPS):
            q4 = heads[g][0]
            state = phase(g, q4, 2 * k, 0, carries[g])
            out.append(phase(g, q4, 2 * k + 1, 1, state))
        return tuple(out)

    n_pairs = jnp.maximum((i * Q_BLK + 2 * SEL_CHUNK - 1) // (2 * SEL_CHUNK), 1)
    init = []
    for g in range(ATT_GROUPS):
        q4, _, _, m0, acc0 = heads[g]
        neg_row = jnp.full((1, nq), NEG, F32)
        state = (m0, jnp.ones((1, nq), F32), acc0, (neg_row, neg_row))
        state = phase(g, q4, -2, 0, state, do_softmax=False, do_values=False)
        init.append(phase(g, q4, -1, 1, state, do_values=False))
    carries = lax.fori_loop(0, n_pairs - 1, pair, tuple(init))
    last = 2 * (n_pairs - 1)
    for g in range(ATT_GROUPS):
        q4, o_cmp, o_win, _, _ = heads[g]
        state = phase(g, q4, last, 0, carries[g], do_scores=False)
        _, _, acc_s, _ = phase(g, q4, last + 1, 1, state, do_scores=False, do_softmax=False)
        o_sel = acc_s[0:HEAD_DIM] / acc_s[HEAD_DIM:HEAD_DIM + 1]
        gates = gate_ref[g]
        outs = []
        for h in range(HPG):
            cols = slice(h * Q_BLK, (h + 1) * Q_BLK)
            outs.append(o_cmp[:, cols] * gates[3 * h:3 * h + 1]
                        + o_sel[:, cols] * gates[3 * h + 1:3 * h + 2]
                        + o_win[:, cols] * gates[3 * h + 2:3 * h + 3])
        o_ref[g * rows:(g + 1) * rows, :] = jnp.concatenate(outs, axis=0).astype(BF16)


def _attention(q_t, gate_t, kc, vct, ks, vst, kw, vwt):
    nqd, s = q_t.shape
    nc = kc.shape[1]
    nb = s // L_SEL
    nq = HPG * Q_BLK
    rows = ATT_GROUPS * HPG * HEAD_DIM
    assert (s // SEL_CHUNK) % 2 == 0 and s >= WINDOW + Q_BLK and N_KV % ATT_GROUPS == 0 and nb >= N_SELECT
    qq = jnp.arange(Q_BLK)[None, :]
    rc = jnp.arange(2 * nc)[:, None]
    cmp_mask = jnp.where(D_CMP * (rc - nc) + L_CMP - 1 <= qq, 0.0, NEG).astype(F32)
    rw = jnp.arange(2 * WINDOW + Q_BLK)[:, None]
    win_mask = jnp.where((qq < rw) & (rw <= qq + WINDOW), 0.0, NEG).astype(F32)
    return pl.pallas_call(
        _attn_kernel,
        grid=(N_KV // ATT_GROUPS, s // Q_BLK),
        in_specs=[pl.BlockSpec((rows, Q_BLK), lambda g, i: (g, i)),
                  pl.BlockSpec((ATT_GROUPS, 16, Q_BLK), lambda g, i: (g, 0, i)),
                  _resident(cmp_mask.shape, lambda g, i: (0, 0)),
                  _resident(win_mask.shape, lambda g, i: (0, 0)),
                  _resident((ATT_GROUPS, nc, LANES), lambda g, i: (g, 0, 0)),
                  _resident((ATT_GROUPS, LANES, nc), lambda g, i: (g, 0, 0)),
                  _resident((ATT_GROUPS, s, LANES), lambda g, i: (g, 0, 0)),
                  _resident((ATT_GROUPS, V_ROWS, s), lambda g, i: (g, 0, 0)),
                  _resident((ATT_GROUPS, s, LANES), lambda g, i: (g, 0, 0)),
                  _resident((ATT_GROUPS, V_ROWS, s), lambda g, i: (g, 0, 0))],
        out_specs=pl.BlockSpec((rows, Q_BLK), lambda g, i: (g, i)),
        out_shape=jax.ShapeDtypeStruct((nqd, s), BF16),
        scratch_shapes=[pltpu.VMEM((ATT_GROUPS, nb, nq), F32),
                        pltpu.VMEM((ATT_GROUPS, PS_PAD + nc, Q_BLK), F32),
                        pltpu.VMEM((ATT_GROUPS, nc, nq), F32),
                        *[pltpu.VMEM((SEL_CHUNK, nq), BF16) for _ in range(2 * ATT_GROUPS)],
                        *[pltpu.VMEM((SEL_CHUNK, nq), BF16) for _ in range(2 * ATT_GROUPS)]],
        compiler_params=_params("arbitrary", "arbitrary"),
        name="nsa_attention",
    )(q_t, gate_t, cmp_mask, win_mask, kc, vct, ks, vst, kw, vwt)


def _outproj_kernel(ot_ref, x_ref, vec_ref, w_ref, o_ref):
    vec = vec_ref[...]
    y = _dot_tn(ot_ref[...], w_ref[...])
    o_ref[...] = _post_residual(x_ref[...], y, vec[3:4], vec[4:5])


def _outproj(o_t, x, vec, w_out):
    s, d = x.shape
    nqd = o_t.shape[0]
    return pl.pallas_call(
        _outproj_kernel,
        grid=(s // ROW_TILE,),
        in_specs=[pl.BlockSpec((nqd, ROW_TILE), lambda i: (0, i)),
                  pl.BlockSpec((ROW_TILE, d), lambda i: (i, 0)),
                  _resident((8, d), lambda i: (0, 0)),
                  _resident((nqd, d), lambda i: (0, 0))],
        out_specs=pl.BlockSpec((ROW_TILE, d), lambda i: (i, 0)),
        out_shape=jax.ShapeDtypeStruct((s, d), F32),
        compiler_params=_params("arbitrary"),
        name="nsa_outproj",
    )(o_t, x, vec, w_out)


def _vec(pre_g, mod, post_g):
    d = pre_g.shape[0]
    shift, scale, gate = mod[:d], mod[d:2 * d], mod[2 * d:3 * d]
    z = jnp.zeros((d,), F32)
    return jnp.stack([pre_g, scale, shift, post_g, gate, z, z, z])


def kernel(x, c, ada_w, ada_b, norm_pre_g, norm_post_g, a_w_in, a_ln_g, a_ln_b, a_ws, a_bs, a_w_out, kv_norm_g, kv_ada_w, kv_ada_b, w_kv, cmp_pos, cmp_w1, cmp_b1, cmp_w2, cmp_b2, b_w_in, b_w_out, ffn_w_gate, ffn_w_up, ffn_w_down, moe_router, moe_router_b, moe_w_gate, moe_w_up, moe_w_down):
    batch, s, d = x.shape
    assert batch == 1 and s % SEL_CHUNK == 0 and s >= WINDOW + Q_BLK
    depth = ada_w.shape[0]
    n_a = depth // 2
    xs = x.reshape(s, d)

    mods = _ada(c, ada_w.reshape(depth * 2, d, 3 * d), ada_b.reshape(depth * 2, 3 * d)).reshape(depth, 2, 3 * d)
    kv_mod = _ada(c, kv_ada_w.reshape(1, d, 2 * d), kv_ada_b.reshape(1, 2 * d))[0]
    shared = None
    n_moe, _, _, f_moe = moe_w_gate.shape
    moe_wg = moe_w_gate.astype(BF16).reshape(n_moe * N_EXPERTS, d, f_moe)
    moe_wu = moe_w_up.astype(BF16).reshape(n_moe * N_EXPERTS, d, f_moe)
    moe_wd = moe_w_down.astype(BF16).reshape(n_moe * N_EXPERTS, f_moe, d)

    for layer in range(depth):
        vec = _vec(norm_pre_g[layer, 0], mods[layer, 0], norm_post_g[layer, 0])
        if layer < n_a:
            xs = _gmlp_layer(xs, vec, a_w_in[layer].astype(BF16),
                             jnp.stack([a_ln_g[layer], a_ln_b[layer]]),
                             a_ws[layer], a_bs[layer].T, a_w_out[layer].astype(BF16))
        else:
            if shared is None:
                kv_vec = _vec(kv_norm_g, jnp.concatenate([kv_mod, jnp.zeros((d,), F32)]), jnp.zeros((d,), F32))
                gd = N_KV * HEAD_DIM
                w_vt = jnp.concatenate([w_kv[:, 3 * gd:4 * gd], w_kv[:, 5 * gd:6 * gd]], axis=1).T
                raw, ks, kw, vst, vwt = _kvproj(xs, kv_vec, w_kv.astype(BF16), w_vt.astype(BF16))
                nc = s // D_CMP
                width = D_CMP * HEAD_DIM
                hid = cmp_w1.shape[-1]
                w2p = jnp.pad(cmp_w2, ((0, 0), (0, 0), (0, LANES - HEAD_DIM))).astype(BF16)
                b2p = jnp.pad(cmp_b2, ((0, 0), (0, LANES - HEAD_DIM))).reshape(2, 1, LANES)
                cmp_n, cmp_t = _compress(raw.reshape(2, N_KV, nc, width),
                                         cmp_pos.reshape(2, 2, 1, width),
                                         cmp_w1.reshape(2, 2, width, hid).astype(BF16),
                                         cmp_b1.reshape(2, 1, hid), w2p, b2p)
                shared = (cmp_n[0], cmp_t[1], ks, vst, kw, vwt)
            i = layer - n_a
            nq = N_HEADS * HEAD_DIM
            w_t = b_w_in[i].T.astype(BF16)
            q_t, gate_t = _qproj(xs, vec, w_t)
            kc, vct, ks, vst, kw, vwt = shared
            o_t = _attention(q_t, gate_t, kc, vct, ks, vst, kw, vwt)
            xs = _outproj(o_t, xs, vec, b_w_out[i].astype(BF16))

        vec = _vec(norm_pre_g[layer, 1], mods[layer, 1], norm_post_g[layer, 1])
        j = layer // 2
        if layer % 2 == 0:
            xs = _swiglu_layer(xs, vec, ffn_w_gate[j].astype(BF16), ffn_w_up[j].astype(BF16),
                               ffn_w_down[j].astype(BF16))
        else:
            xs = _moe_layer(xs, vec, moe_router[j].T, moe_router_b[j].reshape(N_EXPERTS, 1),
                            moe_wg, moe_wu, moe_wd, j * N_EXPERTS)
    return xs.reshape(batch, s, d)
```

```python
import jax
import jax.numpy as jnp
from jax import lax
from jax.experimental import pallas as pl
from jax.experimental.pallas import tpu as pltpu

F32 = jnp.float32
BF16 = jnp.bfloat16
I32 = jnp.int32

EPS = 1e-6
NEG = -1e30

LANES = 128
VMEM_LIMIT_BYTES = 56 * 1024 * 1024

CHUNK = 128
A_GROUPS = 8
N_HEADS = 16
N_KV = 4
HPG = N_HEADS // N_KV
HEAD_DIM = 64
L_CMP = 32
D_CMP = 16
L_SEL = 64
SEL_SHIFT = 6
N_SELECT = 16
N_FORCED = 3
WINDOW = 512
Q_BLK = 128
N_EXPERTS = 8
MOE_BLOCK = 256

ROW_TILE = 512
TOK_TILE = 256
DISP_TILE = 512
SEL_CHUNK = 512
SEL_BLKS = SEL_CHUNK // L_SEL
CMP_CHUNK = 128
ATT_GROUPS = 2
PS_PAD = 8
V_ROWS = 72
F_CHUNK = 512
WIN_PART_ROWS = 128
WIN_PARTS = TOK_TILE // WIN_PART_ROWS + 1
WIN_ROWS = WIN_PARTS * WIN_PART_ROWS
LOG2E = 1.4426950408889634


def _params(*sem):
    return pltpu.CompilerParams(dimension_semantics=sem, vmem_limit_bytes=VMEM_LIMIT_BYTES)


def _resident(shape, index_map):
    return pl.BlockSpec(shape, index_map, pipeline_mode=pl.Buffered(1))


def _split_bf16(a):
    hi = a.astype(BF16)
    lo = (a - hi.astype(F32)).astype(BF16)
    return hi, lo


def _dot(a, b):
    return jnp.dot(a, b, preferred_element_type=F32)


def _dot_nt(a, b):
    return lax.dot_general(a, b, (((1,), (1,)), ((), ())), preferred_element_type=F32)


def _dot_tn(a, b):
    return lax.dot_general(a, b, (((0,), (0,)), ((), ())), preferred_element_type=F32)


def _prenorm(x, g, scale, shift):
    ms = jnp.mean(x * x, axis=-1, keepdims=True)
    return (x * lax.rsqrt(ms + EPS) * g) * (1.0 + scale) + shift


def _post_residual(x, y, g, gate):
    ms = jnp.mean(y * y, axis=-1, keepdims=True)
    return x + gate * (y * lax.rsqrt(ms + EPS) * g)


def _ada_kernel(c_ref, w_ref, b_ref, o_ref):
    c = c_ref[...]
    c_act = jnp.broadcast_to(c * jax.nn.sigmoid(c), (8, c.shape[1]))
    c_hi, c_lo = _split_bf16(c_act)
    w_hi, w_lo = _split_bf16(w_ref[...])
    m = _dot(c_hi, w_hi) + (_dot(c_hi, w_lo) + _dot(c_lo, w_hi))
    o_ref[...] = m[0:1] + b_ref[...]


def _ada(c, w, b):
    n, d, nn = w.shape
    out = pl.pallas_call(
        _ada_kernel,
        grid=(n, nn // d),
        in_specs=[pl.BlockSpec((1, d), lambda i, j: (0, 0)),
                  pl.BlockSpec((None, d, d), lambda i, j: (i, 0, j)),
                  pl.BlockSpec((None, 1, d), lambda i, j: (i, 0, j))],
        out_specs=pl.BlockSpec((None, 1, d), lambda i, j: (i, 0, j)),
        out_shape=jax.ShapeDtypeStruct((n, 1, nn), F32),
        compiler_params=_params("arbitrary", "arbitrary"),
        name="ada",
    )(c, w, b.reshape(n, 1, nn))
    return out.reshape(n, nn)


def _gmlp_kernel(x_ref, vec_ref, win_ref, ln_ref, ws_ref, bst_ref, wout_ref, o_ref, gated_ref):
    x = x_ref[...]
    vec = vec_ref[...]
    h = _prenorm(x, vec[0:1], vec[1:2], vec[2:3]).astype(BF16)
    z = jax.nn.gelu(_dot(h, win_ref[...]))
    width = z.shape[1] // 2
    u = z[:, :width]
    v = z[:, width:]
    mu = jnp.mean(v, axis=-1, keepdims=True)
    vc = v - mu
    var = jnp.mean(vc * vc, axis=-1, keepdims=True)
    ln = ln_ref[...]
    vn = (vc * lax.rsqrt(var + EPS) * ln[0:1] + ln[1:2]).astype(BF16)
    causal = (lax.broadcasted_iota(I32, (CHUNK, CHUNK), 0)
              >= lax.broadcasted_iota(I32, (CHUNK, CHUNK), 1))
    gw = width // A_GROUPS
    bst = bst_ref[...]
    for g in range(A_GROUPS):
        wg = jnp.where(causal, ws_ref[g], 0.0).astype(BF16)
        for ck in range(x.shape[0] // CHUNK):
            rows = slice(ck * CHUNK, (ck + 1) * CHUNK)
            cols = slice(g * gw, (g + 1) * gw)
            mixed = _dot(wg, vn[rows, cols]) + bst[:, g:g + 1]
            gated_ref[rows, cols] = (u[rows, cols] * mixed).astype(BF16)
    y = _dot(gated_ref[...], wout_ref[...])
    o_ref[...] = _post_residual(x, y, vec[3:4], vec[4:5])


def _gmlp_layer(x, vec, w_in, ln, ws, bst, w_out):
    s, d = x.shape
    e2 = w_in.shape[1]
    return pl.pallas_call(
        _gmlp_kernel,
        grid=(s // ROW_TILE,),
        in_specs=[pl.BlockSpec((ROW_TILE, d), lambda i: (i, 0)),
                  _resident((8, d), lambda i: (0, 0)),
                  _resident((d, e2), lambda i: (0, 0)),
                  _resident((2, e2 // 2), lambda i: (0, 0)),
                  _resident(ws.shape, lambda i: (0, 0, 0)),
                  _resident(bst.shape, lambda i: (0, 0)),
                  _resident((e2 // 2, d), lambda i: (0, 0))],
        out_specs=pl.BlockSpec((ROW_TILE, d), lambda i: (i, 0)),
        out_shape=jax.ShapeDtypeStruct((s, d), F32),
        scratch_shapes=[pltpu.VMEM((ROW_TILE, e2 // 2), BF16)],
        compiler_params=_params("arbitrary"),
        name="gmlp",
    )(x, vec, w_in, ln, ws, bst, w_out)


def _swiglu_kernel(x_ref, vec_ref, wg_ref, wu_ref, wd_ref, o_ref):
    x = x_ref[...]
    vec = vec_ref[...]
    h = _prenorm(x, vec[0:1], vec[1:2], vec[2:3]).astype(BF16)
    g = _dot(h, wg_ref[...])
    a = (g * jax.nn.sigmoid(g) * _dot(h, wu_ref[...])).astype(BF16)
    y = _dot(a, wd_ref[...])
    o_ref[...] = _post_residual(x, y, vec[3:4], vec[4:5])


def _swiglu_layer(x, vec, w_gate, w_up, w_down):
    s, d = x.shape
    f = w_gate.shape[1]
    return pl.pallas_call(
        _swiglu_kernel,
        grid=(s // ROW_TILE,),
        in_specs=[pl.BlockSpec((ROW_TILE, d), lambda i: (i, 0)),
                  _resident((8, d), lambda i: (0, 0)),
                  _resident((d, f), lambda i: (0, 0)),
                  _resident((d, f), lambda i: (0, 0)),
                  _resident((f, d), lambda i: (0, 0))],
        out_specs=pl.BlockSpec((ROW_TILE, d), lambda i: (i, 0)),
        out_shape=jax.ShapeDtypeStruct((s, d), F32),
        compiler_params=_params("arbitrary"),
        name="swiglu",
    )(x, vec, w_gate, w_up, w_down)


def _router_kernel(x_ref, vec_ref, wrt_ref, br_ref, h_ref, idx_ref, gate_ref):
    x = x_ref[...]
    vec = vec_ref[...]
    h = _prenorm(x, vec[0:1], vec[1:2], vec[2:3])
    h_ref[...] = h.astype(BF16)
    h_hi, h_lo = _split_bf16(h)
    w_hi, w_lo = _split_bf16(wrt_ref[...])
    logit = _dot_nt(w_hi, h_hi) + (_dot_nt(w_hi, h_lo) + _dot_nt(w_lo, h_hi)) + br_ref[...]
    ne = logit.shape[0]
    eidx = lax.broadcasted_iota(I32, logit.shape, 0)
    m1 = jnp.max(logit, axis=0, keepdims=True)
    i1 = jnp.min(jnp.where(logit == m1, eidx, ne), axis=0, keepdims=True)
    rest = jnp.where(eidx == i1, -jnp.inf, logit)
    m2 = jnp.max(rest, axis=0, keepdims=True)
    i2 = jnp.min(jnp.where(rest == m2, eidx, ne), axis=0, keepdims=True)
    e21 = jnp.exp(m2 - m1)
    g1 = 1.0 / (1.0 + e21)
    g2 = e21 * g1
    pad_i = jnp.zeros((6, i1.shape[1]), I32)
    pad_f = jnp.zeros((6, i1.shape[1]), F32)
    idx_ref[...] = jnp.concatenate([i1, i2, pad_i], axis=0)
    gate_ref[...] = jnp.concatenate([g1, g2, pad_f], axis=0)


def _router(x, vec, w_rt, b_r):
    s, d = x.shape
    ne = w_rt.shape[0]
    return pl.pallas_call(
        _router_kernel,
        grid=(s // ROW_TILE,),
        in_specs=[pl.BlockSpec((ROW_TILE, d), lambda i: (i, 0)),
                  _resident((8, d), lambda i: (0, 0)),
                  _resident((ne, d), lambda i: (0, 0)),
                  _resident((ne, 1), lambda i: (0, 0))],
        out_specs=[pl.BlockSpec((ROW_TILE, d), lambda i: (i, 0)),
                   pl.BlockSpec((8, ROW_TILE), lambda i: (0, i)),
                   pl.BlockSpec((8, ROW_TILE), lambda i: (0, i))],
        out_shape=[jax.ShapeDtypeStruct((s, d), BF16),
                   jax.ShapeDtypeStruct((8, s), I32),
                   jax.ShapeDtypeStruct((8, s), F32)],
        compiler_params=_params("arbitrary"),
        name="moe_router",
    )(x, vec, w_rt, b_r)


def _rank_kernel(idx_ref, rank_ref, start_ref, count_ref, carry_ref):
    @pl.when(pl.program_id(0) == 0)
    def _():
        carry_ref[...] = jnp.zeros_like(carry_ref)

    idx = idx_ref[...]
    tt = idx.shape[1]
    eidx = lax.broadcasted_iota(I32, (N_EXPERTS, tt), 0)
    hit1 = eidx == idx[0:1]
    hit2 = eidx == idx[1:2]
    member = jnp.where(hit1 | hit2, 1.0, 0.0)
    before = (lax.broadcasted_iota(I32, (tt, tt), 0) < lax.broadcasted_iota(I32, (tt, tt), 1))
    carry = carry_ref[...]
    cum = _dot(member.astype(BF16), jnp.where(before, 1.0, 0.0).astype(BF16)) + carry[:, 0:1]
    r1 = jnp.sum(jnp.where(hit1, cum, 0.0), axis=0, keepdims=True)
    r2 = jnp.sum(jnp.where(hit2, cum, 0.0), axis=0, keepdims=True)
    rank_ref[...] = jnp.concatenate([r1, r2, jnp.zeros((6, tt), F32)], axis=0).astype(I32)
    tile_count = jnp.broadcast_to(jnp.sum(member, axis=1, keepdims=True), carry.shape)
    start_ref[...] = carry
    count_ref[...] = tile_count
    carry_ref[...] = carry + tile_count


def _ranks(idx):
    s = idx.shape[1]
    nt = s // TOK_TILE
    return pl.pallas_call(
        _rank_kernel,
        grid=(nt,),
        in_specs=[pl.BlockSpec((8, TOK_TILE), lambda i: (0, i))],
        out_specs=[pl.BlockSpec((8, TOK_TILE), lambda i: (0, i)),
                   pl.BlockSpec((None, N_EXPERTS, LANES), lambda i: (i, 0, 0)),
                   pl.BlockSpec((None, N_EXPERTS, LANES), lambda i: (i, 0, 0))],
        out_shape=[jax.ShapeDtypeStruct((8, s), I32),
                   jax.ShapeDtypeStruct((nt, N_EXPERTS, LANES), F32),
                   jax.ShapeDtypeStruct((nt, N_EXPERTS, LANES), F32)],
        scratch_shapes=[pltpu.VMEM((N_EXPERTS, LANES), F32)],
        compiler_params=_params("arbitrary"),
        name="moe_ranks",
    )(idx)


def _match(idx, rank, expert, rows):
    r1 = jnp.where(idx[0:1] == expert, rank[0:1], -1)
    r2 = jnp.where(idx[1:2] == expert, rank[1:2], -1)
    return rows == r1, rows == r2


def _dispatch_kernel(be_ref, lb_ref, tlo_ref, thi_ref, idx_ref, rank_ref, gate_ref, h_ref, o_ref, rg_ref,
                     acc_ref, gacc_ref):
    b = pl.program_id(0)
    expert = be_ref[b]
    rows = lax.broadcasted_iota(I32, (MOE_BLOCK, DISP_TILE), 0) + lb_ref[b]
    acc_ref[...] = jnp.zeros_like(acc_ref)
    gacc_ref[...] = jnp.zeros_like(gacc_ref)

    def body(t, carry):
        off = pl.multiple_of(t * DISP_TILE, DISP_TILE)
        m1, m2 = _match(idx_ref[:, pl.ds(off, DISP_TILE)], rank_ref[:, pl.ds(off, DISP_TILE)], expert, rows)
        gates = gate_ref[:, pl.ds(off, DISP_TILE)]
        gacc_ref[...] += jnp.sum(jnp.where(m1, gates[0:1], 0.0) + jnp.where(m2, gates[1:2], 0.0),
                                 axis=1, keepdims=True)
        onehot = jnp.where(m1 | m2, 1.0, 0.0).astype(BF16)
        acc_ref[...] += _dot(onehot, h_ref[pl.ds(off, DISP_TILE), :])
        return carry

    lax.fori_loop(tlo_ref[b], thi_ref[b], body, 0)
    o_ref[...] = acc_ref[...].astype(BF16)
    rg_ref[...] = gacc_ref[...]


def _dispatch(blk_e, blk_lb, blk_tlo, blk_thi, idx, rank, gates, h):
    s, d = h.shape
    n_blk = blk_e.shape[0]
    grid_spec = pltpu.PrefetchScalarGridSpec(
        num_scalar_prefetch=4,
        grid=(n_blk,),
        in_specs=[_resident((8, s), lambda b, *_: (0, 0)),
                  _resident((8, s), lambda b, *_: (0, 0)),
                  _resident((8, s), lambda b, *_: (0, 0)),
                  _resident((s, d), lambda b, *_: (0, 0))],
        out_specs=[pl.BlockSpec((MOE_BLOCK, d), lambda b, *_: (b, 0)),
                   pl.BlockSpec((MOE_BLOCK, 1), lambda b, *_: (b, 0))],
        scratch_shapes=[pltpu.VMEM((MOE_BLOCK, d), F32), pltpu.VMEM((MOE_BLOCK, 1), F32)],
    )
    return pl.pallas_call(
        _dispatch_kernel,
        grid_spec=grid_spec,
        out_shape=[jax.ShapeDtypeStruct((n_blk * MOE_BLOCK, d), BF16),
                   jax.ShapeDtypeStruct((n_blk * MOE_BLOCK, 1), F32)],
        compiler_params=_params("arbitrary"),
        name="moe_dispatch",
    )(blk_e, blk_lb, blk_tlo, blk_thi, idx, rank, gates, h)


def _expert_kernel(be_ref, used_ref, x_ref, rg_ref, wg_ref, wu_ref, wd_ref, o_ref):
    b = pl.program_id(0)

    @pl.when(b < used_ref[0])
    def _():
        x = x_ref[...]
        f = wg_ref.shape[1]
        y = jnp.zeros(o_ref.shape, F32)
        for c in range(f // F_CHUNK):
            cols = slice(c * F_CHUNK, (c + 1) * F_CHUNK)
            g = _dot(x, wg_ref[:, cols])
            a = (g * jax.nn.sigmoid(g) * _dot(x, wu_ref[:, cols])).astype(BF16)
            y = y + _dot(a, wd_ref[cols, :])
        o_ref[...] = (y * rg_ref[...]).astype(BF16)

    @pl.when(b >= used_ref[0])
    def _():
        o_ref[...] = jnp.zeros_like(o_ref)


def _experts(blk_e, n_used, x_buf, row_gate, w_gate, w_up, w_down):
    n_rows, d = x_buf.shape
    f = w_gate.shape[2]
    assert f % F_CHUNK == 0
    grid_spec = pltpu.PrefetchScalarGridSpec(
        num_scalar_prefetch=2,
        grid=(n_rows // MOE_BLOCK,),
        in_specs=[pl.BlockSpec((MOE_BLOCK, d), lambda b, be, nu: (b, 0)),
                  pl.BlockSpec((MOE_BLOCK, 1), lambda b, be, nu: (b, 0)),
                  pl.BlockSpec((None, d, f), lambda b, be, nu: (be[b], 0, 0)),
                  pl.BlockSpec((None, d, f), lambda b, be, nu: (be[b], 0, 0)),
                  pl.BlockSpec((None, f, d), lambda b, be, nu: (be[b], 0, 0))],
        out_specs=pl.BlockSpec((MOE_BLOCK, d), lambda b, be, nu: (b, 0)),
    )
    return pl.pallas_call(
        _expert_kernel,
        grid_spec=grid_spec,
        out_shape=jax.ShapeDtypeStruct((n_rows, d), BF16),
        compiler_params=_params("arbitrary"),
        name="moe_experts",
    )(blk_e, n_used, x_buf, row_gate, w_gate, w_up, w_down)


def _combine_kernel(win_ref, lb_ref, idx_ref, rank_ref, *refs):
    y_refs = refs[:N_EXPERTS * WIN_PARTS]
    x_ref, vec_ref, o_ref = refs[N_EXPERTS * WIN_PARTS:]
    t = pl.program_id(0)
    idx = idx_ref[...]
    rank = rank_ref[...]
    acc = jnp.zeros(o_ref.shape, F32)
    for e in range(N_EXPERTS):
        rows = lax.broadcasted_iota(I32, (WIN_ROWS, TOK_TILE), 0) + lb_ref[t * N_EXPERTS + e]
        m1, m2 = _match(idx, rank, e, rows)
        onehot = jnp.where(m1 | m2, 1.0, 0.0).astype(BF16)
        y = jnp.concatenate([y_refs[e * WIN_PARTS + k][...] for k in range(WIN_PARTS)], axis=0)
        acc = acc + _dot_tn(onehot, y)
    vec = vec_ref[...]
    o_ref[...] = _post_residual(x_ref[...], acc, vec[3:4], vec[4:5])


def _combine(win_start, win_lb, idx, rank, y_buf, x, vec):
    s, d = x.shape

    def y_spec(e, k):
        return pl.BlockSpec((WIN_PART_ROWS, d), lambda t, ws, lb: (ws[t * N_EXPERTS + e] + k, 0))

    grid_spec = pltpu.PrefetchScalarGridSpec(
        num_scalar_prefetch=2,
        grid=(s // TOK_TILE,),
        in_specs=[pl.BlockSpec((8, TOK_TILE), lambda t, *_: (0, t)),
                  pl.BlockSpec((8, TOK_TILE), lambda t, *_: (0, t))]
                 + [y_spec(e, k) for e in range(N_EXPERTS) for k in range(WIN_PARTS)]
                 + [pl.BlockSpec((TOK_TILE, d), lambda t, *_: (t, 0)),
                    _resident((8, d), lambda t, *_: (0, 0))],
        out_specs=pl.BlockSpec((TOK_TILE, d), lambda t, *_: (t, 0)),
    )
    return pl.pallas_call(
        _combine_kernel,
        grid_spec=grid_spec,
        out_shape=jax.ShapeDtypeStruct((s, d), F32),
        compiler_params=_params("arbitrary"),
        name="moe_combine",
    )(win_start, win_lb, idx, rank, *([y_buf] * (N_EXPERTS * WIN_PARTS)), x, vec)


def _moe_layer(x, vec, w_rt, b_r, w_gate, w_up, w_down, w_first):
    s, d = x.shape
    nt = s // TOK_TILE
    h, idx, gates = _router(x, vec, w_rt, b_r)
    rank, tile_start, tile_count = _ranks(idx)

    tile_start = tile_start[:, :, 0].astype(I32)
    tile_count = tile_count[:, :, 0].astype(I32)
    tile_end = tile_start + tile_count
    counts = tile_end[-1]
    padded = (counts + MOE_BLOCK - 1) // MOE_BLOCK * MOE_BLOCK
    pend = jnp.cumsum(padded)
    pstart = pend - padded
    n_rows = -(-(2 * s) // MOE_BLOCK) * MOE_BLOCK + N_EXPERTS * MOE_BLOCK
    n_blk = n_rows // MOE_BLOCK
    blk_row = jnp.arange(n_blk, dtype=I32) * MOE_BLOCK
    blk_e = jnp.minimum(jnp.sum(blk_row[:, None] >= pend[None, :], axis=1), N_EXPERTS - 1).astype(I32)
    blk_lb = blk_row - pstart[blk_e]
    per = DISP_TILE // TOK_TILE
    te = tile_end[per - 1::per][:, blk_e]
    ts = tile_start[::per][:, blk_e]
    blk_tlo = jnp.sum(te <= blk_lb[None, :], axis=0).astype(I32)
    blk_thi = jnp.sum(ts < (blk_lb + MOE_BLOCK)[None, :], axis=0).astype(I32)
    n_used = (pend[-1] // MOE_BLOCK).astype(I32).reshape(1)

    x_buf, row_gate = _dispatch(blk_e, blk_lb.astype(I32), blk_tlo, blk_thi, idx, rank, gates, h)
    y_buf = _experts(blk_e + w_first, n_used, x_buf, row_gate, w_gate, w_up, w_down)

    win_start = jnp.minimum((pstart[None, :] + tile_start) // WIN_PART_ROWS, n_rows // WIN_PART_ROWS - WIN_PARTS)
    win_lb = win_start * WIN_PART_ROWS - pstart[None, :]
    return _combine(win_start.reshape(-1).astype(I32), win_lb.reshape(-1).astype(I32), idx, rank, y_buf, x, vec)


def _kvproj_kernel(x_ref, vec_ref, wkv_ref, wvt_ref, raw_ref, ks_ref, kw_ref, vst_ref, vwt_ref):
    x = x_ref[...]
    vec = vec_ref[...]
    h = _prenorm(x, vec[0:1], vec[1:2], vec[2:3]).astype(BF16)
    kv = _dot(h, wkv_ref[...])
    vt = _dot_nt(wvt_ref[...], h)
    tm = x.shape[0]
    gd = N_KV * HEAD_DIM
    key_blk = jnp.right_shift(pl.program_id(0) * tm + lax.broadcasted_iota(I32, (tm, SEL_BLKS), 0), SEL_SHIFT)
    ind = jnp.where((key_blk & (SEL_BLKS - 1)) == lax.broadcasted_iota(I32, (tm, SEL_BLKS), 1), 1.0, 0.0)
    pad_s = jnp.zeros((tm, LANES - HEAD_DIM - SEL_BLKS), F32)
    pad_w = jnp.zeros((tm, LANES - HEAD_DIM), F32)
    ones_row = jnp.concatenate([jnp.ones((1, tm), F32), jnp.zeros((V_ROWS - HEAD_DIM - 1, tm), F32)], axis=0)
    for g in range(N_KV):
        c = g * HEAD_DIM
        raw_ref[0, g] = kv[:, c:c + HEAD_DIM]
        raw_ref[1, g] = kv[:, gd + c:gd + c + HEAD_DIM]
        ks = kv[:, 2 * gd + c:2 * gd + c + HEAD_DIM]
        kw = kv[:, 4 * gd + c:4 * gd + c + HEAD_DIM]
        ks_ref[g] = jnp.concatenate([ks, ind, pad_s], axis=1).astype(BF16)
        kw_ref[g] = jnp.concatenate([kw, pad_w], axis=1).astype(BF16)
        vst_ref[g] = jnp.concatenate([vt[c:c + HEAD_DIM], ones_row], axis=0).astype(BF16)
        vwt_ref[g] = jnp.concatenate([vt[gd + c:gd + c + HEAD_DIM], ones_row], axis=0).astype(BF16)


def _kvproj(x, vec, w_kv, w_vt):
    s, d = x.shape
    nkv = w_kv.shape[1]
    return pl.pallas_call(
        _kvproj_kernel,
        grid=(s // ROW_TILE,),
        in_specs=[pl.BlockSpec((ROW_TILE, d), lambda i: (i, 0)),
                  _resident((8, d), lambda i: (0, 0)),
                  _resident((d, nkv), lambda i: (0, 0)),
                  _resident(w_vt.shape, lambda i: (0, 0))],
        out_specs=[pl.BlockSpec((2, N_KV, ROW_TILE, HEAD_DIM), lambda i: (0, 0, i, 0)),
                   pl.BlockSpec((N_KV, ROW_TILE, LANES), lambda i: (0, i, 0)),
                   pl.BlockSpec((N_KV, ROW_TILE, LANES), lambda i: (0, i, 0)),
                   pl.BlockSpec((N_KV, V_ROWS, ROW_TILE), lambda i: (0, 0, i)),
                   pl.BlockSpec((N_KV, V_ROWS, ROW_TILE), lambda i: (0, 0, i))],
        out_shape=[jax.ShapeDtypeStruct((2, N_KV, s, HEAD_DIM), F32),
                   jax.ShapeDtypeStruct((N_KV, s, LANES), BF16),
                   jax.ShapeDtypeStruct((N_KV, s, LANES), BF16),
                   jax.ShapeDtypeStruct((N_KV, V_ROWS, s), BF16),
                   jax.ShapeDtypeStruct((N_KV, V_ROWS, s), BF16)],
        compiler_params=_params("arbitrary"),
        name="nsa_kvproj",
    )(x, vec, w_kv, w_vt)


def _compress_kernel(raw_ref, pos_ref, w1_ref, b1_ref, w2_ref, b2_ref, n_ref, t_ref):
    raw = raw_ref[...]
    nc = raw.shape[0]
    first = _dot((raw + pos_ref[0]).astype(BF16), w1_ref[0])
    second = _dot((raw + pos_ref[1]).astype(BF16), w1_ref[1])
    hid = jax.nn.gelu(first + pltpu.roll(second, nc - 1, 0) + b1_ref[...])
    out = _dot(hid.astype(BF16), w2_ref[...]) + b2_ref[...]
    n_ref[...] = out.astype(BF16)
    row = lax.broadcasted_iota(I32, (LANES, nc), 0)
    t_ref[...] = jnp.where(row == HEAD_DIM, 1.0, out.T).astype(BF16)


def _compress(raw, pos, w1, b1, w2, b2):
    _, g, nc, width = raw.shape
    hid = w1.shape[-1]
    return pl.pallas_call(
        _compress_kernel,
        grid=(2, g),
        in_specs=[pl.BlockSpec((None, None, nc, width), lambda j, k: (j, k, 0, 0)),
                  pl.BlockSpec((None, 2, 1, width), lambda j, k: (j, 0, 0, 0)),
                  pl.BlockSpec((None, 2, width, hid), lambda j, k: (j, 0, 0, 0)),
                  pl.BlockSpec((None, 1, hid), lambda j, k: (j, 0, 0)),
                  pl.BlockSpec((None, hid, LANES), lambda j, k: (j, 0, 0)),
                  pl.BlockSpec((None, 1, LANES), lambda j, k: (j, 0, 0))],
        out_specs=[pl.BlockSpec((None, None, nc, LANES), lambda j, k: (j, k, 0, 0)),
                   pl.BlockSpec((None, None, LANES, nc), lambda j, k: (j, k, 0, 0))],
        out_shape=[jax.ShapeDtypeStruct((2, g, nc, LANES), BF16),
                   jax.ShapeDtypeStruct((2, g, LANES, nc), BF16)],
        compiler_params=_params("arbitrary", "arbitrary"),
        name="nsa_compress",
    )(raw, pos, w1, b1, w2, b2)


def _qproj_kernel(x_ref, vec_ref, wt_ref, q_ref, gate_ref):
    x = x_ref[...]
    vec = vec_ref[...]
    h = _prenorm(x, vec[0:1], vec[1:2], vec[2:3]).astype(BF16)
    pt = _dot_nt(wt_ref[...], h)
    nq = N_HEADS * HEAD_DIM
    q_ref[...] = (pt[:nq] * (HEAD_DIM ** -0.5 * LOG2E)).astype(BF16)
    gates = jax.nn.sigmoid(pt[nq:nq + 3 * N_HEADS])
    per = 3 * HPG
    pad = jnp.zeros((16 - per, x.shape[0]), F32)
    for g in range(N_KV):
        gate_ref[g] = jnp.concatenate([gates[g * per:(g + 1) * per], pad], axis=0)


def _qproj(x, vec, w_t):
    s, d = x.shape
    return pl.pallas_call(
        _qproj_kernel,
        grid=(s // ROW_TILE,),
        in_specs=[pl.BlockSpec((ROW_TILE, d), lambda i: (i, 0)),
                  _resident((8, d), lambda i: (0, 0)),
                  _resident(w_t.shape, lambda i: (0, 0))],
        out_specs=[pl.BlockSpec((N_HEADS * HEAD_DIM, ROW_TILE), lambda i: (0, i)),
                   pl.BlockSpec((N_KV, 16, ROW_TILE), lambda i: (0, 0, i))],
        out_shape=[jax.ShapeDtypeStruct((N_HEADS * HEAD_DIM, s), BF16),
                   jax.ShapeDtypeStruct((N_KV, 16, s), F32)],
        compiler_params=_params("arbitrary"),
        name="nsa_qproj",
    )(x, vec, w_t)


def _attn_kernel(q_ref, gate_ref, cmask_ref, wmask_ref, kc_ref, vct_ref, ks_ref, vst_ref, kw_ref, vwt_ref,
                 o_ref, bias_ref, ps_ref, sc_ref, *sp_refs):
    s_refs = [[sp_refs[2 * g + k] for k in range(2)] for g in range(ATT_GROUPS)]
    p_refs = [[sp_refs[2 * ATT_GROUPS + 2 * g + k] for k in range(2)] for g in range(ATT_GROUPS)]
    i = pl.program_id(1)
    nq = HPG * Q_BLK
    nc = kc_ref.shape[1]
    nb = bias_ref.shape[1]
    rows = HPG * HEAD_DIM
    t1 = i * Q_BLK + lax.broadcasted_iota(I32, (1, Q_BLK), 1)
    t4 = jnp.concatenate([t1] * HPG, axis=1)
    j_io = lax.broadcasted_iota(I32, (nb, Q_BLK), 0)
    jt = jnp.right_shift(t1, SEL_SHIFT)
    forced = (j_io == 0) | (j_io == jt) | (j_io == jt - 1)
    cand_off = jnp.where((j_io * L_SEL <= t1) & jnp.logical_not(forced), 0.0, NEG)
    past_off = jnp.where(j_io * L_SEL < i * Q_BLK, 0.0, NEG)
    doff = pl.multiple_of(i * Q_BLK, Q_BLK)
    cmask_off = nc - i * (Q_BLK // D_CMP)
    span = WINDOW + Q_BLK
    wstart = pl.multiple_of(jnp.maximum(i * Q_BLK - WINDOW, 0), Q_BLK)
    win_mask = wmask_ref[pl.ds(pl.multiple_of(WINDOW - jnp.minimum(i * Q_BLK, WINDOW), Q_BLK), span), :]
    win_mask = jnp.concatenate([win_mask] * HPG, axis=1)
    diag_mask = jnp.concatenate([wmask_ref[WINDOW:WINDOW + Q_BLK, :]] * HPG, axis=1)
    vpad = jnp.zeros((LANES - HEAD_DIM - 2 * SEL_BLKS, nq), BF16)
    bpad = jnp.zeros((SEL_BLKS, nq), F32)

    def aligned(x, m):
        return x if isinstance(x, int) else pl.multiple_of(x, m)

    groups = range(ATT_GROUPS)
    q4s, q_plains = [], []
    for g in groups:
        qb = q_ref[g * rows:(g + 1) * rows, :]
        q4s.append(jnp.concatenate([qb[h * HEAD_DIM:(h + 1) * HEAD_DIM] for h in range(HPG)], axis=1))
        q_plains.append(jnp.concatenate([q4s[g], jnp.zeros((LANES - HEAD_DIM, nq), BF16)], axis=0))

    def compressed(n_rows):
        chunks = [slice(r0, r0 + CMP_CHUNK) for r0 in range(0, n_rows, CMP_CHUNK)]
        ms = [jnp.full((1, nq), NEG, F32) for _ in groups]
        for rs in chunks:
            mask = cmask_ref[pl.ds(pl.multiple_of(cmask_off + rs.start, 8), CMP_CHUNK), :]
            mask = jnp.concatenate([mask] * HPG, axis=1)
            for g in groups:
                s = _dot(kc_ref[g, rs, :], q_plains[g]) + mask
                sc_ref[g, rs, :] = s
                ms[g] = jnp.maximum(ms[g], jnp.max(s, axis=0, keepdims=True))
        ls = [jnp.zeros((1, nq), F32) for _ in groups]
        for rs in chunks:
            for g in groups:
                e = jnp.exp2(sc_ref[g, rs, :] - ms[g])
                sc_ref[g, rs, :] = e
                ls[g] = ls[g] + jnp.sum(e, axis=0, keepdims=True)
        rls = [jnp.where(t4 >= L_CMP - 1, 1.0 / ls[g], 0.0) for g in groups]
        os_ = [jnp.zeros((HEAD_DIM, nq), F32) for _ in groups]
        for g in groups:
            ps_ref[g, 0:PS_PAD, :] = jnp.zeros((PS_PAD, Q_BLK), F32)
            if n_rows < nc:
                ps_ref[g, PS_PAD + n_rows:PS_PAD + nc, :] = jnp.zeros((nc - n_rows, Q_BLK), F32)
        for rs in chunks:
            for g in groups:
                pc = sc_ref[g, rs, :] * rls[g]
                os_[g] = os_[g] + _dot(vct_ref[g, 0:HEAD_DIM, rs], pc.astype(BF16))
                psum = pc[:, 0:Q_BLK]
                for h in range(1, HPG):
                    psum = psum + pc[:, h * Q_BLK:(h + 1) * Q_BLK]
                ps_ref[g, PS_PAD + rs.start:PS_PAD + rs.stop, :] = psum
        ratio = L_SEL // D_CMP
        imps = []
        for g in groups:
            imp = jnp.zeros((nb, Q_BLK), F32)
            for k in range(1 - L_CMP // D_CMP, ratio):
                imp = imp + ps_ref[g, pl.ds(PS_PAD + k, nb, stride=ratio), :]
            imps.append(imp)
        return tuple(os_) + tuple(imps)

    if nc % (2 * CMP_CHUNK) == 0:
        last_complete = ((i + 1) * Q_BLK - L_CMP) // D_CMP
        cmp_out = lax.cond(last_complete < nc // 2, lambda: compressed(nc // 2), lambda: compressed(nc))
    else:
        cmp_out = compressed(nc)
    o_cmps, imps = cmp_out[:ATT_GROUPS], cmp_out[ATT_GROUPS:]

    def select(imps):
        ws = [imp + cand_off for imp in imps]
        for _ in range(N_SELECT - N_FORCED):
            for g in range(ATT_GROUPS):
                w = ws[g]
                m = jnp.max(w, axis=0, keepdims=True)
                first = jnp.min(jnp.where(w == m, j_io, nb), axis=0, keepdims=True)
                first = jnp.where(m > 0.5 * NEG, first, -1)
                ws[g] = jnp.where(j_io == first, NEG, w)
        for g in range(ATT_GROUPS):
            bias1 = jnp.where(ws[g] < 0.5 * NEG, past_off, NEG)
            bias_ref[g] = jnp.concatenate([bias1] * HPG, axis=1)

    def local():
        sds = [_dot(ks_ref[g, pl.ds(doff, Q_BLK), :], q_plains[g]) + diag_mask for g in groups]
        sws = [_dot(kw_ref[g, pl.ds(wstart, span), :], q_plains[g]) + win_mask for g in groups]
        out = []
        for g in groups:
            m0 = jnp.max(sds[g], axis=0, keepdims=True)
            acc0 = _dot(vst_ref[g, :, pl.ds(doff, Q_BLK)], jnp.exp2(sds[g] - m0).astype(BF16))
            pw = jnp.exp2(sws[g] - jnp.max(sws[g], axis=0, keepdims=True)).astype(BF16)
            acc_w = _dot(vwt_ref[g, :, pl.ds(wstart, span)], pw)
            out.append((acc_w[0:HEAD_DIM] / acc_w[HEAD_DIM:HEAD_DIM + 1], m0, acc0))
        return out

    def query_operand(g, q4, c):
        brow = bias_ref[g, pl.ds(aligned(c * SEL_BLKS, SEL_BLKS), SEL_BLKS), :]
        b16 = jnp.concatenate([brow, bpad], axis=0).astype(BF16)
        return jnp.concatenate([q4, b16, vpad], axis=0)

    def phase(g, q4, c, slot, state, do_scores=True, do_softmax=True, do_values=True):
        m_run, alpha, acc, cmax = state
        other = 1 - slot
        if do_values:
            voff = aligned(c * SEL_CHUNK, SEL_CHUNK)
            acc = alpha * acc + _dot(vst_ref[g, :, pl.ds(voff, SEL_CHUNK)], p_refs[g][slot][...])
        if do_scores:
            koff = aligned((c + 2) * SEL_CHUNK, SEL_CHUNK)
            s = _dot(ks_ref[g, pl.ds(koff, SEL_CHUNK), :], query_operand(g, q4, c + 2))
            s_refs[g][slot][...] = s.astype(BF16)
            new_max = jnp.max(s, axis=0, keepdims=True)
        if do_softmax:
            m_new = jnp.maximum(m_run, cmax[other])
            p_refs[g][other][...] = jnp.exp2(s_refs[g][other][...] - m_new.astype(BF16))
            m_run, alpha = m_new, jnp.exp2(m_run - m_new)
        if do_scores:
            cmax = (new_max, cmax[1]) if slot == 0 else (cmax[0], new_max)
        return m_run, alpha, acc, cmax

    select(imps)
    heads = [(q4s[g], o_cmps[g]) + loc for g, loc in enumerate(local())]

    def pair(k, carries):
        out = []
        for g in range(ATT_GROUPS):
            q4 = heads[g][0]
            state = phase(g, q4, 2 * k, 0, carries[g])
            out.append(phase(g, q4, 2 * k + 1, 1, state))
        return tuple(out)

    n_pairs = jnp.maximum((i * Q_BLK + 2 * SEL_CHUNK - 1) // (2 * SEL_CHUNK), 1)
    init = []
    for g in range(ATT_GROUPS):
        q4, _, _, m0, acc0 = heads[g]
        neg_row = jnp.full((1, nq), NEG, F32)
        state = (m0, jnp.ones((1, nq), F32), acc0, (neg_row, neg_row))
        state = phase(g, q4, -2, 0, state, do_softmax=False, do_values=False)
        init.append(phase(g, q4, -1, 1, state, do_values=False))
    carries = lax.fori_loop(0, n_pairs - 1, pair, tuple(init))
    last = 2 * (n_pairs - 1)
    for g in range(ATT_GROUPS):
        q4, o_cmp, o_win, _, _ = heads[g]
        state = phase(g, q4, last, 0, carries[g], do_scores=False)
        _, _, acc_s, _ = phase(g, q4, last + 1, 1, state, do_scores=False, do_softmax=False)
        o_sel = acc_s[0:HEAD_DIM] / acc_s[HEAD_DIM:HEAD_DIM + 1]
        gates = gate_ref[g]
        outs = []
        for h in range(HPG):
            cols = slice(h * Q_BLK, (h + 1) * Q_BLK)
            outs.append(o_cmp[:, cols] * gates[3 * h:3 * h + 1]
                        + o_sel[:, cols] * gates[3 * h + 1:3 * h + 2]
                        + o_win[:, cols] * gates[3 * h + 2:3 * h + 3])
        o_ref[g * rows:(g + 1) * rows, :] = jnp.concatenate(outs, axis=0).astype(BF16)


def _attention(q_t, gate_t, kc, vct, ks, vst, kw, vwt):
    nqd, s = q_t.shape
    nc = kc.shape[1]
    nb = s // L_SEL
    nq = HPG * Q_BLK
    rows = ATT_GROUPS * HPG * HEAD_DIM
    assert (s // SEL_CHUNK) % 2 == 0 and s >= WINDOW + Q_BLK and N_KV % ATT_GROUPS == 0 and nb >= N_SELECT
    qq = jnp.arange(Q_BLK)[None, :]
    rc = jnp.arange(2 * nc)[:, None]
    cmp_mask = jnp.where(D_CMP * (rc - nc) + L_CMP - 1 <= qq, 0.0, NEG).astype(F32)
    rw = jnp.arange(2 * WINDOW + Q_BLK)[:, None]
    win_mask = jnp.where((qq < rw) & (rw <= qq + WINDOW), 0.0, NEG).astype(F32)
    return pl.pallas_call(
        _attn_kernel,
        grid=(N_KV // ATT_GROUPS, s // Q_BLK),
        in_specs=[pl.BlockSpec((rows, Q_BLK), lambda g, i: (g, i)),
                  pl.BlockSpec((ATT_GROUPS, 16, Q_BLK), lambda g, i: (g, 0, i)),
                  _resident(cmp_mask.shape, lambda g, i: (0, 0)),
                  _resident(win_mask.shape, lambda g, i: (0, 0)),
                  _resident((ATT_GROUPS, nc, LANES), lambda g, i: (g, 0, 0)),
                  _resident((ATT_GROUPS, LANES, nc), lambda g, i: (g, 0, 0)),
                  _resident((ATT_GROUPS, s, LANES), lambda g, i: (g, 0, 0)),
                  _resident((ATT_GROUPS, V_ROWS, s), lambda g, i: (g, 0, 0)),
                  _resident((ATT_GROUPS, s, LANES), lambda g, i: (g, 0, 0)),
                  _resident((ATT_GROUPS, V_ROWS, s), lambda g, i: (g, 0, 0))],
        out_specs=pl.BlockSpec((rows, Q_BLK), lambda g, i: (g, i)),
        out_shape=jax.ShapeDtypeStruct((nqd, s), BF16),
        scratch_shapes=[pltpu.VMEM((ATT_GROUPS, nb, nq), F32),
                        pltpu.VMEM((ATT_GROUPS, PS_PAD + nc, Q_BLK), F32),
                        pltpu.VMEM((ATT_GROUPS, nc, nq), F32),
                        *[pltpu.VMEM((SEL_CHUNK, nq), BF16) for _ in range(2 * ATT_GROUPS)],
                        *[pltpu.VMEM((SEL_CHUNK, nq), BF16) for _ in range(2 * ATT_GROUPS)]],
        compiler_params=_params("arbitrary", "arbitrary"),
        name="nsa_attention",
    )(q_t, gate_t, cmp_mask, win_mask, kc, vct, ks, vst, kw, vwt)


def _outproj_kernel(ot_ref, x_ref, vec_ref, w_ref, o_ref):
    vec = vec_ref[...]
    y = _dot_tn(ot_ref[...], w_ref[...])
    o_ref[...] = _post_residual(x_ref[...], y, vec[3:4], vec[4:5])


def _outproj(o_t, x, vec, w_out):
    s, d = x.shape
    nqd = o_t.shape[0]
    return pl.pallas_call(
        _outproj_kernel,
        grid=(s // ROW_TILE,),
        in_specs=[pl.BlockSpec((nqd, ROW_TILE), lambda i: (0, i)),
                  pl.BlockSpec((ROW_TILE, d), lambda i: (i, 0)),
                  _resident((8, d), lambda i: (0, 0)),
                  _resident((nqd, d), lambda i: (0, 0))],
        out_specs=pl.BlockSpec((ROW_TILE, d), lambda i: (i, 0)),
        out_shape=jax.ShapeDtypeStruct((s, d), F32),
        compiler_params=_params("arbitrary"),
        name="nsa_outproj",
    )(o_t, x, vec, w_out)


def _vec(pre_g, mod, post_g):
    d = pre_g.shape[0]
    shift, scale, gate = mod[:d], mod[d:2 * d], mod[2 * d:3 * d]
    z = jnp.zeros((d,), F32)
    return jnp.stack([pre_g, scale, shift, post_g, gate, z, z, z])


def kernel(x, c, ada_w, ada_b, norm_pre_g, norm_post_g, a_w_in, a_ln_g, a_ln_b, a_ws, a_bs, a_w_out, kv_norm_g, kv_ada_w, kv_ada_b, w_kv, cmp_pos, cmp_w1, cmp_b1, cmp_w2, cmp_b2, b_w_in, b_w_out, ffn_w_gate, ffn_w_up, ffn_w_down, moe_router, moe_router_b, moe_w_gate, moe_w_up, moe_w_down):
    batch, s, d = x.shape
    assert batch == 1 and s % SEL_CHUNK == 0 and s >= WINDOW + Q_BLK
    depth = ada_w.shape[0]
    n_a = depth // 2
    xs = x.reshape(s, d)

    mods = _ada(c, ada_w.reshape(depth * 2, d, 3 * d), ada_b.reshape(depth * 2, 3 * d)).reshape(depth, 2, 3 * d)
    kv_mod = _ada(c, kv_ada_w.reshape(1, d, 2 * d), kv_ada_b.reshape(1, 2 * d))[0]
    shared = None
    n_moe, _, _, f_moe = moe_w_gate.shape
    moe_wg = moe_w_gate.astype(BF16).reshape(n_moe * N_EXPERTS, d, f_moe)
    moe_wu = moe_w_up.astype(BF16).reshape(n_moe * N_EXPERTS, d, f_moe)
    moe_wd = moe_w_down.astype(BF16).reshape(n_moe * N_EXPERTS, f_moe, d)

    for layer in range(depth):
        vec = _vec(norm_pre_g[layer, 0], mods[layer, 0], norm_post_g[layer, 0])
        if layer < n_a:
            xs = _gmlp_layer(xs, vec, a_w_in[layer].astype(BF16),
                             jnp.stack([a_ln_g[layer], a_ln_b[layer]]),
                             a_ws[layer], a_bs[layer].T, a_w_out[layer].astype(BF16))
        else:
            if shared is None:
                kv_vec = _vec(kv_norm_g, jnp.concatenate([kv_mod, jnp.zeros((d,), F32)]), jnp.zeros((d,), F32))
                gd = N_KV * HEAD_DIM
                w_vt = jnp.concatenate([w_kv[:, 3 * gd:4 * gd], w_kv[:, 5 * gd:6 * gd]], axis=1).T
                raw, ks, kw, vst, vwt = _kvproj(xs, kv_vec, w_kv.astype(BF16), w_vt.astype(BF16))
                nc = s // D_CMP
                width = D_CMP * HEAD_DIM
                hid = cmp_w1.shape[-1]
                w2p = jnp.pad(cmp_w2, ((0, 0), (0, 0), (0, LANES - HEAD_DIM))).astype(BF16)
                b2p = jnp.pad(cmp_b2, ((0, 0), (0, LANES - HEAD_DIM))).reshape(2, 1, LANES)
                cmp_n, cmp_t = _compress(raw.reshape(2, N_KV, nc, width),
                                         cmp_pos.reshape(2, 2, 1, width),
                                         cmp_w1.reshape(2, 2, width, hid).astype(BF16),
                                         cmp_b1.reshape(2, 1, hid), w2p, b2p)
                shared = (cmp_n[0], cmp_t[1], ks, vst, kw, vwt)
            i = layer - n_a
            nq = N_HEADS * HEAD_DIM
            w_t = b_w_in[i].T.astype(BF16)
            q_t, gate_t = _qproj(xs, vec, w_t)
            kc, vct, ks, vst, kw, vwt = shared
            o_t = _attention(q_t, gate_t, kc, vct, ks, vst, kw, vwt)
            xs = _outproj(o_t, xs, vec, b_w_out[i].astype(BF16))

        vec = _vec(norm_pre_g[layer, 1], mods[layer, 1], norm_post_g[layer, 1])
        j = layer // 2
        if layer % 2 == 0:
            xs = _swiglu_layer(xs, vec, ffn_w_gate[j].astype(BF16), ffn_w_up[j].astype(BF16),
                               ffn_w_down[j].astype(BF16))
        else:
            xs = _moe_layer(xs, vec, moe_router[j].T, moe_router_b[j].reshape(N_EXPERTS, 1),
                            moe_wg, moe_wu, moe_wd, j * N_EXPERTS)
    return xs.reshape(batch, s, d)
```

```python
import jax
import jax.numpy as jnp
from jax import lax
from jax.experimental import pallas as pl
from jax.experimental.pallas import tpu as pltpu

F32 = jnp.float32
BF16 = jnp.bfloat16
I32 = jnp.int32

EPS = 1e-6
NEG = -1e30

LANES = 128
SUBLANES = 8
VMEM_LIMIT_BYTES = 56 * 1024 * 1024

CHUNK = 128
A_GROUPS = 8
N_HEADS = 16
N_KV = 4
HPG = N_HEADS // N_KV
HEAD_DIM = 64
L_CMP = 32
D_CMP = 16
L_SEL = 64
SEL_SHIFT = 6
N_SELECT = 16
N_FORCED = 3
WINDOW = 512
Q_BLK = 128
N_EXPERTS = 8
MOE_BLOCK = 256

ROW_TILE = 512
TOK_TILE = 256
DISP_TILE = 512
SEL_CHUNK = 512
SEL_BLKS = SEL_CHUNK // L_SEL
CMP_CHUNK = 128
ATT_GROUPS = 2
GATE_ROWS = 16
PS_PAD = 8
V_ROWS = 72
F_CHUNK = 512
WIN_PART_ROWS = 128
WIN_PARTS = TOK_TILE // WIN_PART_ROWS + 1
WIN_ROWS = WIN_PARTS * WIN_PART_ROWS
LOG2E = 1.4426950408889634


def _params(*sem):
    return pltpu.CompilerParams(dimension_semantics=sem, vmem_limit_bytes=VMEM_LIMIT_BYTES)


def _resident(shape, index_map):
    return pl.BlockSpec(shape, index_map, pipeline_mode=pl.Buffered(1))


def _split_bf16(a):
    hi = a.astype(BF16)
    lo = (a - hi.astype(F32)).astype(BF16)
    return hi, lo


def _dot(a, b):
    return jnp.dot(a, b, preferred_element_type=F32)


def _dot_nt(a, b):
    return lax.dot_general(a, b, (((1,), (1,)), ((), ())), preferred_element_type=F32)


def _dot_tn(a, b):
    return lax.dot_general(a, b, (((0,), (0,)), ((), ())), preferred_element_type=F32)


def _prenorm(x, g, scale, shift):
    ms = jnp.mean(x * x, axis=-1, keepdims=True)
    return (x * lax.rsqrt(ms + EPS) * g) * (1.0 + scale) + shift


def _post_residual(x, y, g, gate):
    ms = jnp.mean(y * y, axis=-1, keepdims=True)
    return x + gate * (y * lax.rsqrt(ms + EPS) * g)


def _ada_kernel(c_ref, w_ref, b_ref, o_ref):
    c = c_ref[...]
    c_act = jnp.broadcast_to(c * jax.nn.sigmoid(c), (SUBLANES, c.shape[1]))
    c_hi, c_lo = _split_bf16(c_act)
    w_hi, w_lo = _split_bf16(w_ref[...])
    m = _dot(c_hi, w_hi) + (_dot(c_hi, w_lo) + _dot(c_lo, w_hi))
    o_ref[...] = m[0:1] + b_ref[...]


def _ada(c, w, b):
    n, d, nn = w.shape
    out = pl.pallas_call(
        _ada_kernel,
        grid=(n, nn // d),
        in_specs=[pl.BlockSpec((1, d), lambda i, j: (0, 0)),
                  pl.BlockSpec((None, d, d), lambda i, j: (i, 0, j)),
                  pl.BlockSpec((None, 1, d), lambda i, j: (i, 0, j))],
        out_specs=pl.BlockSpec((None, 1, d), lambda i, j: (i, 0, j)),
        out_shape=jax.ShapeDtypeStruct((n, 1, nn), F32),
        compiler_params=_params("arbitrary", "arbitrary"),
        name="ada",
    )(c, w, b.reshape(n, 1, nn))
    return out.reshape(n, nn)


def _gmlp_kernel(x_ref, vec_ref, win_ref, ln_ref, ws_ref, bst_ref, wout_ref, o_ref, gated_ref):
    x = x_ref[...]
    vec = vec_ref[...]
    h = _prenorm(x, vec[0:1], vec[1:2], vec[2:3]).astype(BF16)
    z = jax.nn.gelu(_dot(h, win_ref[...]))
    width = z.shape[1] // 2
    u = z[:, :width]
    v = z[:, width:]
    mu = jnp.mean(v, axis=-1, keepdims=True)
    vc = v - mu
    var = jnp.mean(vc * vc, axis=-1, keepdims=True)
    ln = ln_ref[...]
    vn = (vc * lax.rsqrt(var + EPS) * ln[0:1] + ln[1:2]).astype(BF16)
    causal = (lax.broadcasted_iota(I32, (CHUNK, CHUNK), 0)
              >= lax.broadcasted_iota(I32, (CHUNK, CHUNK), 1))
    gw = width // A_GROUPS
    bst = bst_ref[...]
    for g in range(A_GROUPS):
        wg = jnp.where(causal, ws_ref[g], 0.0).astype(BF16)
        for ck in range(x.shape[0] // CHUNK):
            rows = slice(ck * CHUNK, (ck + 1) * CHUNK)
            cols = slice(g * gw, (g + 1) * gw)
            mixed = _dot(wg, vn[rows, cols]) + bst[:, g:g + 1]
            gated_ref[rows, cols] = (u[rows, cols] * mixed).astype(BF16)
    y = _dot(gated_ref[...], wout_ref[...])
    o_ref[...] = _post_residual(x, y, vec[3:4], vec[4:5])


def _gmlp_layer(x, vec, w_in, ln, ws, bst, w_out):
    s, d = x.shape
    e2 = w_in.shape[1]
    return pl.pallas_call(
        _gmlp_kernel,
        grid=(s // ROW_TILE,),
        in_specs=[pl.BlockSpec((ROW_TILE, d), lambda i: (i, 0)),
                  _resident((SUBLANES, d), lambda i: (0, 0)),
                  _resident((d, e2), lambda i: (0, 0)),
                  _resident((2, e2 // 2), lambda i: (0, 0)),
                  _resident(ws.shape, lambda i: (0, 0, 0)),
                  _resident(bst.shape, lambda i: (0, 0)),
                  _resident((e2 // 2, d), lambda i: (0, 0))],
        out_specs=pl.BlockSpec((ROW_TILE, d), lambda i: (i, 0)),
        out_shape=jax.ShapeDtypeStruct((s, d), F32),
        scratch_shapes=[pltpu.VMEM((ROW_TILE, e2 // 2), BF16)],
        compiler_params=_params("arbitrary"),
        name="gmlp",
    )(x, vec, w_in, ln, ws, bst, w_out)


def _swiglu_kernel(x_ref, vec_ref, wg_ref, wu_ref, wd_ref, o_ref):
    x = x_ref[...]
    vec = vec_ref[...]
    h = _prenorm(x, vec[0:1], vec[1:2], vec[2:3]).astype(BF16)
    g = _dot(h, wg_ref[...])
    a = (g * jax.nn.sigmoid(g) * _dot(h, wu_ref[...])).astype(BF16)
    y = _dot(a, wd_ref[...])
    o_ref[...] = _post_residual(x, y, vec[3:4], vec[4:5])


def _swiglu_layer(x, vec, w_gate, w_up, w_down):
    s, d = x.shape
    f = w_gate.shape[1]
    return pl.pallas_call(
        _swiglu_kernel,
        grid=(s // ROW_TILE,),
        in_specs=[pl.BlockSpec((ROW_TILE, d), lambda i: (i, 0)),
                  _resident((SUBLANES, d), lambda i: (0, 0)),
                  _resident((d, f), lambda i: (0, 0)),
                  _resident((d, f), lambda i: (0, 0)),
                  _resident((f, d), lambda i: (0, 0))],
        out_specs=pl.BlockSpec((ROW_TILE, d), lambda i: (i, 0)),
        out_shape=jax.ShapeDtypeStruct((s, d), F32),
        compiler_params=_params("arbitrary"),
        name="swiglu",
    )(x, vec, w_gate, w_up, w_down)


def _router_kernel(x_ref, vec_ref, wrt_ref, br_ref, h_ref, idx_ref, gate_ref):
    x = x_ref[...]
    vec = vec_ref[...]
    h = _prenorm(x, vec[0:1], vec[1:2], vec[2:3])
    h_ref[...] = h.astype(BF16)
    h_hi, h_lo = _split_bf16(h)
    w_hi, w_lo = _split_bf16(wrt_ref[...])
    logit = _dot_nt(w_hi, h_hi) + (_dot_nt(w_hi, h_lo) + _dot_nt(w_lo, h_hi)) + br_ref[...]
    ne = logit.shape[0]
    eidx = lax.broadcasted_iota(I32, logit.shape, 0)
    m1 = jnp.max(logit, axis=0, keepdims=True)
    i1 = jnp.min(jnp.where(logit == m1, eidx, ne), axis=0, keepdims=True)
    rest = jnp.where(eidx == i1, -jnp.inf, logit)
    m2 = jnp.max(rest, axis=0, keepdims=True)
    i2 = jnp.min(jnp.where(rest == m2, eidx, ne), axis=0, keepdims=True)
    e21 = jnp.exp(m2 - m1)
    g1 = 1.0 / (1.0 + e21)
    g2 = e21 * g1
    pad_i = jnp.zeros((SUBLANES - 2, i1.shape[1]), I32)
    pad_f = jnp.zeros((SUBLANES - 2, i1.shape[1]), F32)
    idx_ref[...] = jnp.concatenate([i1, i2, pad_i], axis=0)
    gate_ref[...] = jnp.concatenate([g1, g2, pad_f], axis=0)


def _router(x, vec, w_rt, b_r):
    s, d = x.shape
    ne = w_rt.shape[0]
    return pl.pallas_call(
        _router_kernel,
        grid=(s // ROW_TILE,),
        in_specs=[pl.BlockSpec((ROW_TILE, d), lambda i: (i, 0)),
                  _resident((SUBLANES, d), lambda i: (0, 0)),
                  _resident((ne, d), lambda i: (0, 0)),
                  _resident((ne, 1), lambda i: (0, 0))],
        out_specs=[pl.BlockSpec((ROW_TILE, d), lambda i: (i, 0)),
                   pl.BlockSpec((SUBLANES, ROW_TILE), lambda i: (0, i)),
                   pl.BlockSpec((SUBLANES, ROW_TILE), lambda i: (0, i))],
        out_shape=[jax.ShapeDtypeStruct((s, d), BF16),
                   jax.ShapeDtypeStruct((SUBLANES, s), I32),
                   jax.ShapeDtypeStruct((SUBLANES, s), F32)],
        compiler_params=_params("arbitrary"),
        name="moe_router",
    )(x, vec, w_rt, b_r)


def _rank_kernel(idx_ref, rank_ref, start_ref, count_ref, carry_ref):
    @pl.when(pl.program_id(0) == 0)
    def _():
        carry_ref[...] = jnp.zeros_like(carry_ref)

    idx = idx_ref[...]
    tt = idx.shape[1]
    eidx = lax.broadcasted_iota(I32, (N_EXPERTS, tt), 0)
    hit1 = eidx == idx[0:1]
    hit2 = eidx == idx[1:2]
    member = jnp.where(hit1 | hit2, 1.0, 0.0)
    before = (lax.broadcasted_iota(I32, (tt, tt), 0) < lax.broadcasted_iota(I32, (tt, tt), 1))
    carry = carry_ref[...]
    cum = _dot(member.astype(BF16), jnp.where(before, 1.0, 0.0).astype(BF16)) + carry[:, 0:1]
    r1 = jnp.sum(jnp.where(hit1, cum, 0.0), axis=0, keepdims=True)
    r2 = jnp.sum(jnp.where(hit2, cum, 0.0), axis=0, keepdims=True)
    rank_ref[...] = jnp.concatenate([r1, r2, jnp.zeros((SUBLANES - 2, tt), F32)], axis=0).astype(I32)
    tile_count = jnp.broadcast_to(jnp.sum(member, axis=1, keepdims=True), carry.shape)
    start_ref[...] = carry
    count_ref[...] = tile_count
    carry_ref[...] = carry + tile_count


def _ranks(idx):
    s = idx.shape[1]
    nt = s // TOK_TILE
    return pl.pallas_call(
        _rank_kernel,
        grid=(nt,),
        in_specs=[pl.BlockSpec((SUBLANES, TOK_TILE), lambda i: (0, i))],
        out_specs=[pl.BlockSpec((SUBLANES, TOK_TILE), lambda i: (0, i)),
                   pl.BlockSpec((None, N_EXPERTS, LANES), lambda i: (i, 0, 0)),
                   pl.BlockSpec((None, N_EXPERTS, LANES), lambda i: (i, 0, 0))],
        out_shape=[jax.ShapeDtypeStruct((SUBLANES, s), I32),
                   jax.ShapeDtypeStruct((nt, N_EXPERTS, LANES), F32),
                   jax.ShapeDtypeStruct((nt, N_EXPERTS, LANES), F32)],
        scratch_shapes=[pltpu.VMEM((N_EXPERTS, LANES), F32)],
        compiler_params=_params("arbitrary"),
        name="moe_ranks",
    )(idx)


def _match(idx, rank, expert, rows):
    r1 = jnp.where(idx[0:1] == expert, rank[0:1], -1)
    r2 = jnp.where(idx[1:2] == expert, rank[1:2], -1)
    return rows == r1, rows == r2


def _dispatch_kernel(be_ref, lb_ref, tlo_ref, thi_ref, idx_ref, rank_ref, gate_ref, h_ref, o_ref, rg_ref,
                     acc_ref, gacc_ref):
    b = pl.program_id(0)
    expert = be_ref[b]
    rows = lax.broadcasted_iota(I32, (MOE_BLOCK, DISP_TILE), 0) + lb_ref[b]
    acc_ref[...] = jnp.zeros_like(acc_ref)
    gacc_ref[...] = jnp.zeros_like(gacc_ref)

    def body(t, carry):
        off = pl.multiple_of(t * DISP_TILE, DISP_TILE)
        m1, m2 = _match(idx_ref[:, pl.ds(off, DISP_TILE)], rank_ref[:, pl.ds(off, DISP_TILE)], expert, rows)
        gates = gate_ref[:, pl.ds(off, DISP_TILE)]
        gacc_ref[...] += jnp.sum(jnp.where(m1, gates[0:1], 0.0) + jnp.where(m2, gates[1:2], 0.0),
                                 axis=1, keepdims=True)
        onehot = jnp.where(m1 | m2, 1.0, 0.0).astype(BF16)
        acc_ref[...] += _dot(onehot, h_ref[pl.ds(off, DISP_TILE), :])
        return carry

    lax.fori_loop(tlo_ref[b], thi_ref[b], body, 0)
    o_ref[...] = acc_ref[...].astype(BF16)
    rg_ref[...] = gacc_ref[...]


def _dispatch(blk_e, blk_lb, blk_tlo, blk_thi, idx, rank, gates, h):
    s, d = h.shape
    n_blk = blk_e.shape[0]
    grid_spec = pltpu.PrefetchScalarGridSpec(
        num_scalar_prefetch=4,
        grid=(n_blk,),
        in_specs=[_resident((SUBLANES, s), lambda b, *_: (0, 0)),
                  _resident((SUBLANES, s), lambda b, *_: (0, 0)),
                  _resident((SUBLANES, s), lambda b, *_: (0, 0)),
                  _resident((s, d), lambda b, *_: (0, 0))],
        out_specs=[pl.BlockSpec((MOE_BLOCK, d), lambda b, *_: (b, 0)),
                   pl.BlockSpec((MOE_BLOCK, 1), lambda b, *_: (b, 0))],
        scratch_shapes=[pltpu.VMEM((MOE_BLOCK, d), F32), pltpu.VMEM((MOE_BLOCK, 1), F32)],
    )
    return pl.pallas_call(
        _dispatch_kernel,
        grid_spec=grid_spec,
        out_shape=[jax.ShapeDtypeStruct((n_blk * MOE_BLOCK, d), BF16),
                   jax.ShapeDtypeStruct((n_blk * MOE_BLOCK, 1), F32)],
        compiler_params=_params("arbitrary"),
        name="moe_dispatch",
    )(blk_e, blk_lb, blk_tlo, blk_thi, idx, rank, gates, h)


def _expert_kernel(be_ref, used_ref, x_ref, rg_ref, wg_ref, wu_ref, wd_ref, o_ref):
    b = pl.program_id(0)

    @pl.when(b < used_ref[0])
    def _():
        x = x_ref[...]
        f = wg_ref.shape[1]
        y = jnp.zeros(o_ref.shape, F32)
        for c in range(f // F_CHUNK):
            cols = slice(c * F_CHUNK, (c + 1) * F_CHUNK)
            g = _dot(x, wg_ref[:, cols])
            a = (g * jax.nn.sigmoid(g) * _dot(x, wu_ref[:, cols])).astype(BF16)
            y = y + _dot(a, wd_ref[cols, :])
        o_ref[...] = (y * rg_ref[...]).astype(BF16)

    @pl.when(b >= used_ref[0])
    def _():
        o_ref[...] = jnp.zeros_like(o_ref)


def _experts(blk_e, n_used, x_buf, row_gate, w_gate, w_up, w_down):
    n_rows, d = x_buf.shape
    f = w_gate.shape[2]
    assert f % F_CHUNK == 0
    grid_spec = pltpu.PrefetchScalarGridSpec(
        num_scalar_prefetch=2,
        grid=(n_rows // MOE_BLOCK,),
        in_specs=[pl.BlockSpec((MOE_BLOCK, d), lambda b, be, nu: (b, 0)),
                  pl.BlockSpec((MOE_BLOCK, 1), lambda b, be, nu: (b, 0)),
                  pl.BlockSpec((None, d, f), lambda b, be, nu: (be[b], 0, 0)),
                  pl.BlockSpec((None, d, f), lambda b, be, nu: (be[b], 0, 0)),
                  pl.BlockSpec((None, f, d), lambda b, be, nu: (be[b], 0, 0))],
        out_specs=pl.BlockSpec((MOE_BLOCK, d), lambda b, be, nu: (b, 0)),
    )
    return pl.pallas_call(
        _expert_kernel,
        grid_spec=grid_spec,
        out_shape=jax.ShapeDtypeStruct((n_rows, d), BF16),
        compiler_params=_params("arbitrary"),
        name="moe_experts",
    )(blk_e, n_used, x_buf, row_gate, w_gate, w_up, w_down)


def _combine_kernel(win_ref, lb_ref, idx_ref, rank_ref, *refs):
    y_refs = refs[:N_EXPERTS * WIN_PARTS]
    x_ref, vec_ref, o_ref = refs[N_EXPERTS * WIN_PARTS:]
    t = pl.program_id(0)
    idx = idx_ref[...]
    rank = rank_ref[...]
    acc = jnp.zeros(o_ref.shape, F32)
    for e in range(N_EXPERTS):
        rows = lax.broadcasted_iota(I32, (WIN_ROWS, TOK_TILE), 0) + lb_ref[t * N_EXPERTS + e]
        m1, m2 = _match(idx, rank, e, rows)
        onehot = jnp.where(m1 | m2, 1.0, 0.0).astype(BF16)
        y = jnp.concatenate([y_refs[e * WIN_PARTS + k][...] for k in range(WIN_PARTS)], axis=0)
        acc = acc + _dot_tn(onehot, y)
    vec = vec_ref[...]
    o_ref[...] = _post_residual(x_ref[...], acc, vec[3:4], vec[4:5])


def _combine(win_start, win_lb, idx, rank, y_buf, x, vec):
    s, d = x.shape

    def y_spec(e, k):
        return pl.BlockSpec((WIN_PART_ROWS, d), lambda t, ws, lb: (ws[t * N_EXPERTS + e] + k, 0))

    grid_spec = pltpu.PrefetchScalarGridSpec(
        num_scalar_prefetch=2,
        grid=(s // TOK_TILE,),
        in_specs=[pl.BlockSpec((SUBLANES, TOK_TILE), lambda t, *_: (0, t)),
                  pl.BlockSpec((SUBLANES, TOK_TILE), lambda t, *_: (0, t))]
                 + [y_spec(e, k) for e in range(N_EXPERTS) for k in range(WIN_PARTS)]
                 + [pl.BlockSpec((TOK_TILE, d), lambda t, *_: (t, 0)),
                    _resident((SUBLANES, d), lambda t, *_: (0, 0))],
        out_specs=pl.BlockSpec((TOK_TILE, d), lambda t, *_: (t, 0)),
    )
    return pl.pallas_call(
        _combine_kernel,
        grid_spec=grid_spec,
        out_shape=jax.ShapeDtypeStruct((s, d), F32),
        compiler_params=_params("arbitrary"),
        name="moe_combine",
    )(win_start, win_lb, idx, rank, *([y_buf] * (N_EXPERTS * WIN_PARTS)), x, vec)


def _moe_layer(x, vec, w_rt, b_r, w_gate, w_up, w_down, w_first):
    s, d = x.shape
    nt = s // TOK_TILE
    h, idx, gates = _router(x, vec, w_rt, b_r)
    rank, tile_start, tile_count = _ranks(idx)

    tile_start = tile_start[:, :, 0].astype(I32)
    tile_count = tile_count[:, :, 0].astype(I32)
    tile_end = tile_start + tile_count
    counts = tile_end[-1]
    padded = (counts + MOE_BLOCK - 1) // MOE_BLOCK * MOE_BLOCK
    pend = jnp.cumsum(padded)
    pstart = pend - padded
    n_rows = -(-(2 * s) // MOE_BLOCK) * MOE_BLOCK + N_EXPERTS * MOE_BLOCK
    n_blk = n_rows // MOE_BLOCK
    blk_row = jnp.arange(n_blk, dtype=I32) * MOE_BLOCK
    blk_e = jnp.minimum(jnp.sum(blk_row[:, None] >= pend[None, :], axis=1), N_EXPERTS - 1).astype(I32)
    blk_lb = blk_row - pstart[blk_e]
    per = DISP_TILE // TOK_TILE
    te = tile_end[per - 1::per][:, blk_e]
    ts = tile_start[::per][:, blk_e]
    blk_tlo = jnp.sum(te <= blk_lb[None, :], axis=0).astype(I32)
    blk_thi = jnp.sum(ts < (blk_lb + MOE_BLOCK)[None, :], axis=0).astype(I32)
    n_used = (pend[-1] // MOE_BLOCK).astype(I32).reshape(1)

    x_buf, row_gate = _dispatch(blk_e, blk_lb.astype(I32), blk_tlo, blk_thi, idx, rank, gates, h)
    y_buf = _experts(blk_e + w_first, n_used, x_buf, row_gate, w_gate, w_up, w_down)

    win_start = jnp.minimum((pstart[None, :] + tile_start) // WIN_PART_ROWS, n_rows // WIN_PART_ROWS - WIN_PARTS)
    win_lb = win_start * WIN_PART_ROWS - pstart[None, :]
    return _combine(win_start.reshape(-1).astype(I32), win_lb.reshape(-1).astype(I32), idx, rank, y_buf, x, vec)


def _kvproj_kernel(x_ref, vec_ref, wkv_ref, wvt_ref, raw_ref, ks_ref, kw_ref, vst_ref, vwt_ref):
    x = x_ref[...]
    vec = vec_ref[...]
    h = _prenorm(x, vec[0:1], vec[1:2], vec[2:3]).astype(BF16)
    kv = _dot(h, wkv_ref[...])
    vt = _dot_nt(wvt_ref[...], h)
    tm = x.shape[0]
    gd = N_KV * HEAD_DIM
    key_blk = jnp.right_shift(pl.program_id(0) * tm + lax.broadcasted_iota(I32, (tm, SEL_BLKS), 0), SEL_SHIFT)
    ind = jnp.where((key_blk & (SEL_BLKS - 1)) == lax.broadcasted_iota(I32, (tm, SEL_BLKS), 1), 1.0, 0.0)
    pad_s = jnp.zeros((tm, LANES - HEAD_DIM - SEL_BLKS), F32)
    pad_w = jnp.zeros((tm, LANES - HEAD_DIM), F32)
    ones_row = jnp.concatenate([jnp.ones((1, tm), F32), jnp.zeros((V_ROWS - HEAD_DIM - 1, tm), F32)], axis=0)
    for g in range(N_KV):
        c = g * HEAD_DIM
        raw_ref[0, g] = kv[:, c:c + HEAD_DIM]
        raw_ref[1, g] = kv[:, gd + c:gd + c + HEAD_DIM]
        ks = kv[:, 2 * gd + c:2 * gd + c + HEAD_DIM]
        kw = kv[:, 4 * gd + c:4 * gd + c + HEAD_DIM]
        ks_ref[g] = jnp.concatenate([ks, ind, pad_s], axis=1).astype(BF16)
        kw_ref[g] = jnp.concatenate([kw, pad_w], axis=1).astype(BF16)
        vst_ref[g] = jnp.concatenate([vt[c:c + HEAD_DIM], ones_row], axis=0).astype(BF16)
        vwt_ref[g] = jnp.concatenate([vt[gd + c:gd + c + HEAD_DIM], ones_row], axis=0).astype(BF16)


def _kvproj(x, vec, w_kv, w_vt):
    s, d = x.shape
    nkv = w_kv.shape[1]
    return pl.pallas_call(
        _kvproj_kernel,
        grid=(s // ROW_TILE,),
        in_specs=[pl.BlockSpec((ROW_TILE, d), lambda i: (i, 0)),
                  _resident((SUBLANES, d), lambda i: (0, 0)),
                  _resident((d, nkv), lambda i: (0, 0)),
                  _resident(w_vt.shape, lambda i: (0, 0))],
        out_specs=[pl.BlockSpec((2, N_KV, ROW_TILE, HEAD_DIM), lambda i: (0, 0, i, 0)),
                   pl.BlockSpec((N_KV, ROW_TILE, LANES), lambda i: (0, i, 0)),
                   pl.BlockSpec((N_KV, ROW_TILE, LANES), lambda i: (0, i, 0)),
                   pl.BlockSpec((N_KV, V_ROWS, ROW_TILE), lambda i: (0, 0, i)),
                   pl.BlockSpec((N_KV, V_ROWS, ROW_TILE), lambda i: (0, 0, i))],
        out_shape=[jax.ShapeDtypeStruct((2, N_KV, s, HEAD_DIM), F32),
                   jax.ShapeDtypeStruct((N_KV, s, LANES), BF16),
                   jax.ShapeDtypeStruct((N_KV, s, LANES), BF16),
                   jax.ShapeDtypeStruct((N_KV, V_ROWS, s), BF16),
                   jax.ShapeDtypeStruct((N_KV, V_ROWS, s), BF16)],
        compiler_params=_params("arbitrary"),
        name="nsa_kvproj",
    )(x, vec, w_kv, w_vt)


def _compress_kernel(raw_ref, pos_ref, w1_ref, b1_ref, w2_ref, b2_ref, n_ref, t_ref):
    raw = raw_ref[...]
    nc = raw.shape[0]
    first = _dot((raw + pos_ref[0]).astype(BF16), w1_ref[0])
    second = _dot((raw + pos_ref[1]).astype(BF16), w1_ref[1])
    hid = jax.nn.gelu(first + pltpu.roll(second, nc - 1, 0) + b1_ref[...])
    out = _dot(hid.astype(BF16), w2_ref[...]) + b2_ref[...]
    n_ref[...] = out.astype(BF16)
    row = lax.broadcasted_iota(I32, (LANES, nc), 0)
    t_ref[...] = jnp.where(row == HEAD_DIM, 1.0, out.T).astype(BF16)


def _compress(raw, pos, w1, b1, w2, b2):
    _, g, nc, width = raw.shape
    hid = w1.shape[-1]
    return pl.pallas_call(
        _compress_kernel,
        grid=(2, g),
        in_specs=[pl.BlockSpec((None, None, nc, width), lambda j, k: (j, k, 0, 0)),
                  pl.BlockSpec((None, 2, 1, width), lambda j, k: (j, 0, 0, 0)),
                  pl.BlockSpec((None, 2, width, hid), lambda j, k: (j, 0, 0, 0)),
                  pl.BlockSpec((None, 1, hid), lambda j, k: (j, 0, 0)),
                  pl.BlockSpec((None, hid, LANES), lambda j, k: (j, 0, 0)),
                  pl.BlockSpec((None, 1, LANES), lambda j, k: (j, 0, 0))],
        out_specs=[pl.BlockSpec((None, None, nc, LANES), lambda j, k: (j, k, 0, 0)),
                   pl.BlockSpec((None, None, LANES, nc), lambda j, k: (j, k, 0, 0))],
        out_shape=[jax.ShapeDtypeStruct((2, g, nc, LANES), BF16),
                   jax.ShapeDtypeStruct((2, g, LANES, nc), BF16)],
        compiler_params=_params("arbitrary", "arbitrary"),
        name="nsa_compress",
    )(raw, pos, w1, b1, w2, b2)


def _qproj_kernel(x_ref, vec_ref, wt_ref, q_ref, gate_ref):
    x = x_ref[...]
    vec = vec_ref[...]
    h = _prenorm(x, vec[0:1], vec[1:2], vec[2:3]).astype(BF16)
    pt = _dot_nt(wt_ref[...], h)
    nq = N_HEADS * HEAD_DIM
    q_ref[...] = (pt[:nq] * (HEAD_DIM ** -0.5 * LOG2E)).astype(BF16)
    gates = jax.nn.sigmoid(pt[nq:nq + 3 * N_HEADS])
    per = 3 * HPG
    pad = jnp.zeros((GATE_ROWS - per, x.shape[0]), F32)
    for g in range(N_KV):
        gate_ref[g] = jnp.concatenate([gates[g * per:(g + 1) * per], pad], axis=0)


def _qproj(x, vec, w_t):
    s, d = x.shape
    return pl.pallas_call(
        _qproj_kernel,
        grid=(s // ROW_TILE,),
        in_specs=[pl.BlockSpec((ROW_TILE, d), lambda i: (i, 0)),
                  _resident((SUBLANES, d), lambda i: (0, 0)),
                  _resident(w_t.shape, lambda i: (0, 0))],
        out_specs=[pl.BlockSpec((N_HEADS * HEAD_DIM, ROW_TILE), lambda i: (0, i)),
                   pl.BlockSpec((N_KV, GATE_ROWS, ROW_TILE), lambda i: (0, 0, i))],
        out_shape=[jax.ShapeDtypeStruct((N_HEADS * HEAD_DIM, s), BF16),
                   jax.ShapeDtypeStruct((N_KV, GATE_ROWS, s), F32)],
        compiler_params=_params("arbitrary"),
        name="nsa_qproj",
    )(x, vec, w_t)


def _attn_kernel(q_ref, gate_ref, cmask_ref, wmask_ref, kc_ref, vct_ref, ks_ref, vst_ref, kw_ref, vwt_ref,
                 o_ref, bias_ref, ps_ref, sc_ref, *sp_refs):
    s_refs = [[sp_refs[2 * g + k] for k in range(2)] for g in range(ATT_GROUPS)]
    p_refs = [[sp_refs[2 * ATT_GROUPS + 2 * g + k] for k in range(2)] for g in range(ATT_GROUPS)]
    i = pl.program_id(1)
    nq = HPG * Q_BLK
    nc = kc_ref.shape[1]
    nb = bias_ref.shape[1]
    rows = HPG * HEAD_DIM
    t1 = i * Q_BLK + lax.broadcasted_iota(I32, (1, Q_BLK), 1)
    t4 = jnp.concatenate([t1] * HPG, axis=1)
    j_io = lax.broadcasted_iota(I32, (nb, Q_BLK), 0)
    jt = jnp.right_shift(t1, SEL_SHIFT)
    forced = (j_io == 0) | (j_io == jt) | (j_io == jt - 1)
    cand_off = jnp.where((j_io * L_SEL <= t1) & jnp.logical_not(forced), 0.0, NEG)
    past_off = jnp.where(j_io * L_SEL < i * Q_BLK, 0.0, NEG)
    doff = pl.multiple_of(i * Q_BLK, Q_BLK)
    cmask_off = nc - i * (Q_BLK // D_CMP)
    span = WINDOW + Q_BLK
    wstart = pl.multiple_of(jnp.maximum(i * Q_BLK - WINDOW, 0), Q_BLK)
    win_mask = wmask_ref[pl.ds(pl.multiple_of(WINDOW - jnp.minimum(i * Q_BLK, WINDOW), Q_BLK), span), :]
    win_mask = jnp.concatenate([win_mask] * HPG, axis=1)
    diag_mask = jnp.concatenate([wmask_ref[WINDOW:WINDOW + Q_BLK, :]] * HPG, axis=1)
    vpad = jnp.zeros((LANES - HEAD_DIM - 2 * SEL_BLKS, nq), BF16)
    bpad = jnp.zeros((SEL_BLKS, nq), F32)

    def aligned(x, m):
        return x if isinstance(x, int) else pl.multiple_of(x, m)

    groups = range(ATT_GROUPS)
    q4s, q_plains = [], []
    for g in groups:
        qb = q_ref[g * rows:(g + 1) * rows, :]
        q4s.append(jnp.concatenate([qb[h * HEAD_DIM:(h + 1) * HEAD_DIM] for h in range(HPG)], axis=1))
        q_plains.append(jnp.concatenate([q4s[g], jnp.zeros((LANES - HEAD_DIM, nq), BF16)], axis=0))

    def compressed(n_rows):
        chunks = [slice(r0, r0 + CMP_CHUNK) for r0 in range(0, n_rows, CMP_CHUNK)]
        ms = [jnp.full((1, nq), NEG, F32) for _ in groups]
        for rs in chunks:
            mask = cmask_ref[pl.ds(pl.multiple_of(cmask_off + rs.start, 8), CMP_CHUNK), :]
            mask = jnp.concatenate([mask] * HPG, axis=1)
            for g in groups:
                s = _dot(kc_ref[g, rs, :], q_plains[g]) + mask
                sc_ref[g, rs, :] = s
                ms[g] = jnp.maximum(ms[g], jnp.max(s, axis=0, keepdims=True))
        ls = [jnp.zeros((1, nq), F32) for _ in groups]
        for rs in chunks:
            for g in groups:
                e = jnp.exp2(sc_ref[g, rs, :] - ms[g])
                sc_ref[g, rs, :] = e
                ls[g] = ls[g] + jnp.sum(e, axis=0, keepdims=True)
        rls = [jnp.where(t4 >= L_CMP - 1, 1.0 / ls[g], 0.0) for g in groups]
        os_ = [jnp.zeros((HEAD_DIM, nq), F32) for _ in groups]
        for g in groups:
            ps_ref[g, 0:PS_PAD, :] = jnp.zeros((PS_PAD, Q_BLK), F32)
            if n_rows < nc:
                ps_ref[g, PS_PAD + n_rows:PS_PAD + nc, :] = jnp.zeros((nc - n_rows, Q_BLK), F32)
        for rs in chunks:
            for g in groups:
                pc = sc_ref[g, rs, :] * rls[g]
                os_[g] = os_[g] + _dot(vct_ref[g, 0:HEAD_DIM, rs], pc.astype(BF16))
                psum = pc[:, 0:Q_BLK]
                for h in range(1, HPG):
                    psum = psum + pc[:, h * Q_BLK:(h + 1) * Q_BLK]
                ps_ref[g, PS_PAD + rs.start:PS_PAD + rs.stop, :] = psum
        ratio = L_SEL // D_CMP
        imps = []
        for g in groups:
            imp = jnp.zeros((nb, Q_BLK), F32)
            for k in range(1 - L_CMP // D_CMP, ratio):
                imp = imp + ps_ref[g, pl.ds(PS_PAD + k, nb, stride=ratio), :]
            imps.append(imp)
        return tuple(os_) + tuple(imps)

    if nc % (2 * CMP_CHUNK) == 0:
        last_complete = ((i + 1) * Q_BLK - L_CMP) // D_CMP
        cmp_out = lax.cond(last_complete < nc // 2, lambda: compressed(nc // 2), lambda: compressed(nc))
    else:
        cmp_out = compressed(nc)
    o_cmps, imps = cmp_out[:ATT_GROUPS], cmp_out[ATT_GROUPS:]

    def select(imps):
        w = jnp.concatenate([imp + cand_off for imp in imps], axis=1)
        j_all = lax.broadcasted_iota(I32, w.shape, 0).astype(F32)
        for _ in range(N_SELECT - N_FORCED):
            m = jnp.max(w, axis=0, keepdims=True)
            first = jnp.min(jnp.where(w == m, j_all, float(nb)), axis=0, keepdims=True)
            first = jnp.where(m > 0.5 * NEG, first, -1.0)
            w = jnp.where(j_all == first, NEG, w)
        for g in range(ATT_GROUPS):
            bias1 = jnp.where(w[:, g * Q_BLK:(g + 1) * Q_BLK] < 0.5 * NEG, past_off, NEG)
            bias_ref[g] = jnp.concatenate([bias1] * HPG, axis=1)

    def local():
        sds = [_dot(ks_ref[g, pl.ds(doff, Q_BLK), :], q_plains[g]) + diag_mask for g in groups]
        sws = [_dot(kw_ref[g, pl.ds(wstart, span), :], q_plains[g]) + win_mask for g in groups]
        out = []
        for g in groups:
            m0 = jnp.max(sds[g], axis=0, keepdims=True)
            acc0 = _dot(vst_ref[g, :, pl.ds(doff, Q_BLK)], jnp.exp2(sds[g] - m0).astype(BF16))
            pw = jnp.exp2(sws[g] - jnp.max(sws[g], axis=0, keepdims=True)).astype(BF16)
            acc_w = _dot(vwt_ref[g, :, pl.ds(wstart, span)], pw)
            out.append((acc_w[0:HEAD_DIM] / acc_w[HEAD_DIM:HEAD_DIM + 1], m0, acc0))
        return out

    def query_operand(g, q4, c):
        brow = bias_ref[g, pl.ds(aligned(c * SEL_BLKS, SEL_BLKS), SEL_BLKS), :]
        b16 = jnp.concatenate([brow, bpad], axis=0).astype(BF16)
        return jnp.concatenate([q4, b16, vpad], axis=0)

    def phase(g, q4, c, slot, state, do_scores=True, do_softmax=True, do_values=True):
        m_run, alpha, acc, cmax = state
        other = 1 - slot
        if do_values:
            voff = aligned(c * SEL_CHUNK, SEL_CHUNK)
            acc = alpha * acc + _dot(vst_ref[g, :, pl.ds(voff, SEL_CHUNK)], p_refs[g][slot][...])
        if do_scores:
            koff = aligned((c + 2) * SEL_CHUNK, SEL_CHUNK)
            s = _dot(ks_ref[g, pl.ds(koff, SEL_CHUNK), :], query_operand(g, q4, c + 2))
            s_refs[g][slot][...] = s.astype(BF16)
            new_max = jnp.max(s, axis=0, keepdims=True)
        if do_softmax:
            m_new = jnp.maximum(m_run, cmax[other])
            p_refs[g][other][...] = jnp.exp2(s_refs[g][other][...] - m_new.astype(BF16))
            m_run, alpha = m_new, jnp.exp2(m_run - m_new)
        if do_scores:
            cmax = (new_max, cmax[1]) if slot == 0 else (cmax[0], new_max)
        return m_run, alpha, acc, cmax

    select(imps)
    heads = [(q4s[g], o_cmps[g]) + loc for g, loc in enumerate(local())]

    def pair(k, carries):
        out = []
        for g in range(ATT_GROUPS):
            q4 = heads[g][0]
            state = phase(g, q4, 2 * k, 0, carries[g])
            out.append(phase(g, q4, 2 * k + 1, 1, state))
        return tuple(out)

    n_pairs = jnp.maximum((i * Q_BLK + 2 * SEL_CHUNK - 1) // (2 * SEL_CHUNK), 1)
    init = []
    for g in range(ATT_GROUPS):
        q4, _, _, m0, acc0 = heads[g]
        neg_row = jnp.full((1, nq), NEG, F32)
        state = (m0, jnp.ones((1, nq), F32), acc0, (neg_row, neg_row))
        state = phase(g, q4, -2, 0, state, do_softmax=False, do_values=False)
        init.append(phase(g, q4, -1, 1, state, do_values=False))
    carries = lax.fori_loop(0, n_pairs - 1, pair, tuple(init))
    last = 2 * (n_pairs - 1)
    for g in range(ATT_GROUPS):
        q4, o_cmp, o_win, _, _ = heads[g]
        state = phase(g, q4, last, 0, carries[g], do_scores=False)
        _, _, acc_s, _ = phase(g, q4, last + 1, 1, state, do_scores=False, do_softmax=False)
        o_sel = acc_s[0:HEAD_DIM] / acc_s[HEAD_DIM:HEAD_DIM + 1]
        gates = gate_ref[g]
        outs = []
        for h in range(HPG):
            cols = slice(h * Q_BLK, (h + 1) * Q_BLK)
            outs.append(o_cmp[:, cols] * gates[3 * h:3 * h + 1]
                        + o_sel[:, cols] * gates[3 * h + 1:3 * h + 2]
                        + o_win[:, cols] * gates[3 * h + 2:3 * h + 3])
        o_ref[g * rows:(g + 1) * rows, :] = jnp.concatenate(outs, axis=0).astype(BF16)


def _attention(q_t, gate_t, kc, vct, ks, vst, kw, vwt):
    nqd, s = q_t.shape
    nc = kc.shape[1]
    nb = s // L_SEL
    nq = HPG * Q_BLK
    rows = ATT_GROUPS * HPG * HEAD_DIM
    assert (s // SEL_CHUNK) % 2 == 0 and s >= WINDOW + Q_BLK and N_KV % ATT_GROUPS == 0 and nb >= N_SELECT
    qq = jnp.arange(Q_BLK)[None, :]
    rc = jnp.arange(2 * nc)[:, None]
    cmp_mask = jnp.where(D_CMP * (rc - nc) + L_CMP - 1 <= qq, 0.0, NEG).astype(F32)
    rw = jnp.arange(2 * WINDOW + Q_BLK)[:, None]
    win_mask = jnp.where((qq < rw) & (rw <= qq + WINDOW), 0.0, NEG).astype(F32)
    return pl.pallas_call(
        _attn_kernel,
        grid=(N_KV // ATT_GROUPS, s // Q_BLK),
        in_specs=[pl.BlockSpec((rows, Q_BLK), lambda g, i: (g, i)),
                  pl.BlockSpec((ATT_GROUPS, GATE_ROWS, Q_BLK), lambda g, i: (g, 0, i)),
                  _resident(cmp_mask.shape, lambda g, i: (0, 0)),
                  _resident(win_mask.shape, lambda g, i: (0, 0)),
                  _resident((ATT_GROUPS, nc, LANES), lambda g, i: (g, 0, 0)),
                  _resident((ATT_GROUPS, LANES, nc), lambda g, i: (g, 0, 0)),
                  _resident((ATT_GROUPS, s, LANES), lambda g, i: (g, 0, 0)),
                  _resident((ATT_GROUPS, V_ROWS, s), lambda g, i: (g, 0, 0)),
                  _resident((ATT_GROUPS, s, LANES), lambda g, i: (g, 0, 0)),
                  _resident((ATT_GROUPS, V_ROWS, s), lambda g, i: (g, 0, 0))],
        out_specs=pl.BlockSpec((rows, Q_BLK), lambda g, i: (g, i)),
        out_shape=jax.ShapeDtypeStruct((nqd, s), BF16),
        scratch_shapes=[pltpu.VMEM((ATT_GROUPS, nb, nq), F32),
                        pltpu.VMEM((ATT_GROUPS, PS_PAD + nc, Q_BLK), F32),
                        pltpu.VMEM((ATT_GROUPS, nc, nq), F32),
                        *[pltpu.VMEM((SEL_CHUNK, nq), BF16) for _ in range(2 * ATT_GROUPS)],
                        *[pltpu.VMEM((SEL_CHUNK, nq), BF16) for _ in range(2 * ATT_GROUPS)]],
        compiler_params=_params("arbitrary", "arbitrary"),
        name="nsa_attention",
    )(q_t, gate_t, cmp_mask, win_mask, kc, vct, ks, vst, kw, vwt)


def _outproj_kernel(ot_ref, x_ref, vec_ref, w_ref, o_ref):
    vec = vec_ref[...]
    y = _dot_tn(ot_ref[...], w_ref[...])
    o_ref[...] = _post_residual(x_ref[...], y, vec[3:4], vec[4:5])


def _outproj(o_t, x, vec, w_out):
    s, d = x.shape
    nqd = o_t.shape[0]
    return pl.pallas_call(
        _outproj_kernel,
        grid=(s // ROW_TILE,),
        in_specs=[pl.BlockSpec((nqd, ROW_TILE), lambda i: (0, i)),
                  pl.BlockSpec((ROW_TILE, d), lambda i: (i, 0)),
                  _resident((SUBLANES, d), lambda i: (0, 0)),
                  _resident((nqd, d), lambda i: (0, 0))],
        out_specs=pl.BlockSpec((ROW_TILE, d), lambda i: (i, 0)),
        out_shape=jax.ShapeDtypeStruct((s, d), F32),
        compiler_params=_params("arbitrary"),
        name="nsa_outproj",
    )(o_t, x, vec, w_out)


def _vec(pre_g, mod, post_g):
    d = pre_g.shape[0]
    shift, scale, gate = mod[:d], mod[d:2 * d], mod[2 * d:3 * d]
    rows = [pre_g, scale, shift, post_g, gate]
    return jnp.stack(rows + [jnp.zeros((d,), F32)] * (SUBLANES - len(rows)))


def kernel(x, c, ada_w, ada_b, norm_pre_g, norm_post_g, a_w_in, a_ln_g, a_ln_b, a_ws, a_bs, a_w_out, kv_norm_g, kv_ada_w, kv_ada_b, w_kv, cmp_pos, cmp_w1, cmp_b1, cmp_w2, cmp_b2, b_w_in, b_w_out, ffn_w_gate, ffn_w_up, ffn_w_down, moe_router, moe_router_b, moe_w_gate, moe_w_up, moe_w_down):
    batch, s, d = x.shape
    assert batch == 1 and s % SEL_CHUNK == 0 and s >= WINDOW + Q_BLK
    depth = ada_w.shape[0]
    n_a = depth // 2
    xs = x.reshape(s, d)

    mods = _ada(c, ada_w.reshape(depth * 2, d, 3 * d), ada_b.reshape(depth * 2, 3 * d)).reshape(depth, 2, 3 * d)
    kv_mod = _ada(c, kv_ada_w.reshape(1, d, 2 * d), kv_ada_b.reshape(1, 2 * d))[0]
    shared = None
    n_moe, _, _, f_moe = moe_w_gate.shape
    moe_wg = moe_w_gate.astype(BF16).reshape(n_moe * N_EXPERTS, d, f_moe)
    moe_wu = moe_w_up.astype(BF16).reshape(n_moe * N_EXPERTS, d, f_moe)
    moe_wd = moe_w_down.astype(BF16).reshape(n_moe * N_EXPERTS, f_moe, d)

    for layer in range(depth):
        vec = _vec(norm_pre_g[layer, 0], mods[layer, 0], norm_post_g[layer, 0])
        if layer < n_a:
            xs = _gmlp_layer(xs, vec, a_w_in[layer].astype(BF16),
                             jnp.stack([a_ln_g[layer], a_ln_b[layer]]),
                             a_ws[layer], a_bs[layer].T, a_w_out[layer].astype(BF16))
        else:
            if shared is None:
                kv_vec = _vec(kv_norm_g, jnp.concatenate([kv_mod, jnp.zeros((d,), F32)]), jnp.zeros((d,), F32))
                gd = N_KV * HEAD_DIM
                w_vt = jnp.concatenate([w_kv[:, 3 * gd:4 * gd], w_kv[:, 5 * gd:6 * gd]], axis=1).T
                raw, ks, kw, vst, vwt = _kvproj(xs, kv_vec, w_kv.astype(BF16), w_vt.astype(BF16))
                nc = s // D_CMP
                width = D_CMP * HEAD_DIM
                hid = cmp_w1.shape[-1]
                w2p = jnp.pad(cmp_w2, ((0, 0), (0, 0), (0, LANES - HEAD_DIM))).astype(BF16)
                b2p = jnp.pad(cmp_b2, ((0, 0), (0, LANES - HEAD_DIM))).reshape(2, 1, LANES)
                cmp_n, cmp_t = _compress(raw.reshape(2, N_KV, nc, width),
                                         cmp_pos.reshape(2, 2, 1, width),
                                         cmp_w1.reshape(2, 2, width, hid).astype(BF16),
                                         cmp_b1.reshape(2, 1, hid), w2p, b2p)
                shared = (cmp_n[0], cmp_t[1], ks, vst, kw, vwt)
            i = layer - n_a
            nq = N_HEADS * HEAD_DIM
            w_t = b_w_in[i].T.astype(BF16)
            q_t, gate_t = _qproj(xs, vec, w_t)
            kc, vct, ks, vst, kw, vwt = shared
            o_t = _attention(q_t, gate_t, kc, vct, ks, vst, kw, vwt)
            xs = _outproj(o_t, xs, vec, b_w_out[i].astype(BF16))

        vec = _vec(norm_pre_g[layer, 1], mods[layer, 1], norm_post_g[layer, 1])
        j = layer // 2
        if layer % 2 == 0:
            xs = _swiglu_layer(xs, vec, ffn_w_gate[j].astype(BF16), ffn_w_up[j].astype(BF16),
                               ffn_w_down[j].astype(BF16))
        else:
            xs = _moe_layer(xs, vec, moe_router[j].T, moe_router_b[j].reshape(N_EXPERTS, 1),
                            moe_wg, moe_wu, moe_wd, j * N_EXPERTS)
    return xs.reshape(batch, s, d)
```

```python
import jax
import jax.numpy as jnp
from jax import lax
from jax.experimental import pallas as pl
from jax.experimental.pallas import tpu as pltpu

F32 = jnp.float32
BF16 = jnp.bfloat16
I32 = jnp.int32

EPS = 1e-6
NEG = -1e30

LANES = 128
SUBLANES = 8
VMEM_LIMIT_BYTES = 56 * 1024 * 1024

CHUNK = 128
A_GROUPS = 8
N_HEADS = 16
N_KV = 4
HPG = N_HEADS // N_KV
HEAD_DIM = 64
L_CMP = 32
D_CMP = 16
L_SEL = 64
SEL_SHIFT = 6
N_SELECT = 16
N_FORCED = 3
WINDOW = 512
Q_BLK = 128
N_EXPERTS = 8
MOE_BLOCK = 256

ROW_TILE = 512
TOK_TILE = 256
DISP_TILE = 512
SEL_CHUNK = 512
SEL_BLKS = SEL_CHUNK // L_SEL
CMP_CHUNK = 128
ATT_GROUPS = 2
GATE_ROWS = 16
PS_PAD = 8
V_ROWS = 72
F_CHUNK = 512
WIN_PART_ROWS = 128
WIN_PARTS = TOK_TILE // WIN_PART_ROWS + 1
WIN_ROWS = WIN_PARTS * WIN_PART_ROWS
LOG2E = 1.4426950408889634


def _params(*sem):
    return pltpu.CompilerParams(dimension_semantics=sem, vmem_limit_bytes=VMEM_LIMIT_BYTES)


def _resident(shape, index_map):
    return pl.BlockSpec(shape, index_map, pipeline_mode=pl.Buffered(1))


def _split_bf16(a):
    hi = a.astype(BF16)
    lo = (a - hi.astype(F32)).astype(BF16)
    return hi, lo


def _dot(a, b):
    return jnp.dot(a, b, preferred_element_type=F32)


def _dot_nt(a, b):
    return lax.dot_general(a, b, (((1,), (1,)), ((), ())), preferred_element_type=F32)


def _dot_tn(a, b):
    return lax.dot_general(a, b, (((0,), (0,)), ((), ())), preferred_element_type=F32)


def _prenorm(x, g, scale, shift):
    ms = jnp.mean(x * x, axis=-1, keepdims=True)
    return (x * lax.rsqrt(ms + EPS) * g) * (1.0 + scale) + shift


def _post_residual(x, y, g, gate):
    ms = jnp.mean(y * y, axis=-1, keepdims=True)
    return x + gate * (y * lax.rsqrt(ms + EPS) * g)


def _ada_kernel(c_ref, w_ref, b_ref, o_ref):
    c = c_ref[...]
    c_act = jnp.broadcast_to(c * jax.nn.sigmoid(c), (SUBLANES, c.shape[1]))
    c_hi, c_lo = _split_bf16(c_act)
    w_hi, w_lo = _split_bf16(w_ref[...])
    m = _dot(c_hi, w_hi) + (_dot(c_hi, w_lo) + _dot(c_lo, w_hi))
    o_ref[...] = m[0:1] + b_ref[...]


def _ada(c, w, b):
    n, d, nn = w.shape
    out = pl.pallas_call(
        _ada_kernel,
        grid=(n, nn // d),
        in_specs=[pl.BlockSpec((1, d), lambda i, j: (0, 0)),
                  pl.BlockSpec((None, d, d), lambda i, j: (i, 0, j)),
                  pl.BlockSpec((None, 1, d), lambda i, j: (i, 0, j))],
        out_specs=pl.BlockSpec((None, 1, d), lambda i, j: (i, 0, j)),
        out_shape=jax.ShapeDtypeStruct((n, 1, nn), F32),
        compiler_params=_params("arbitrary", "arbitrary"),
        name="ada",
    )(c, w, b.reshape(n, 1, nn))
    return out.reshape(n, nn)


def _gmlp_kernel(x_ref, vec_ref, win_ref, ln_ref, ws_ref, bst_ref, wout_ref, o_ref, gated_ref):
    x = x_ref[...]
    vec = vec_ref[...]
    h = _prenorm(x, vec[0:1], vec[1:2], vec[2:3]).astype(BF16)
    z = jax.nn.gelu(_dot(h, win_ref[...]))
    width = z.shape[1] // 2
    u = z[:, :width]
    v = z[:, width:]
    mu = jnp.mean(v, axis=-1, keepdims=True)
    vc = v - mu
    var = jnp.mean(vc * vc, axis=-1, keepdims=True)
    ln = ln_ref[...]
    vn = (vc * lax.rsqrt(var + EPS) * ln[0:1] + ln[1:2]).astype(BF16)
    causal = (lax.broadcasted_iota(I32, (CHUNK, CHUNK), 0)
              >= lax.broadcasted_iota(I32, (CHUNK, CHUNK), 1))
    gw = width // A_GROUPS
    bst = bst_ref[...]
    for g in range(A_GROUPS):
        wg = jnp.where(causal, ws_ref[g], 0.0).astype(BF16)
        for ck in range(x.shape[0] // CHUNK):
            rows = slice(ck * CHUNK, (ck + 1) * CHUNK)
            cols = slice(g * gw, (g + 1) * gw)
            mixed = _dot(wg, vn[rows, cols]) + bst[:, g:g + 1]
            gated_ref[rows, cols] = (u[rows, cols] * mixed).astype(BF16)
    y = _dot(gated_ref[...], wout_ref[...])
    o_ref[...] = _post_residual(x, y, vec[3:4], vec[4:5])


def _gmlp_layer(x, vec, w_in, ln, ws, bst, w_out):
    s, d = x.shape
    e2 = w_in.shape[1]
    return pl.pallas_call(
        _gmlp_kernel,
        grid=(s // ROW_TILE,),
        in_specs=[pl.BlockSpec((ROW_TILE, d), lambda i: (i, 0)),
                  _resident((SUBLANES, d), lambda i: (0, 0)),
                  _resident((d, e2), lambda i: (0, 0)),
                  _resident((2, e2 // 2), lambda i: (0, 0)),
                  _resident(ws.shape, lambda i: (0, 0, 0)),
                  _resident(bst.shape, lambda i: (0, 0)),
                  _resident((e2 // 2, d), lambda i: (0, 0))],
        out_specs=pl.BlockSpec((ROW_TILE, d), lambda i: (i, 0)),
        out_shape=jax.ShapeDtypeStruct((s, d), F32),
        scratch_shapes=[pltpu.VMEM((ROW_TILE, e2 // 2), BF16)],
        compiler_params=_params("arbitrary"),
        name="gmlp",
    )(x, vec, w_in, ln, ws, bst, w_out)


def _swiglu_kernel(x_ref, vec_ref, wg_ref, wu_ref, wd_ref, o_ref):
    x = x_ref[...]
    vec = vec_ref[...]
    h = _prenorm(x, vec[0:1], vec[1:2], vec[2:3]).astype(BF16)
    g = _dot(h, wg_ref[...])
    a = (g * jax.nn.sigmoid(g) * _dot(h, wu_ref[...])).astype(BF16)
    y = _dot(a, wd_ref[...])
    o_ref[...] = _post_residual(x, y, vec[3:4], vec[4:5])


def _swiglu_layer(x, vec, w_gate, w_up, w_down):
    s, d = x.shape
    f = w_gate.shape[1]
    return pl.pallas_call(
        _swiglu_kernel,
        grid=(s // ROW_TILE,),
        in_specs=[pl.BlockSpec((ROW_TILE, d), lambda i: (i, 0)),
                  _resident((SUBLANES, d), lambda i: (0, 0)),
                  _resident((d, f), lambda i: (0, 0)),
                  _resident((d, f), lambda i: (0, 0)),
                  _resident((f, d), lambda i: (0, 0))],
        out_specs=pl.BlockSpec((ROW_TILE, d), lambda i: (i, 0)),
        out_shape=jax.ShapeDtypeStruct((s, d), F32),
        compiler_params=_params("arbitrary"),
        name="swiglu",
    )(x, vec, w_gate, w_up, w_down)


def _router_kernel(x_ref, vec_ref, wrt_ref, br_ref, h_ref, idx_ref, gate_ref):
    x = x_ref[...]
    vec = vec_ref[...]
    h = _prenorm(x, vec[0:1], vec[1:2], vec[2:3])
    h_ref[...] = h.astype(BF16)
    h_hi, h_lo = _split_bf16(h)
    w_hi, w_lo = _split_bf16(wrt_ref[...])
    logit = _dot_nt(w_hi, h_hi) + (_dot_nt(w_hi, h_lo) + _dot_nt(w_lo, h_hi)) + br_ref[...]
    ne = logit.shape[0]
    eidx = lax.broadcasted_iota(I32, logit.shape, 0)
    m1 = jnp.max(logit, axis=0, keepdims=True)
    i1 = jnp.min(jnp.where(logit == m1, eidx, ne), axis=0, keepdims=True)
    rest = jnp.where(eidx == i1, -jnp.inf, logit)
    m2 = jnp.max(rest, axis=0, keepdims=True)
    i2 = jnp.min(jnp.where(rest == m2, eidx, ne), axis=0, keepdims=True)
    e21 = jnp.exp(m2 - m1)
    g1 = 1.0 / (1.0 + e21)
    g2 = e21 * g1
    pad_i = jnp.zeros((SUBLANES - 2, i1.shape[1]), I32)
    pad_f = jnp.zeros((SUBLANES - 2, i1.shape[1]), F32)
    idx_ref[...] = jnp.concatenate([i1, i2, pad_i], axis=0)
    gate_ref[...] = jnp.concatenate([g1, g2, pad_f], axis=0)


def _router(x, vec, w_rt, b_r):
    s, d = x.shape
    ne = w_rt.shape[0]
    return pl.pallas_call(
        _router_kernel,
        grid=(s // ROW_TILE,),
        in_specs=[pl.BlockSpec((ROW_TILE, d), lambda i: (i, 0)),
                  _resident((SUBLANES, d), lambda i: (0, 0)),
                  _resident((ne, d), lambda i: (0, 0)),
                  _resident((ne, 1), lambda i: (0, 0))],
        out_specs=[pl.BlockSpec((ROW_TILE, d), lambda i: (i, 0)),
                   pl.BlockSpec((SUBLANES, ROW_TILE), lambda i: (0, i)),
                   pl.BlockSpec((SUBLANES, ROW_TILE), lambda i: (0, i))],
        out_shape=[jax.ShapeDtypeStruct((s, d), BF16),
                   jax.ShapeDtypeStruct((SUBLANES, s), I32),
                   jax.ShapeDtypeStruct((SUBLANES, s), F32)],
        compiler_params=_params("arbitrary"),
        name="moe_router",
    )(x, vec, w_rt, b_r)


def _rank_kernel(idx_ref, rank_ref, start_ref, count_ref, carry_ref):
    @pl.when(pl.program_id(0) == 0)
    def _():
        carry_ref[...] = jnp.zeros_like(carry_ref)

    idx = idx_ref[...]
    tt = idx.shape[1]
    eidx = lax.broadcasted_iota(I32, (N_EXPERTS, tt), 0)
    hit1 = eidx == idx[0:1]
    hit2 = eidx == idx[1:2]
    member = jnp.where(hit1 | hit2, 1.0, 0.0)
    before = (lax.broadcasted_iota(I32, (tt, tt), 0) < lax.broadcasted_iota(I32, (tt, tt), 1))
    carry = carry_ref[...]
    cum = _dot(member.astype(BF16), jnp.where(before, 1.0, 0.0).astype(BF16)) + carry[:, 0:1]
    r1 = jnp.sum(jnp.where(hit1, cum, 0.0), axis=0, keepdims=True)
    r2 = jnp.sum(jnp.where(hit2, cum, 0.0), axis=0, keepdims=True)
    rank_ref[...] = jnp.concatenate([r1, r2, jnp.zeros((SUBLANES - 2, tt), F32)], axis=0).astype(I32)
    tile_count = jnp.broadcast_to(jnp.sum(member, axis=1, keepdims=True), carry.shape)
    start_ref[...] = carry
    count_ref[...] = tile_count
    carry_ref[...] = carry + tile_count


def _ranks(idx):
    s = idx.shape[1]
    nt = s // TOK_TILE
    return pl.pallas_call(
        _rank_kernel,
        grid=(nt,),
        in_specs=[pl.BlockSpec((SUBLANES, TOK_TILE), lambda i: (0, i))],
        out_specs=[pl.BlockSpec((SUBLANES, TOK_TILE), lambda i: (0, i)),
                   pl.BlockSpec((None, N_EXPERTS, LANES), lambda i: (i, 0, 0)),
                   pl.BlockSpec((None, N_EXPERTS, LANES), lambda i: (i, 0, 0))],
        out_shape=[jax.ShapeDtypeStruct((SUBLANES, s), I32),
                   jax.ShapeDtypeStruct((nt, N_EXPERTS, LANES), F32),
                   jax.ShapeDtypeStruct((nt, N_EXPERTS, LANES), F32)],
        scratch_shapes=[pltpu.VMEM((N_EXPERTS, LANES), F32)],
        compiler_params=_params("arbitrary"),
        name="moe_ranks",
    )(idx)


def _match(idx, rank, expert, rows):
    r1 = jnp.where(idx[0:1] == expert, rank[0:1], -1)
    r2 = jnp.where(idx[1:2] == expert, rank[1:2], -1)
    return rows == r1, rows == r2


def _dispatch_kernel(be_ref, lb_ref, tlo_ref, thi_ref, idx_ref, rank_ref, gate_ref, h_ref, o_ref, rg_ref,
                     acc_ref, gacc_ref):
    b = pl.program_id(0)
    expert = be_ref[b]
    rows = lax.broadcasted_iota(I32, (MOE_BLOCK, DISP_TILE), 0) + lb_ref[b]
    acc_ref[...] = jnp.zeros_like(acc_ref)
    gacc_ref[...] = jnp.zeros_like(gacc_ref)

    def body(t, carry):
        off = pl.multiple_of(t * DISP_TILE, DISP_TILE)
        m1, m2 = _match(idx_ref[:, pl.ds(off, DISP_TILE)], rank_ref[:, pl.ds(off, DISP_TILE)], expert, rows)
        gates = gate_ref[:, pl.ds(off, DISP_TILE)]
        gacc_ref[...] += jnp.sum(jnp.where(m1, gates[0:1], 0.0) + jnp.where(m2, gates[1:2], 0.0),
                                 axis=1, keepdims=True)
        onehot = jnp.where(m1 | m2, 1.0, 0.0).astype(BF16)
        acc_ref[...] += _dot(onehot, h_ref[pl.ds(off, DISP_TILE), :])
        return carry

    lax.fori_loop(tlo_ref[b], thi_ref[b], body, 0)
    o_ref[...] = acc_ref[...].astype(BF16)
    rg_ref[...] = gacc_ref[...]


def _dispatch(blk_e, blk_lb, blk_tlo, blk_thi, idx, rank, gates, h):
    s, d = h.shape
    n_blk = blk_e.shape[0]
    grid_spec = pltpu.PrefetchScalarGridSpec(
        num_scalar_prefetch=4,
        grid=(n_blk,),
        in_specs=[_resident((SUBLANES, s), lambda b, *_: (0, 0)),
                  _resident((SUBLANES, s), lambda b, *_: (0, 0)),
                  _resident((SUBLANES, s), lambda b, *_: (0, 0)),
                  _resident((s, d), lambda b, *_: (0, 0))],
        out_specs=[pl.BlockSpec((MOE_BLOCK, d), lambda b, *_: (b, 0)),
                   pl.BlockSpec((MOE_BLOCK, 1), lambda b, *_: (b, 0))],
        scratch_shapes=[pltpu.VMEM((MOE_BLOCK, d), F32), pltpu.VMEM((MOE_BLOCK, 1), F32)],
    )
    return pl.pallas_call(
        _dispatch_kernel,
        grid_spec=grid_spec,
        out_shape=[jax.ShapeDtypeStruct((n_blk * MOE_BLOCK, d), BF16),
                   jax.ShapeDtypeStruct((n_blk * MOE_BLOCK, 1), F32)],
        compiler_params=_params("arbitrary"),
        name="moe_dispatch",
    )(blk_e, blk_lb, blk_tlo, blk_thi, idx, rank, gates, h)


def _expert_kernel(be_ref, used_ref, x_ref, rg_ref, wg_ref, wu_ref, wd_ref, o_ref):
    b = pl.program_id(0)

    @pl.when(b < used_ref[0])
    def _():
        x = x_ref[...]
        f = wg_ref.shape[1]
        y = jnp.zeros(o_ref.shape, F32)
        for c in range(f // F_CHUNK):
            cols = slice(c * F_CHUNK, (c + 1) * F_CHUNK)
            g = _dot(x, wg_ref[:, cols])
            a = (g * jax.nn.sigmoid(g) * _dot(x, wu_ref[:, cols])).astype(BF16)
            y = y + _dot(a, wd_ref[cols, :])
        o_ref[...] = (y * rg_ref[...]).astype(BF16)

    @pl.when(b >= used_ref[0])
    def _():
        o_ref[...] = jnp.zeros_like(o_ref)


def _experts(blk_e, n_used, x_buf, row_gate, w_gate, w_up, w_down):
    n_rows, d = x_buf.shape
    f = w_gate.shape[2]
    assert f % F_CHUNK == 0
    grid_spec = pltpu.PrefetchScalarGridSpec(
        num_scalar_prefetch=2,
        grid=(n_rows // MOE_BLOCK,),
        in_specs=[pl.BlockSpec((MOE_BLOCK, d), lambda b, be, nu: (b, 0)),
                  pl.BlockSpec((MOE_BLOCK, 1), lambda b, be, nu: (b, 0)),
                  pl.BlockSpec((None, d, f), lambda b, be, nu: (be[b], 0, 0)),
                  pl.BlockSpec((None, d, f), lambda b, be, nu: (be[b], 0, 0)),
                  pl.BlockSpec((None, f, d), lambda b, be, nu: (be[b], 0, 0))],
        out_specs=pl.BlockSpec((MOE_BLOCK, d), lambda b, be, nu: (b, 0)),
    )
    return pl.pallas_call(
        _expert_kernel,
        grid_spec=grid_spec,
        out_shape=jax.ShapeDtypeStruct((n_rows, d), BF16),
        compiler_params=_params("arbitrary"),
        name="moe_experts",
    )(blk_e, n_used, x_buf, row_gate, w_gate, w_up, w_down)


def _combine_kernel(win_ref, lb_ref, tail_ref, idx_ref, rank_ref, *refs):
    y_refs = refs[:N_EXPERTS * WIN_PARTS]
    x_ref, vec_ref, o_ref, acc_ref = refs[N_EXPERTS * WIN_PARTS:]
    t = pl.program_id(0)
    idx = idx_ref[...]
    rank = rank_ref[...]
    main_rows = (WIN_PARTS - 1) * WIN_PART_ROWS

    def gathered(e, first_part, n_parts):
        n_rows = n_parts * WIN_PART_ROWS
        rows = (lax.broadcasted_iota(I32, (n_rows, TOK_TILE), 0)
                + (lb_ref[t * N_EXPERTS + e] + first_part * WIN_PART_ROWS))
        m1, m2 = _match(idx, rank, e, rows)
        onehot = jnp.where(m1 | m2, 1.0, 0.0).astype(BF16)
        y = jnp.concatenate([y_refs[e * WIN_PARTS + first_part + k][...] for k in range(n_parts)], axis=0)
        return _dot_tn(onehot, y)

    acc = jnp.zeros(o_ref.shape, F32)
    for e in range(N_EXPERTS):
        acc = acc + gathered(e, 0, WIN_PARTS - 1)
    acc_ref[...] = acc
    for e in range(N_EXPERTS):
        @pl.when(tail_ref[t * N_EXPERTS + e] > 0)
        def _():
            acc_ref[...] += gathered(e, WIN_PARTS - 1, 1)
    vec = vec_ref[...]
    o_ref[...] = _post_residual(x_ref[...], acc_ref[...], vec[3:4], vec[4:5])


def _combine(win_start, win_lb, win_tail, idx, rank, y_buf, x, vec):
    s, d = x.shape

    def y_spec(e, k):
        return pl.BlockSpec((WIN_PART_ROWS, d), lambda t, ws, lb, tl: (ws[t * N_EXPERTS + e] + k, 0))

    grid_spec = pltpu.PrefetchScalarGridSpec(
        num_scalar_prefetch=3,
        grid=(s // TOK_TILE,),
        in_specs=[pl.BlockSpec((SUBLANES, TOK_TILE), lambda t, *_: (0, t)),
                  pl.BlockSpec((SUBLANES, TOK_TILE), lambda t, *_: (0, t))]
                 + [y_spec(e, k) for e in range(N_EXPERTS) for k in range(WIN_PARTS)]
                 + [pl.BlockSpec((TOK_TILE, d), lambda t, *_: (t, 0)),
                    _resident((SUBLANES, d), lambda t, *_: (0, 0))],
        out_specs=pl.BlockSpec((TOK_TILE, d), lambda t, *_: (t, 0)),
        scratch_shapes=[pltpu.VMEM((TOK_TILE, d), F32)],
    )
    return pl.pallas_call(
        _combine_kernel,
        grid_spec=grid_spec,
        out_shape=jax.ShapeDtypeStruct((s, d), F32),
        compiler_params=_params("arbitrary"),
        name="moe_combine",
    )(win_start, win_lb, win_tail, idx, rank, *([y_buf] * (N_EXPERTS * WIN_PARTS)), x, vec)


def _moe_layer(x, vec, w_rt, b_r, w_gate, w_up, w_down, w_first):
    s, d = x.shape
    nt = s // TOK_TILE
    h, idx, gates = _router(x, vec, w_rt, b_r)
    rank, tile_start, tile_count = _ranks(idx)

    tile_start = tile_start[:, :, 0].astype(I32)
    tile_count = tile_count[:, :, 0].astype(I32)
    tile_end = tile_start + tile_count
    counts = tile_end[-1]
    padded = (counts + MOE_BLOCK - 1) // MOE_BLOCK * MOE_BLOCK
    pend = jnp.cumsum(padded)
    pstart = pend - padded
    n_rows = -(-(2 * s) // MOE_BLOCK) * MOE_BLOCK + N_EXPERTS * MOE_BLOCK
    n_blk = n_rows // MOE_BLOCK
    blk_row = jnp.arange(n_blk, dtype=I32) * MOE_BLOCK
    blk_e = jnp.minimum(jnp.sum(blk_row[:, None] >= pend[None, :], axis=1), N_EXPERTS - 1).astype(I32)
    blk_lb = blk_row - pstart[blk_e]
    per = DISP_TILE // TOK_TILE
    te = tile_end[per - 1::per][:, blk_e]
    ts = tile_start[::per][:, blk_e]
    blk_tlo = jnp.sum(te <= blk_lb[None, :], axis=0).astype(I32)
    blk_thi = jnp.sum(ts < (blk_lb + MOE_BLOCK)[None, :], axis=0).astype(I32)
    n_used = (pend[-1] // MOE_BLOCK).astype(I32).reshape(1)

    x_buf, row_gate = _dispatch(blk_e, blk_lb.astype(I32), blk_tlo, blk_thi, idx, rank, gates, h)
    y_buf = _experts(blk_e + w_first, n_used, x_buf, row_gate, w_gate, w_up, w_down)

    win_start = jnp.minimum((pstart[None, :] + tile_start) // WIN_PART_ROWS, n_rows // WIN_PART_ROWS - WIN_PARTS)
    win_lb = win_start * WIN_PART_ROWS - pstart[None, :]
    win_tail = tile_end - win_lb > (WIN_PARTS - 1) * WIN_PART_ROWS
    return _combine(win_start.reshape(-1).astype(I32), win_lb.reshape(-1).astype(I32),
                    win_tail.reshape(-1).astype(I32), idx, rank, y_buf, x, vec)


def _kvproj_kernel(x_ref, vec_ref, wkv_ref, wvt_ref, raw_ref, ks_ref, kw_ref, vst_ref, vwt_ref):
    x = x_ref[...]
    vec = vec_ref[...]
    h = _prenorm(x, vec[0:1], vec[1:2], vec[2:3]).astype(BF16)
    kv = _dot(h, wkv_ref[...])
    vt = _dot_nt(wvt_ref[...], h)
    tm = x.shape[0]
    gd = N_KV * HEAD_DIM
    key_blk = jnp.right_shift(pl.program_id(0) * tm + lax.broadcasted_iota(I32, (tm, SEL_BLKS), 0), SEL_SHIFT)
    ind = jnp.where((key_blk & (SEL_BLKS - 1)) == lax.broadcasted_iota(I32, (tm, SEL_BLKS), 1), 1.0, 0.0)
    pad_s = jnp.zeros((tm, LANES - HEAD_DIM - SEL_BLKS), F32)
    pad_w = jnp.zeros((tm, LANES - HEAD_DIM), F32)
    ones_row = jnp.concatenate([jnp.ones((1, tm), F32), jnp.zeros((V_ROWS - HEAD_DIM - 1, tm), F32)], axis=0)
    for g in range(N_KV):
        c = g * HEAD_DIM
        raw_ref[0, g] = kv[:, c:c + HEAD_DIM]
        raw_ref[1, g] = kv[:, gd + c:gd + c + HEAD_DIM]
        ks = kv[:, 2 * gd + c:2 * gd + c + HEAD_DIM]
        kw = kv[:, 4 * gd + c:4 * gd + c + HEAD_DIM]
        ks_ref[g] = jnp.concatenate([ks, ind, pad_s], axis=1).astype(BF16)
        kw_ref[g] = jnp.concatenate([kw, pad_w], axis=1).astype(BF16)
        vst_ref[g] = jnp.concatenate([vt[c:c + HEAD_DIM], ones_row], axis=0).astype(BF16)
        vwt_ref[g] = jnp.concatenate([vt[gd + c:gd + c + HEAD_DIM], ones_row], axis=0).astype(BF16)


def _kvproj(x, vec, w_kv, w_vt):
    s, d = x.shape
    nkv = w_kv.shape[1]
    return pl.pallas_call(
        _kvproj_kernel,
        grid=(s // ROW_TILE,),
        in_specs=[pl.BlockSpec((ROW_TILE, d), lambda i: (i, 0)),
                  _resident((SUBLANES, d), lambda i: (0, 0)),
                  _resident((d, nkv), lambda i: (0, 0)),
                  _resident(w_vt.shape, lambda i: (0, 0))],
        out_specs=[pl.BlockSpec((2, N_KV, ROW_TILE, HEAD_DIM), lambda i: (0, 0, i, 0)),
                   pl.BlockSpec((N_KV, ROW_TILE, LANES), lambda i: (0, i, 0)),
                   pl.BlockSpec((N_KV, ROW_TILE, LANES), lambda i: (0, i, 0)),
                   pl.BlockSpec((N_KV, V_ROWS, ROW_TILE), lambda i: (0, 0, i)),
                   pl.BlockSpec((N_KV, V_ROWS, ROW_TILE), lambda i: (0, 0, i))],
        out_shape=[jax.ShapeDtypeStruct((2, N_KV, s, HEAD_DIM), F32),
                   jax.ShapeDtypeStruct((N_KV, s, LANES), BF16),
                   jax.ShapeDtypeStruct((N_KV, s, LANES), BF16),
                   jax.ShapeDtypeStruct((N_KV, V_ROWS, s), BF16),
                   jax.ShapeDtypeStruct((N_KV, V_ROWS, s), BF16)],
        compiler_params=_params("arbitrary"),
        name="nsa_kvproj",
    )(x, vec, w_kv, w_vt)


def _compress_kernel(raw_ref, pos_ref, w1_ref, b1_ref, w2_ref, b2_ref, n_ref, t_ref):
    raw = raw_ref[...]
    nc = raw.shape[0]
    first = _dot((raw + pos_ref[0]).astype(BF16), w1_ref[0])
    second = _dot((raw + pos_ref[1]).astype(BF16), w1_ref[1])
    hid = jax.nn.gelu(first + pltpu.roll(second, nc - 1, 0) + b1_ref[...])
    out = _dot(hid.astype(BF16), w2_ref[...]) + b2_ref[...]
    n_ref[...] = out.astype(BF16)
    row = lax.broadcasted_iota(I32, (LANES, nc), 0)
    t_ref[...] = jnp.where(row == HEAD_DIM, 1.0, out.T).astype(BF16)


def _compress(raw, pos, w1, b1, w2, b2):
    _, g, nc, width = raw.shape
    hid = w1.shape[-1]
    return pl.pallas_call(
        _compress_kernel,
        grid=(2, g),
        in_specs=[pl.BlockSpec((None, None, nc, width), lambda j, k: (j, k, 0, 0)),
                  pl.BlockSpec((None, 2, 1, width), lambda j, k: (j, 0, 0, 0)),
                  pl.BlockSpec((None, 2, width, hid), lambda j, k: (j, 0, 0, 0)),
                  pl.BlockSpec((None, 1, hid), lambda j, k: (j, 0, 0)),
                  pl.BlockSpec((None, hid, LANES), lambda j, k: (j, 0, 0)),
                  pl.BlockSpec((None, 1, LANES), lambda j, k: (j, 0, 0))],
        out_specs=[pl.BlockSpec((None, None, nc, LANES), lambda j, k: (j, k, 0, 0)),
                   pl.BlockSpec((None, None, LANES, nc), lambda j, k: (j, k, 0, 0))],
        out_shape=[jax.ShapeDtypeStruct((2, g, nc, LANES), BF16),
                   jax.ShapeDtypeStruct((2, g, LANES, nc), BF16)],
        compiler_params=_params("arbitrary", "arbitrary"),
        name="nsa_compress",
    )(raw, pos, w1, b1, w2, b2)


def _qproj_kernel(x_ref, vec_ref, wt_ref, q_ref, gate_ref):
    x = x_ref[...]
    vec = vec_ref[...]
    h = _prenorm(x, vec[0:1], vec[1:2], vec[2:3]).astype(BF16)
    pt = _dot_nt(wt_ref[...], h)
    nq = N_HEADS * HEAD_DIM
    q_ref[...] = (pt[:nq] * (HEAD_DIM ** -0.5 * LOG2E)).astype(BF16)
    gates = jax.nn.sigmoid(pt[nq:nq + 3 * N_HEADS])
    per = 3 * HPG
    pad = jnp.zeros((GATE_ROWS - per, x.shape[0]), F32)
    for g in range(N_KV):
        gate_ref[g] = jnp.concatenate([gates[g * per:(g + 1) * per], pad], axis=0)


def _qproj(x, vec, w_t):
    s, d = x.shape
    return pl.pallas_call(
        _qproj_kernel,
        grid=(s // ROW_TILE,),
        in_specs=[pl.BlockSpec((ROW_TILE, d), lambda i: (i, 0)),
                  _resident((SUBLANES, d), lambda i: (0, 0)),
                  _resident(w_t.shape, lambda i: (0, 0))],
        out_specs=[pl.BlockSpec((N_HEADS * HEAD_DIM, ROW_TILE), lambda i: (0, i)),
                   pl.BlockSpec((N_KV, GATE_ROWS, ROW_TILE), lambda i: (0, 0, i))],
        out_shape=[jax.ShapeDtypeStruct((N_HEADS * HEAD_DIM, s), BF16),
                   jax.ShapeDtypeStruct((N_KV, GATE_ROWS, s), F32)],
        compiler_params=_params("arbitrary"),
        name="nsa_qproj",
    )(x, vec, w_t)


def _attn_kernel(q_ref, gate_ref, cmask_ref, wmask_ref, kc_ref, vct_ref, ks_ref, vst_ref, kw_ref, vwt_ref,
                 o_ref, bias_ref, ps_ref, sc_ref, *sp_refs):
    s_refs = [[sp_refs[2 * g + k] for k in range(2)] for g in range(ATT_GROUPS)]
    p_refs = [[sp_refs[2 * ATT_GROUPS + 2 * g + k] for k in range(2)] for g in range(ATT_GROUPS)]
    i = pl.program_id(1)
    nq = HPG * Q_BLK
    nc = kc_ref.shape[1]
    nb = bias_ref.shape[1]
    rows = HPG * HEAD_DIM
    t1 = i * Q_BLK + lax.broadcasted_iota(I32, (1, Q_BLK), 1)
    t4 = jnp.concatenate([t1] * HPG, axis=1)
    j_io = lax.broadcasted_iota(I32, (nb, Q_BLK), 0)
    jt = jnp.right_shift(t1, SEL_SHIFT)
    forced = (j_io == 0) | (j_io == jt) | (j_io == jt - 1)
    cand_off = jnp.where((j_io * L_SEL <= t1) & jnp.logical_not(forced), 0.0, NEG)
    past_off = jnp.where(j_io * L_SEL < i * Q_BLK, 0.0, NEG)
    doff = pl.multiple_of(i * Q_BLK, Q_BLK)
    cmask_off = nc - i * (Q_BLK // D_CMP)
    span = WINDOW + Q_BLK
    wstart = pl.multiple_of(jnp.maximum(i * Q_BLK - WINDOW, 0), Q_BLK)
    win_mask = wmask_ref[pl.ds(pl.multiple_of(WINDOW - jnp.minimum(i * Q_BLK, WINDOW), Q_BLK), span), :]
    win_mask = jnp.concatenate([win_mask] * HPG, axis=1)
    diag_mask = jnp.concatenate([wmask_ref[WINDOW:WINDOW + Q_BLK, :]] * HPG, axis=1)
    vpad = jnp.zeros((LANES - HEAD_DIM - 2 * SEL_BLKS, nq), BF16)
    bpad = jnp.zeros((SEL_BLKS, nq), F32)

    def aligned(x, m):
        return x if isinstance(x, int) else pl.multiple_of(x, m)

    groups = range(ATT_GROUPS)
    q4s, q_plains = [], []
    for g in groups:
        qb = q_ref[g * rows:(g + 1) * rows, :]
        q4s.append(jnp.concatenate([qb[h * HEAD_DIM:(h + 1) * HEAD_DIM] for h in range(HPG)], axis=1))
        q_plains.append(jnp.concatenate([q4s[g], jnp.zeros((LANES - HEAD_DIM, nq), BF16)], axis=0))

    def compressed(n_rows):
        chunks = [slice(r0, r0 + CMP_CHUNK) for r0 in range(0, n_rows, CMP_CHUNK)]
        ms = [jnp.full((1, nq), NEG, F32) for _ in groups]
        for rs in chunks:
            mask = cmask_ref[pl.ds(pl.multiple_of(cmask_off + rs.start, 8), CMP_CHUNK), :]
            mask = jnp.concatenate([mask] * HPG, axis=1)
            for g in groups:
                s = _dot(kc_ref[g, rs, :], q_plains[g]) + mask
                sc_ref[g, rs, :] = s
                ms[g] = jnp.maximum(ms[g], jnp.max(s, axis=0, keepdims=True))
        ls = [jnp.zeros((1, nq), F32) for _ in groups]
        for rs in chunks:
            for g in groups:
                e = jnp.exp2(sc_ref[g, rs, :] - ms[g])
                sc_ref[g, rs, :] = e
                ls[g] = ls[g] + jnp.sum(e, axis=0, keepdims=True)
        rls = [jnp.where(t4 >= L_CMP - 1, 1.0 / ls[g], 0.0) for g in groups]
        os_ = [jnp.zeros((HEAD_DIM, nq), F32) for _ in groups]
        for g in groups:
            ps_ref[g, 0:PS_PAD, :] = jnp.zeros((PS_PAD, Q_BLK), F32)
            if n_rows < nc:
                ps_ref[g, PS_PAD + n_rows:PS_PAD + nc, :] = jnp.zeros((nc - n_rows, Q_BLK), F32)
        for rs in chunks:
            for g in groups:
                pc = sc_ref[g, rs, :] * rls[g]
                os_[g] = os_[g] + _dot(vct_ref[g, 0:HEAD_DIM, rs], pc.astype(BF16))
                psum = pc[:, 0:Q_BLK]
                for h in range(1, HPG):
                    psum = psum + pc[:, h * Q_BLK:(h + 1) * Q_BLK]
                ps_ref[g, PS_PAD + rs.start:PS_PAD + rs.stop, :] = psum
        ratio = L_SEL // D_CMP
        imps = []
        for g in groups:
            imp = jnp.zeros((nb, Q_BLK), F32)
            for k in range(1 - L_CMP // D_CMP, ratio):
                imp = imp + ps_ref[g, pl.ds(PS_PAD + k, nb, stride=ratio), :]
            imps.append(imp)
        return tuple(os_) + tuple(imps)

    if nc % (2 * CMP_CHUNK) == 0:
        last_complete = ((i + 1) * Q_BLK - L_CMP) // D_CMP
        cmp_out = lax.cond(last_complete < nc // 2, lambda: compressed(nc // 2), lambda: compressed(nc))
    else:
        cmp_out = compressed(nc)
    o_cmps, imps = cmp_out[:ATT_GROUPS], cmp_out[ATT_GROUPS:]

    def select(imps):
        w = jnp.concatenate([imp + cand_off for imp in imps], axis=1)
        j_all = lax.broadcasted_iota(I32, w.shape, 0).astype(F32)
        for _ in range(N_SELECT - N_FORCED):
            m = jnp.max(w, axis=0, keepdims=True)
            first = jnp.min(jnp.where(w == m, j_all, float(nb)), axis=0, keepdims=True)
            first = jnp.where(m > 0.5 * NEG, first, -1.0)
            w = jnp.where(j_all == first, NEG, w)
        for g in range(ATT_GROUPS):
            bias1 = jnp.where(w[:, g * Q_BLK:(g + 1) * Q_BLK] < 0.5 * NEG, past_off, NEG)
            bias_ref[g] = jnp.concatenate([bias1] * HPG, axis=1)

    def local():
        sds = [_dot(ks_ref[g, pl.ds(doff, Q_BLK), :], q_plains[g]) + diag_mask for g in groups]
        sws = [_dot(kw_ref[g, pl.ds(wstart, span), :], q_plains[g]) + win_mask for g in groups]
        out = []
        for g in groups:
            m0 = jnp.max(sds[g], axis=0, keepdims=True)
            acc0 = _dot(vst_ref[g, :, pl.ds(doff, Q_BLK)], jnp.exp2(sds[g] - m0).astype(BF16))
            pw = jnp.exp2(sws[g] - jnp.max(sws[g], axis=0, keepdims=True)).astype(BF16)
            acc_w = _dot(vwt_ref[g, :, pl.ds(wstart, span)], pw)
            out.append((acc_w[0:HEAD_DIM] / acc_w[HEAD_DIM:HEAD_DIM + 1], m0, acc0))
        return out

    def query_operand(g, q4, c):
        brow = bias_ref[g, pl.ds(aligned(c * SEL_BLKS, SEL_BLKS), SEL_BLKS), :]
        b16 = jnp.concatenate([brow, bpad], axis=0).astype(BF16)
        return jnp.concatenate([q4, b16, vpad], axis=0)

    def phase(g, q4, c, slot, state, do_scores=True, do_softmax=True, do_values=True):
        m_run, alpha, acc, cmax = state
        other = 1 - slot
        if do_values:
            voff = aligned(c * SEL_CHUNK, SEL_CHUNK)
            acc = alpha * acc + _dot(vst_ref[g, :, pl.ds(voff, SEL_CHUNK)], p_refs[g][slot][...])
        if do_scores:
            koff = aligned((c + 2) * SEL_CHUNK, SEL_CHUNK)
            s = _dot(ks_ref[g, pl.ds(koff, SEL_CHUNK), :], query_operand(g, q4, c + 2))
            s_refs[g][slot][...] = s.astype(BF16)
            new_max = jnp.max(s, axis=0, keepdims=True)
        if do_softmax:
            m_new = jnp.maximum(m_run, cmax[other])
            p_refs[g][other][...] = jnp.exp2(s_refs[g][other][...] - m_new.astype(BF16))
            m_run, alpha = m_new, jnp.exp2(m_run - m_new)
        if do_scores:
            cmax = (new_max, cmax[1]) if slot == 0 else (cmax[0], new_max)
        return m_run, alpha, acc, cmax

    select(imps)
    heads = [(q4s[g], o_cmps[g]) + loc for g, loc in enumerate(local())]

    def pair(k, carries):
        out = []
        for g in range(ATT_GROUPS):
            q4 = heads[g][0]
            state = phase(g, q4, 2 * k, 0, carries[g])
            out.append(phase(g, q4, 2 * k + 1, 1, state))
        return tuple(out)

    n_pairs = jnp.maximum((i * Q_BLK + 2 * SEL_CHUNK - 1) // (2 * SEL_CHUNK), 1)
    init = []
    for g in range(ATT_GROUPS):
        q4, _, _, m0, acc0 = heads[g]
        neg_row = jnp.full((1, nq), NEG, F32)
        state = (m0, jnp.ones((1, nq), F32), acc0, (neg_row, neg_row))
        state = phase(g, q4, -2, 0, state, do_softmax=False, do_values=False)
        init.append(phase(g, q4, -1, 1, state, do_values=False))
    carries = lax.fori_loop(0, n_pairs - 1, pair, tuple(init))
    last = 2 * (n_pairs - 1)
    for g in range(ATT_GROUPS):
        q4, o_cmp, o_win, _, _ = heads[g]
        state = phase(g, q4, last, 0, carries[g], do_scores=False)
        _, _, acc_s, _ = phase(g, q4, last + 1, 1, state, do_scores=False, do_softmax=False)
        o_sel = acc_s[0:HEAD_DIM] / acc_s[HEAD_DIM:HEAD_DIM + 1]
        gates = gate_ref[g]
        outs = []
        for h in range(HPG):
            cols = slice(h * Q_BLK, (h + 1) * Q_BLK)
            outs.append(o_cmp[:, cols] * gates[3 * h:3 * h + 1]
                        + o_sel[:, cols] * gates[3 * h + 1:3 * h + 2]
                        + o_win[:, cols] * gates[3 * h + 2:3 * h + 3])
        o_ref[g * rows:(g + 1) * rows, :] = jnp.concatenate(outs, axis=0).astype(BF16)


def _attention(q_t, gate_t, kc, vct, ks, vst, kw, vwt):
    nqd, s = q_t.shape
    nc = kc.shape[1]
    nb = s // L_SEL
    nq = HPG * Q_BLK
    rows = ATT_GROUPS * HPG * HEAD_DIM
    assert (s // SEL_CHUNK) % 2 == 0 and s >= WINDOW + Q_BLK and N_KV % ATT_GROUPS == 0 and nb >= N_SELECT
    qq = jnp.arange(Q_BLK)[None, :]
    rc = jnp.arange(2 * nc)[:, None]
    cmp_mask = jnp.where(D_CMP * (rc - nc) + L_CMP - 1 <= qq, 0.0, NEG).astype(F32)
    rw = jnp.arange(2 * WINDOW + Q_BLK)[:, None]
    win_mask = jnp.where((qq < rw) & (rw <= qq + WINDOW), 0.0, NEG).astype(F32)
    return pl.pallas_call(
        _attn_kernel,
        grid=(N_KV // ATT_GROUPS, s // Q_BLK),
        in_specs=[pl.BlockSpec((rows, Q_BLK), lambda g, i: (g, i)),
                  pl.BlockSpec((ATT_GROUPS, GATE_ROWS, Q_BLK), lambda g, i: (g, 0, i)),
                  _resident(cmp_mask.shape, lambda g, i: (0, 0)),
                  _resident(win_mask.shape, lambda g, i: (0, 0)),
                  _resident((ATT_GROUPS, nc, LANES), lambda g, i: (g, 0, 0)),
                  _resident((ATT_GROUPS, LANES, nc), lambda g, i: (g, 0, 0)),
                  _resident((ATT_GROUPS, s, LANES), lambda g, i: (g, 0, 0)),
                  _resident((ATT_GROUPS, V_ROWS, s), lambda g, i: (g, 0, 0)),
                  _resident((ATT_GROUPS, s, LANES), lambda g, i: (g, 0, 0)),
                  _resident((ATT_GROUPS, V_ROWS, s), lambda g, i: (g, 0, 0))],
        out_specs=pl.BlockSpec((rows, Q_BLK), lambda g, i: (g, i)),
        out_shape=jax.ShapeDtypeStruct((nqd, s), BF16),
        scratch_shapes=[pltpu.VMEM((ATT_GROUPS, nb, nq), F32),
                        pltpu.VMEM((ATT_GROUPS, PS_PAD + nc, Q_BLK), F32),
                        pltpu.VMEM((ATT_GROUPS, nc, nq), F32),
                        *[pltpu.VMEM((SEL_CHUNK, nq), BF16) for _ in range(2 * ATT_GROUPS)],
                        *[pltpu.VMEM((SEL_CHUNK, nq), BF16) for _ in range(2 * ATT_GROUPS)]],
        compiler_params=_params("arbitrary", "arbitrary"),
        name="nsa_attention",
    )(q_t, gate_t, cmp_mask, win_mask, kc, vct, ks, vst, kw, vwt)


def _outproj_kernel(ot_ref, x_ref, vec_ref, w_ref, o_ref):
    vec = vec_ref[...]
    y = _dot_tn(ot_ref[...], w_ref[...])
    o_ref[...] = _post_residual(x_ref[...], y, vec[3:4], vec[4:5])


def _outproj(o_t, x, vec, w_out):
    s, d = x.shape
    nqd = o_t.shape[0]
    return pl.pallas_call(
        _outproj_kernel,
        grid=(s // ROW_TILE,),
        in_specs=[pl.BlockSpec((nqd, ROW_TILE), lambda i: (0, i)),
                  pl.BlockSpec((ROW_TILE, d), lambda i: (i, 0)),
                  _resident((SUBLANES, d), lambda i: (0, 0)),
                  _resident((nqd, d), lambda i: (0, 0))],
        out_specs=pl.BlockSpec((ROW_TILE, d), lambda i: (i, 0)),
        out_shape=jax.ShapeDtypeStruct((s, d), F32),
        compiler_params=_params("arbitrary"),
        name="nsa_outproj",
    )(o_t, x, vec, w_out)


def _vec(pre_g, mod, post_g):
    d = pre_g.shape[0]
    shift, scale, gate = mod[:d], mod[d:2 * d], mod[2 * d:3 * d]
    rows = [pre_g, scale, shift, post_g, gate]
    return jnp.stack(rows + [jnp.zeros((d,), F32)] * (SUBLANES - len(rows)))


def kernel(x, c, ada_w, ada_b, norm_pre_g, norm_post_g, a_w_in, a_ln_g, a_ln_b, a_ws, a_bs, a_w_out, kv_norm_g, kv_ada_w, kv_ada_b, w_kv, cmp_pos, cmp_w1, cmp_b1, cmp_w2, cmp_b2, b_w_in, b_w_out, ffn_w_gate, ffn_w_up, ffn_w_down, moe_router, moe_router_b, moe_w_gate, moe_w_up, moe_w_down):
    batch, s, d = x.shape
    assert batch == 1 and s % SEL_CHUNK == 0 and s >= WINDOW + Q_BLK
    depth = ada_w.shape[0]
    n_a = depth // 2
    xs = x.reshape(s, d)

    mods = _ada(c, ada_w.reshape(depth * 2, d, 3 * d), ada_b.reshape(depth * 2, 3 * d)).reshape(depth, 2, 3 * d)
    kv_mod = _ada(c, kv_ada_w.reshape(1, d, 2 * d), kv_ada_b.reshape(1, 2 * d))[0]
    shared = None
    n_moe, _, _, f_moe = moe_w_gate.shape
    moe_wg = moe_w_gate.astype(BF16).reshape(n_moe * N_EXPERTS, d, f_moe)
    moe_wu = moe_w_up.astype(BF16).reshape(n_moe * N_EXPERTS, d, f_moe)
    moe_wd = moe_w_down.astype(BF16).reshape(n_moe * N_EXPERTS, f_moe, d)

    for layer in range(depth):
        vec = _vec(norm_pre_g[layer, 0], mods[layer, 0], norm_post_g[layer, 0])
        if layer < n_a:
            xs = _gmlp_layer(xs, vec, a_w_in[layer].astype(BF16),
                             jnp.stack([a_ln_g[layer], a_ln_b[layer]]),
                             a_ws[layer], a_bs[layer].T, a_w_out[layer].astype(BF16))
        else:
            if shared is None:
                kv_vec = _vec(kv_norm_g, jnp.concatenate([kv_mod, jnp.zeros((d,), F32)]), jnp.zeros((d,), F32))
                gd = N_KV * HEAD_DIM
                w_vt = jnp.concatenate([w_kv[:, 3 * gd:4 * gd], w_kv[:, 5 * gd:6 * gd]], axis=1).T
                raw, ks, kw, vst, vwt = _kvproj(xs, kv_vec, w_kv.astype(BF16), w_vt.astype(BF16))
                nc = s // D_CMP
                width = D_CMP * HEAD_DIM
                hid = cmp_w1.shape[-1]
                w2p = jnp.pad(cmp_w2, ((0, 0), (0, 0), (0, LANES - HEAD_DIM))).astype(BF16)
                b2p = jnp.pad(cmp_b2, ((0, 0), (0, LANES - HEAD_DIM))).reshape(2, 1, LANES)
                cmp_n, cmp_t = _compress(raw.reshape(2, N_KV, nc, width),
                                         cmp_pos.reshape(2, 2, 1, width),
                                         cmp_w1.reshape(2, 2, width, hid).astype(BF16),
                                         cmp_b1.reshape(2, 1, hid), w2p, b2p)
                shared = (cmp_n[0], cmp_t[1], ks, vst, kw, vwt)
            i = layer - n_a
            nq = N_HEADS * HEAD_DIM
            w_t = b_w_in[i].T.astype(BF16)
            q_t, gate_t = _qproj(xs, vec, w_t)
            kc, vct, ks, vst, kw, vwt = shared
            o_t = _attention(q_t, gate_t, kc, vct, ks, vst, kw, vwt)
            xs = _outproj(o_t, xs, vec, b_w_out[i].astype(BF16))

        vec = _vec(norm_pre_g[layer, 1], mods[layer, 1], norm_post_g[layer, 1])
        j = layer // 2
        if layer % 2 == 0:
            xs = _swiglu_layer(xs, vec, ffn_w_gate[j].astype(BF16), ffn_w_up[j].astype(BF16),
                               ffn_w_down[j].astype(BF16))
        else:
            xs = _moe_layer(xs, vec, moe_router[j].T, moe_router_b[j].reshape(N_EXPERTS, 1),
                            moe_wg, moe_wu, moe_wd, j * N_EXPERTS)
    return xs.reshape(batch, s, d)
```

```python
import jax
import jax.numpy as jnp
from jax import lax
from jax.experimental import pallas as pl
from jax.experimental.pallas import tpu as pltpu

F32 = jnp.float32
BF16 = jnp.bfloat16
I32 = jnp.int32

EPS = 1e-6
NEG = -1e30

LANES = 128
SUBLANES = 8
VMEM_LIMIT_BYTES = 56 * 1024 * 1024

CHUNK = 128
A_GROUPS = 8
N_HEADS = 16
N_KV = 4
HPG = N_HEADS // N_KV
HEAD_DIM = 64
L_CMP = 32
D_CMP = 16
L_SEL = 64
SEL_SHIFT = 6
N_SELECT = 16
N_FORCED = 3
WINDOW = 512
Q_BLK = 128
N_EXPERTS = 8
MOE_BLOCK = 256

ROW_TILE = 512
TOK_TILE = 256
DISP_TILE = 512
SEL_CHUNK = 512
SEL_BLKS = SEL_CHUNK // L_SEL
CMP_CHUNK = 128
ATT_GROUPS = 2
GATE_ROWS = 16
PS_PAD = 8
V_ROWS = 72
F_CHUNK = 512
WIN_PART_ROWS = 128
WIN_PARTS = TOK_TILE // WIN_PART_ROWS + 1
WIN_ROWS = WIN_PARTS * WIN_PART_ROWS
LOG2E = 1.4426950408889634


def _params(*sem):
    return pltpu.CompilerParams(dimension_semantics=sem, vmem_limit_bytes=VMEM_LIMIT_BYTES)


def _resident(shape, index_map):
    return pl.BlockSpec(shape, index_map, pipeline_mode=pl.Buffered(1))


def _split_bf16(a):
    hi = a.astype(BF16)
    lo = (a - hi.astype(F32)).astype(BF16)
    return hi, lo


def _dot(a, b):
    return jnp.dot(a, b, preferred_element_type=F32)


def _dot_nt(a, b):
    return lax.dot_general(a, b, (((1,), (1,)), ((), ())), preferred_element_type=F32)


def _dot_tn(a, b):
    return lax.dot_general(a, b, (((0,), (0,)), ((), ())), preferred_element_type=F32)


def _prenorm(x, g, scale, shift):
    ms = jnp.mean(x * x, axis=-1, keepdims=True)
    return (x * lax.rsqrt(ms + EPS) * g) * (1.0 + scale) + shift


def _post_residual(x, y, g, gate):
    ms = jnp.mean(y * y, axis=-1, keepdims=True)
    return x + gate * (y * lax.rsqrt(ms + EPS) * g)


def _ada_kernel(c_ref, w_ref, b_ref, o_ref):
    c = c_ref[...]
    c_act = jnp.broadcast_to(c * jax.nn.sigmoid(c), (SUBLANES, c.shape[1]))
    c_hi, c_lo = _split_bf16(c_act)
    w_hi, w_lo = _split_bf16(w_ref[...])
    m = _dot(c_hi, w_hi) + (_dot(c_hi, w_lo) + _dot(c_lo, w_hi))
    o_ref[...] = m[0:1] + b_ref[...]


def _ada(c, w, b):
    n, d, nn = w.shape
    out = pl.pallas_call(
        _ada_kernel,
        grid=(n, nn // d),
        in_specs=[pl.BlockSpec((1, d), lambda i, j: (0, 0)),
                  pl.BlockSpec((None, d, d), lambda i, j: (i, 0, j)),
                  pl.BlockSpec((None, 1, d), lambda i, j: (i, 0, j))],
        out_specs=pl.BlockSpec((None, 1, d), lambda i, j: (i, 0, j)),
        out_shape=jax.ShapeDtypeStruct((n, 1, nn), F32),
        compiler_params=_params("arbitrary", "arbitrary"),
        name="ada",
    )(c, w, b.reshape(n, 1, nn))
    return out.reshape(n, nn)


def _gmlp_kernel(x_ref, vec_ref, win_ref, ln_ref, ws_ref, bst_ref, wout_ref, o_ref, gated_ref):
    x = x_ref[...]
    vec = vec_ref[...]
    h = _prenorm(x, vec[0:1], vec[1:2], vec[2:3]).astype(BF16)
    z = jax.nn.gelu(_dot(h, win_ref[...]))
    width = z.shape[1] // 2
    u = z[:, :width]
    v = z[:, width:]
    mu = jnp.mean(v, axis=-1, keepdims=True)
    vc = v - mu
    var = jnp.mean(vc * vc, axis=-1, keepdims=True)
    ln = ln_ref[...]
    vn = (vc * lax.rsqrt(var + EPS) * ln[0:1] + ln[1:2]).astype(BF16)
    causal = (lax.broadcasted_iota(I32, (CHUNK, CHUNK), 0)
              >= lax.broadcasted_iota(I32, (CHUNK, CHUNK), 1))
    gw = width // A_GROUPS
    bst = bst_ref[...]
    for g in range(A_GROUPS):
        wg = jnp.where(causal, ws_ref[g], 0.0).astype(BF16)
        for ck in range(x.shape[0] // CHUNK):
            rows = slice(ck * CHUNK, (ck + 1) * CHUNK)
            cols = slice(g * gw, (g + 1) * gw)
            mixed = _dot(wg, vn[rows, cols]) + bst[:, g:g + 1]
            gated_ref[rows, cols] = (u[rows, cols] * mixed).astype(BF16)
    y = _dot(gated_ref[...], wout_ref[...])
    o_ref[...] = _post_residual(x, y, vec[3:4], vec[4:5])


def _gmlp_layer(x, vec, w_in, ln, ws, bst, w_out):
    s, d = x.shape
    e2 = w_in.shape[1]
    return pl.pallas_call(
        _gmlp_kernel,
        grid=(s // ROW_TILE,),
        in_specs=[pl.BlockSpec((ROW_TILE, d), lambda i: (i, 0)),
                  _resident((SUBLANES, d), lambda i: (0, 0)),
                  _resident((d, e2), lambda i: (0, 0)),
                  _resident((2, e2 // 2), lambda i: (0, 0)),
                  _resident(ws.shape, lambda i: (0, 0, 0)),
                  _resident(bst.shape, lambda i: (0, 0)),
                  _resident((e2 // 2, d), lambda i: (0, 0))],
        out_specs=pl.BlockSpec((ROW_TILE, d), lambda i: (i, 0)),
        out_shape=jax.ShapeDtypeStruct((s, d), F32),
        scratch_shapes=[pltpu.VMEM((ROW_TILE, e2 // 2), BF16)],
        compiler_params=_params("arbitrary"),
        name="gmlp",
    )(x, vec, w_in, ln, ws, bst, w_out)


def _attn_residual(ot_ref, x_ref, avec_ref, wo_ref):
    avec = avec_ref[...]
    return _post_residual(x_ref[...], _dot_tn(ot_ref[...], wo_ref[...]), avec[3:4], avec[4:5])


def _attn_specs(o_t, d):
    nqd = o_t.shape[0]
    return [pl.BlockSpec((nqd, ROW_TILE), lambda i: (0, i)),
            pl.BlockSpec((ROW_TILE, d), lambda i: (i, 0)),
            _resident((SUBLANES, d), lambda i: (0, 0)),
            _resident((nqd, d), lambda i: (0, 0))]


def _swiglu_rows(x, vec, wg_ref, wu_ref, wd_ref):
    h = _prenorm(x, vec[0:1], vec[1:2], vec[2:3]).astype(BF16)
    g = _dot(h, wg_ref[...])
    a = (g * jax.nn.sigmoid(g) * _dot(h, wu_ref[...])).astype(BF16)
    y = _dot(a, wd_ref[...])
    return _post_residual(x, y, vec[3:4], vec[4:5])


def _swiglu_kernel(x_ref, vec_ref, wg_ref, wu_ref, wd_ref, o_ref):
    o_ref[...] = _swiglu_rows(x_ref[...], vec_ref[...], wg_ref, wu_ref, wd_ref)


def _attn_swiglu_kernel(ot_ref, x_ref, avec_ref, wo_ref, vec_ref, wg_ref, wu_ref, wd_ref, o_ref):
    x = _attn_residual(ot_ref, x_ref, avec_ref, wo_ref)
    o_ref[...] = _swiglu_rows(x, vec_ref[...], wg_ref, wu_ref, wd_ref)


def _swiglu_layer(x, vec, w_gate, w_up, w_down, attn=None):
    s, d = x.shape
    f = w_gate.shape[1]
    own_specs = [_resident((SUBLANES, d), lambda i: (0, 0)),
                 _resident((d, f), lambda i: (0, 0)),
                 _resident((d, f), lambda i: (0, 0)),
                 _resident((f, d), lambda i: (0, 0))]
    if attn is None:
        body, in_specs, args = _swiglu_kernel, [pl.BlockSpec((ROW_TILE, d), lambda i: (i, 0))], (x,)
    else:
        o_t, avec, w_out = attn
        body, in_specs, args = _attn_swiglu_kernel, _attn_specs(o_t, d), (o_t, x, avec, w_out)
    return pl.pallas_call(
        body,
        grid=(s // ROW_TILE,),
        in_specs=in_specs + own_specs,
        out_specs=pl.BlockSpec((ROW_TILE, d), lambda i: (i, 0)),
        out_shape=jax.ShapeDtypeStruct((s, d), F32),
        compiler_params=_params("arbitrary"),
        name="swiglu",
    )(*args, vec, w_gate, w_up, w_down)


def _router_kernel(x_ref, vec_ref, wrt_ref, br_ref, h_ref, idx_ref, gate_ref):
    _route_rows(x_ref[...], vec_ref[...], wrt_ref, br_ref, h_ref, idx_ref, gate_ref)


def _attn_router_kernel(ot_ref, x_ref, avec_ref, wo_ref, vec_ref, wrt_ref, br_ref, xo_ref, h_ref, idx_ref, gate_ref):
    x = _attn_residual(ot_ref, x_ref, avec_ref, wo_ref)
    xo_ref[...] = x
    _route_rows(x, vec_ref[...], wrt_ref, br_ref, h_ref, idx_ref, gate_ref)


def _route_rows(x, vec, wrt_ref, br_ref, h_ref, idx_ref, gate_ref):
    h = _prenorm(x, vec[0:1], vec[1:2], vec[2:3])
    h_ref[...] = h.astype(BF16)
    h_hi, h_lo = _split_bf16(h)
    w_hi, w_lo = _split_bf16(wrt_ref[...])
    logit = _dot_nt(w_hi, h_hi) + (_dot_nt(w_hi, h_lo) + _dot_nt(w_lo, h_hi)) + br_ref[...]
    ne = logit.shape[0]
    eidx = lax.broadcasted_iota(I32, logit.shape, 0)
    m1 = jnp.max(logit, axis=0, keepdims=True)
    i1 = jnp.min(jnp.where(logit == m1, eidx, ne), axis=0, keepdims=True)
    rest = jnp.where(eidx == i1, -jnp.inf, logit)
    m2 = jnp.max(rest, axis=0, keepdims=True)
    i2 = jnp.min(jnp.where(rest == m2, eidx, ne), axis=0, keepdims=True)
    e21 = jnp.exp(m2 - m1)
    g1 = 1.0 / (1.0 + e21)
    g2 = e21 * g1
    pad_i = jnp.zeros((SUBLANES - 2, i1.shape[1]), I32)
    pad_f = jnp.zeros((SUBLANES - 2, i1.shape[1]), F32)
    idx_ref[...] = jnp.concatenate([i1, i2, pad_i], axis=0)
    gate_ref[...] = jnp.concatenate([g1, g2, pad_f], axis=0)


def _router(x, vec, w_rt, b_r, attn=None):
    s, d = x.shape
    ne = w_rt.shape[0]
    own_specs = [_resident((SUBLANES, d), lambda i: (0, 0)),
                 _resident((ne, d), lambda i: (0, 0)),
                 _resident((ne, 1), lambda i: (0, 0))]
    out_specs = [pl.BlockSpec((ROW_TILE, d), lambda i: (i, 0)),
                 pl.BlockSpec((SUBLANES, ROW_TILE), lambda i: (0, i)),
                 pl.BlockSpec((SUBLANES, ROW_TILE), lambda i: (0, i))]
    out_shape = [jax.ShapeDtypeStruct((s, d), BF16),
                 jax.ShapeDtypeStruct((SUBLANES, s), I32),
                 jax.ShapeDtypeStruct((SUBLANES, s), F32)]
    if attn is None:
        h, idx, gates = pl.pallas_call(
            _router_kernel,
            grid=(s // ROW_TILE,),
            in_specs=[pl.BlockSpec((ROW_TILE, d), lambda i: (i, 0))] + own_specs,
            out_specs=out_specs,
            out_shape=out_shape,
            compiler_params=_params("arbitrary"),
            name="moe_router",
        )(x, vec, w_rt, b_r)
        return x, h, idx, gates
    o_t, avec, w_out = attn
    return pl.pallas_call(
        _attn_router_kernel,
        grid=(s // ROW_TILE,),
        in_specs=_attn_specs(o_t, d) + own_specs,
        out_specs=[pl.BlockSpec((ROW_TILE, d), lambda i: (i, 0))] + out_specs,
        out_shape=[jax.ShapeDtypeStruct((s, d), F32)] + out_shape,
        compiler_params=_params("arbitrary"),
        name="moe_router",
    )(o_t, x, avec, w_out, vec, w_rt, b_r)


def _rank_kernel(idx_ref, rank_ref, start_ref, count_ref, carry_ref):
    @pl.when(pl.program_id(0) == 0)
    def _():
        carry_ref[...] = jnp.zeros_like(carry_ref)

    idx = idx_ref[...]
    tt = idx.shape[1]
    eidx = lax.broadcasted_iota(I32, (N_EXPERTS, tt), 0)
    hit1 = eidx == idx[0:1]
    hit2 = eidx == idx[1:2]
    member = jnp.where(hit1 | hit2, 1.0, 0.0)
    before = (lax.broadcasted_iota(I32, (tt, tt), 0) < lax.broadcasted_iota(I32, (tt, tt), 1))
    carry = carry_ref[...]
    cum = _dot(member.astype(BF16), jnp.where(before, 1.0, 0.0).astype(BF16)) + carry[:, 0:1]
    r1 = jnp.sum(jnp.where(hit1, cum, 0.0), axis=0, keepdims=True)
    r2 = jnp.sum(jnp.where(hit2, cum, 0.0), axis=0, keepdims=True)
    rank_ref[...] = jnp.concatenate([r1, r2, jnp.zeros((SUBLANES - 2, tt), F32)], axis=0).astype(I32)
    tile_count = jnp.broadcast_to(jnp.sum(member, axis=1, keepdims=True), carry.shape)
    start_ref[...] = carry
    count_ref[...] = tile_count
    carry_ref[...] = carry + tile_count


def _ranks(idx):
    s = idx.shape[1]
    nt = s // TOK_TILE
    return pl.pallas_call(
        _rank_kernel,
        grid=(nt,),
        in_specs=[pl.BlockSpec((SUBLANES, TOK_TILE), lambda i: (0, i))],
        out_specs=[pl.BlockSpec((SUBLANES, TOK_TILE), lambda i: (0, i)),
                   pl.BlockSpec((None, N_EXPERTS, LANES), lambda i: (i, 0, 0)),
                   pl.BlockSpec((None, N_EXPERTS, LANES), lambda i: (i, 0, 0))],
        out_shape=[jax.ShapeDtypeStruct((SUBLANES, s), I32),
                   jax.ShapeDtypeStruct((nt, N_EXPERTS, LANES), F32),
                   jax.ShapeDtypeStruct((nt, N_EXPERTS, LANES), F32)],
        scratch_shapes=[pltpu.VMEM((N_EXPERTS, LANES), F32)],
        compiler_params=_params("arbitrary"),
        name="moe_ranks",
    )(idx)


def _match(idx, rank, expert, rows):
    r1 = jnp.where(idx[0:1] == expert, rank[0:1], -1)
    r2 = jnp.where(idx[1:2] == expert, rank[1:2], -1)
    return rows == r1, rows == r2


def _dispatch_kernel(be_ref, lb_ref, tlo_ref, thi_ref, idx_ref, rank_ref, gate_ref, h_ref, o_ref, rg_ref,
                     acc_ref, gacc_ref):
    b = pl.program_id(0)
    expert = be_ref[b]
    rows = lax.broadcasted_iota(I32, (MOE_BLOCK, DISP_TILE), 0) + lb_ref[b]
    acc_ref[...] = jnp.zeros_like(acc_ref)
    gacc_ref[...] = jnp.zeros_like(gacc_ref)

    def body(t, carry):
        off = pl.multiple_of(t * DISP_TILE, DISP_TILE)
        m1, m2 = _match(idx_ref[:, pl.ds(off, DISP_TILE)], rank_ref[:, pl.ds(off, DISP_TILE)], expert, rows)
        gates = gate_ref[:, pl.ds(off, DISP_TILE)]
        gacc_ref[...] += jnp.sum(jnp.where(m1, gates[0:1], 0.0) + jnp.where(m2, gates[1:2], 0.0),
                                 axis=1, keepdims=True)
        onehot = jnp.where(m1 | m2, 1.0, 0.0).astype(BF16)
        acc_ref[...] += _dot(onehot, h_ref[pl.ds(off, DISP_TILE), :])
        return carry

    lax.fori_loop(tlo_ref[b], thi_ref[b], body, 0)
    o_ref[...] = acc_ref[...].astype(BF16)
    rg_ref[...] = gacc_ref[...]


def _dispatch(blk_e, blk_lb, blk_tlo, blk_thi, idx, rank, gates, h):
    s, d = h.shape
    n_blk = blk_e.shape[0]
    grid_spec = pltpu.PrefetchScalarGridSpec(
        num_scalar_prefetch=4,
        grid=(n_blk,),
        in_specs=[_resident((SUBLANES, s), lambda b, *_: (0, 0)),
                  _resident((SUBLANES, s), lambda b, *_: (0, 0)),
                  _resident((SUBLANES, s), lambda b, *_: (0, 0)),
                  _resident((s, d), lambda b, *_: (0, 0))],
        out_specs=[pl.BlockSpec((MOE_BLOCK, d), lambda b, *_: (b, 0)),
                   pl.BlockSpec((MOE_BLOCK, 1), lambda b, *_: (b, 0))],
        scratch_shapes=[pltpu.VMEM((MOE_BLOCK, d), F32), pltpu.VMEM((MOE_BLOCK, 1), F32)],
    )
    return pl.pallas_call(
        _dispatch_kernel,
        grid_spec=grid_spec,
        out_shape=[jax.ShapeDtypeStruct((n_blk * MOE_BLOCK, d), BF16),
                   jax.ShapeDtypeStruct((n_blk * MOE_BLOCK, 1), F32)],
        compiler_params=_params("arbitrary"),
        name="moe_dispatch",
    )(blk_e, blk_lb, blk_tlo, blk_thi, idx, rank, gates, h)


def _expert_kernel(be_ref, used_ref, x_ref, rg_ref, wg_ref, wu_ref, wd_ref, o_ref):
    b = pl.program_id(0)

    @pl.when(b < used_ref[0])
    def _():
        x = x_ref[...]
        f = wg_ref.shape[1]
        y = jnp.zeros(o_ref.shape, F32)
        for c in range(f // F_CHUNK):
            cols = slice(c * F_CHUNK, (c + 1) * F_CHUNK)
            g = _dot(x, wg_ref[:, cols])
            a = (g * jax.nn.sigmoid(g) * _dot(x, wu_ref[:, cols])).astype(BF16)
            y = y + _dot(a, wd_ref[cols, :])
        o_ref[...] = (y * rg_ref[...]).astype(BF16)

    @pl.when(b >= used_ref[0])
    def _():
        o_ref[...] = jnp.zeros_like(o_ref)


def _experts(blk_e, n_used, x_buf, row_gate, w_gate, w_up, w_down):
    n_rows, d = x_buf.shape
    f = w_gate.shape[2]
    assert f % F_CHUNK == 0
    grid_spec = pltpu.PrefetchScalarGridSpec(
        num_scalar_prefetch=2,
        grid=(n_rows // MOE_BLOCK,),
        in_specs=[pl.BlockSpec((MOE_BLOCK, d), lambda b, be, nu: (b, 0)),
                  pl.BlockSpec((MOE_BLOCK, 1), lambda b, be, nu: (b, 0)),
                  pl.BlockSpec((None, d, f), lambda b, be, nu: (be[b], 0, 0)),
                  pl.BlockSpec((None, d, f), lambda b, be, nu: (be[b], 0, 0)),
                  pl.BlockSpec((None, f, d), lambda b, be, nu: (be[b], 0, 0))],
        out_specs=pl.BlockSpec((MOE_BLOCK, d), lambda b, be, nu: (b, 0)),
    )
    return pl.pallas_call(
        _expert_kernel,
        grid_spec=grid_spec,
        out_shape=jax.ShapeDtypeStruct((n_rows, d), BF16),
        compiler_params=_params("arbitrary"),
        name="moe_experts",
    )(blk_e, n_used, x_buf, row_gate, w_gate, w_up, w_down)


def _combine_kernel(win_ref, lb_ref, tail_ref, idx_ref, rank_ref, *refs):
    y_refs = refs[:N_EXPERTS * WIN_PARTS]
    x_ref, vec_ref, o_ref, acc_ref = refs[N_EXPERTS * WIN_PARTS:]
    t = pl.program_id(0)
    idx = idx_ref[...]
    rank = rank_ref[...]
    main_rows = (WIN_PARTS - 1) * WIN_PART_ROWS

    def gathered(e, first_part, n_parts):
        n_rows = n_parts * WIN_PART_ROWS
        rows = (lax.broadcasted_iota(I32, (n_rows, TOK_TILE), 0)
                + (lb_ref[t * N_EXPERTS + e] + first_part * WIN_PART_ROWS))
        m1, m2 = _match(idx, rank, e, rows)
        onehot = jnp.where(m1 | m2, 1.0, 0.0).astype(BF16)
        y = jnp.concatenate([y_refs[e * WIN_PARTS + first_part + k][...] for k in range(n_parts)], axis=0)
        return _dot_tn(onehot, y)

    acc = jnp.zeros(o_ref.shape, F32)
    for e in range(N_EXPERTS):
        acc = acc + gathered(e, 0, WIN_PARTS - 1)
    acc_ref[...] = acc
    for e in range(N_EXPERTS):
        @pl.when(tail_ref[t * N_EXPERTS + e] > 0)
        def _():
            acc_ref[...] += gathered(e, WIN_PARTS - 1, 1)
    vec = vec_ref[...]
    o_ref[...] = _post_residual(x_ref[...], acc_ref[...], vec[3:4], vec[4:5])


def _combine(win_start, win_lb, win_tail, idx, rank, y_buf, x, vec):
    s, d = x.shape

    def y_spec(e, k):
        return pl.BlockSpec((WIN_PART_ROWS, d), lambda t, ws, lb, tl: (ws[t * N_EXPERTS + e] + k, 0))

    grid_spec = pltpu.PrefetchScalarGridSpec(
        num_scalar_prefetch=3,
        grid=(s // TOK_TILE,),
        in_specs=[pl.BlockSpec((SUBLANES, TOK_TILE), lambda t, *_: (0, t)),
                  pl.BlockSpec((SUBLANES, TOK_TILE), lambda t, *_: (0, t))]
                 + [y_spec(e, k) for e in range(N_EXPERTS) for k in range(WIN_PARTS)]
                 + [pl.BlockSpec((TOK_TILE, d), lambda t, *_: (t, 0)),
                    _resident((SUBLANES, d), lambda t, *_: (0, 0))],
        out_specs=pl.BlockSpec((TOK_TILE, d), lambda t, *_: (t, 0)),
        scratch_shapes=[pltpu.VMEM((TOK_TILE, d), F32)],
    )
    return pl.pallas_call(
        _combine_kernel,
        grid_spec=grid_spec,
        out_shape=jax.ShapeDtypeStruct((s, d), F32),
        compiler_params=_params("arbitrary"),
        name="moe_combine",
    )(win_start, win_lb, win_tail, idx, rank, *([y_buf] * (N_EXPERTS * WIN_PARTS)), x, vec)


def _moe_layer(x, vec, w_rt, b_r, w_gate, w_up, w_down, w_first, attn=None):
    s, d = x.shape
    nt = s // TOK_TILE
    x, h, idx, gates = _router(x, vec, w_rt, b_r, attn)
    rank, tile_start, tile_count = _ranks(idx)

    tile_start = tile_start[:, :, 0].astype(I32)
    tile_count = tile_count[:, :, 0].astype(I32)
    tile_end = tile_start + tile_count
    counts = tile_end[-1]
    padded = (counts + MOE_BLOCK - 1) // MOE_BLOCK * MOE_BLOCK
    pend = jnp.cumsum(padded)
    pstart = pend - padded
    n_rows = -(-(2 * s) // MOE_BLOCK) * MOE_BLOCK + N_EXPERTS * MOE_BLOCK
    n_blk = n_rows // MOE_BLOCK
    blk_row = jnp.arange(n_blk, dtype=I32) * MOE_BLOCK
    blk_e = jnp.minimum(jnp.sum(blk_row[:, None] >= pend[None, :], axis=1), N_EXPERTS - 1).astype(I32)
    blk_lb = blk_row - pstart[blk_e]
    per = DISP_TILE // TOK_TILE
    te = tile_end[per - 1::per][:, blk_e]
    ts = tile_start[::per][:, blk_e]
    blk_tlo = jnp.sum(te <= blk_lb[None, :], axis=0).astype(I32)
    blk_thi = jnp.sum(ts < (blk_lb + MOE_BLOCK)[None, :], axis=0).astype(I32)
    n_used = (pend[-1] // MOE_BLOCK).astype(I32).reshape(1)

    x_buf, row_gate = _dispatch(blk_e, blk_lb.astype(I32), blk_tlo, blk_thi, idx, rank, gates, h)
    y_buf = _experts(blk_e + w_first, n_used, x_buf, row_gate, w_gate, w_up, w_down)

    win_start = jnp.minimum((pstart[None, :] + tile_start) // WIN_PART_ROWS, n_rows // WIN_PART_ROWS - WIN_PARTS)
    win_lb = win_start * WIN_PART_ROWS - pstart[None, :]
    win_tail = tile_end - win_lb > (WIN_PARTS - 1) * WIN_PART_ROWS
    return _combine(win_start.reshape(-1).astype(I32), win_lb.reshape(-1).astype(I32),
                    win_tail.reshape(-1).astype(I32), idx, rank, y_buf, x, vec)


def _kvproj_kernel(x_ref, vec_ref, wkv_ref, wvt_ref, raw_ref, ks_ref, kw_ref, vst_ref, vwt_ref):
    x = x_ref[...]
    vec = vec_ref[...]
    h = _prenorm(x, vec[0:1], vec[1:2], vec[2:3]).astype(BF16)
    kv = _dot(h, wkv_ref[...])
    vt = _dot_nt(wvt_ref[...], h)
    tm = x.shape[0]
    gd = N_KV * HEAD_DIM
    key_blk = jnp.right_shift(pl.program_id(0) * tm + lax.broadcasted_iota(I32, (tm, SEL_BLKS), 0), SEL_SHIFT)
    ind = jnp.where((key_blk & (SEL_BLKS - 1)) == lax.broadcasted_iota(I32, (tm, SEL_BLKS), 1), 1.0, 0.0)
    pad_s = jnp.zeros((tm, LANES - HEAD_DIM - SEL_BLKS), F32)
    pad_w = jnp.zeros((tm, LANES - HEAD_DIM), F32)
    ones_row = jnp.concatenate([jnp.ones((1, tm), F32), jnp.zeros((V_ROWS - HEAD_DIM - 1, tm), F32)], axis=0)
    for g in range(N_KV):
        c = g * HEAD_DIM
        raw_ref[0, g] = kv[:, c:c + HEAD_DIM]
        raw_ref[1, g] = kv[:, gd + c:gd + c + HEAD_DIM]
        ks = kv[:, 2 * gd + c:2 * gd + c + HEAD_DIM]
        kw = kv[:, 4 * gd + c:4 * gd + c + HEAD_DIM]
        ks_ref[g] = jnp.concatenate([ks, ind, pad_s], axis=1).astype(BF16)
        kw_ref[g] = jnp.concatenate([kw, pad_w], axis=1).astype(BF16)
        vst_ref[g] = jnp.concatenate([vt[c:c + HEAD_DIM], ones_row], axis=0).astype(BF16)
        vwt_ref[g] = jnp.concatenate([vt[gd + c:gd + c + HEAD_DIM], ones_row], axis=0).astype(BF16)


def _kvproj(x, vec, w_kv, w_vt):
    s, d = x.shape
    nkv = w_kv.shape[1]
    return pl.pallas_call(
        _kvproj_kernel,
        grid=(s // ROW_TILE,),
        in_specs=[pl.BlockSpec((ROW_TILE, d), lambda i: (i, 0)),
                  _resident((SUBLANES, d), lambda i: (0, 0)),
                  _resident((d, nkv), lambda i: (0, 0)),
                  _resident(w_vt.shape, lambda i: (0, 0))],
        out_specs=[pl.BlockSpec((2, N_KV, ROW_TILE, HEAD_DIM), lambda i: (0, 0, i, 0)),
                   pl.BlockSpec((N_KV, ROW_TILE, LANES), lambda i: (0, i, 0)),
                   pl.BlockSpec((N_KV, ROW_TILE, LANES), lambda i: (0, i, 0)),
                   pl.BlockSpec((N_KV, V_ROWS, ROW_TILE), lambda i: (0, 0, i)),
                   pl.BlockSpec((N_KV, V_ROWS, ROW_TILE), lambda i: (0, 0, i))],
        out_shape=[jax.ShapeDtypeStruct((2, N_KV, s, HEAD_DIM), F32),
                   jax.ShapeDtypeStruct((N_KV, s, LANES), BF16),
                   jax.ShapeDtypeStruct((N_KV, s, LANES), BF16),
                   jax.ShapeDtypeStruct((N_KV, V_ROWS, s), BF16),
                   jax.ShapeDtypeStruct((N_KV, V_ROWS, s), BF16)],
        compiler_params=_params("arbitrary"),
        name="nsa_kvproj",
    )(x, vec, w_kv, w_vt)


def _compress_kernel(raw_ref, pos_ref, w1_ref, b1_ref, w2_ref, b2_ref, n_ref, t_ref):
    raw = raw_ref[...]
    nc = raw.shape[0]
    first = _dot((raw + pos_ref[0]).astype(BF16), w1_ref[0])
    second = _dot((raw + pos_ref[1]).astype(BF16), w1_ref[1])
    hid = jax.nn.gelu(first + pltpu.roll(second, nc - 1, 0) + b1_ref[...])
    out = _dot(hid.astype(BF16), w2_ref[...]) + b2_ref[...]
    n_ref[...] = out.astype(BF16)
    row = lax.broadcasted_iota(I32, (LANES, nc), 0)
    t_ref[...] = jnp.where(row == HEAD_DIM, 1.0, out.T).astype(BF16)


def _compress(raw, pos, w1, b1, w2, b2):
    _, g, nc, width = raw.shape
    hid = w1.shape[-1]
    return pl.pallas_call(
        _compress_kernel,
        grid=(2, g),
        in_specs=[pl.BlockSpec((None, None, nc, width), lambda j, k: (j, k, 0, 0)),
                  pl.BlockSpec((None, 2, 1, width), lambda j, k: (j, 0, 0, 0)),
                  pl.BlockSpec((None, 2, width, hid), lambda j, k: (j, 0, 0, 0)),
                  pl.BlockSpec((None, 1, hid), lambda j, k: (j, 0, 0)),
                  pl.BlockSpec((None, hid, LANES), lambda j, k: (j, 0, 0)),
                  pl.BlockSpec((None, 1, LANES), lambda j, k: (j, 0, 0))],
        out_specs=[pl.BlockSpec((None, None, nc, LANES), lambda j, k: (j, k, 0, 0)),
                   pl.BlockSpec((None, None, LANES, nc), lambda j, k: (j, k, 0, 0))],
        out_shape=[jax.ShapeDtypeStruct((2, g, nc, LANES), BF16),
                   jax.ShapeDtypeStruct((2, g, LANES, nc), BF16)],
        compiler_params=_params("arbitrary", "arbitrary"),
        name="nsa_compress",
    )(raw, pos, w1, b1, w2, b2)


def _qproj_kernel(x_ref, vec_ref, wt_ref, q_ref, gate_ref):
    x = x_ref[...]
    vec = vec_ref[...]
    h = _prenorm(x, vec[0:1], vec[1:2], vec[2:3]).astype(BF16)
    pt = _dot_nt(wt_ref[...], h)
    nq = N_HEADS * HEAD_DIM
    q_ref[...] = (pt[:nq] * (HEAD_DIM ** -0.5 * LOG2E)).astype(BF16)
    gates = jax.nn.sigmoid(pt[nq:nq + 3 * N_HEADS])
    per = 3 * HPG
    pad = jnp.zeros((GATE_ROWS - per, x.shape[0]), F32)
    for g in range(N_KV):
        gate_ref[g] = jnp.concatenate([gates[g * per:(g + 1) * per], pad], axis=0)


def _qproj(x, vec, w_t):
    s, d = x.shape
    return pl.pallas_call(
        _qproj_kernel,
        grid=(s // ROW_TILE,),
        in_specs=[pl.BlockSpec((ROW_TILE, d), lambda i: (i, 0)),
                  _resident((SUBLANES, d), lambda i: (0, 0)),
                  _resident(w_t.shape, lambda i: (0, 0))],
        out_specs=[pl.BlockSpec((N_HEADS * HEAD_DIM, ROW_TILE), lambda i: (0, i)),
                   pl.BlockSpec((N_KV, GATE_ROWS, ROW_TILE), lambda i: (0, 0, i))],
        out_shape=[jax.ShapeDtypeStruct((N_HEADS * HEAD_DIM, s), BF16),
                   jax.ShapeDtypeStruct((N_KV, GATE_ROWS, s), F32)],
        compiler_params=_params("arbitrary"),
        name="nsa_qproj",
    )(x, vec, w_t)


def _attn_kernel(q_ref, gate_ref, cmask_ref, wmask_ref, kc_ref, vct_ref, ks_ref, vst_ref, kw_ref, vwt_ref,
                 o_ref, bias_ref, ps_ref, sc_ref, *sp_refs):
    s_refs = [[sp_refs[2 * g + k] for k in range(2)] for g in range(ATT_GROUPS)]
    p_refs = [[sp_refs[2 * ATT_GROUPS + 2 * g + k] for k in range(2)] for g in range(ATT_GROUPS)]
    i = pl.program_id(1)
    nq = HPG * Q_BLK
    nc = kc_ref.shape[1]
    nb = bias_ref.shape[1]
    rows = HPG * HEAD_DIM
    t1 = i * Q_BLK + lax.broadcasted_iota(I32, (1, Q_BLK), 1)
    t4 = jnp.concatenate([t1] * HPG, axis=1)
    j_io = lax.broadcasted_iota(I32, (nb, Q_BLK), 0)
    jt = jnp.right_shift(t1, SEL_SHIFT)
    forced = (j_io == 0) | (j_io == jt) | (j_io == jt - 1)
    cand_off = jnp.where((j_io * L_SEL <= t1) & jnp.logical_not(forced), 0.0, NEG)
    past_off = jnp.where(j_io * L_SEL < i * Q_BLK, 0.0, NEG)
    doff = pl.multiple_of(i * Q_BLK, Q_BLK)
    cmask_off = nc - i * (Q_BLK // D_CMP)
    span = WINDOW + Q_BLK
    wstart = pl.multiple_of(jnp.maximum(i * Q_BLK - WINDOW, 0), Q_BLK)
    win_mask = wmask_ref[pl.ds(pl.multiple_of(WINDOW - jnp.minimum(i * Q_BLK, WINDOW), Q_BLK), span), :]
    win_mask = jnp.concatenate([win_mask] * HPG, axis=1)
    diag_mask = jnp.concatenate([wmask_ref[WINDOW:WINDOW + Q_BLK, :]] * HPG, axis=1)
    vpad = jnp.zeros((LANES - HEAD_DIM - 2 * SEL_BLKS, nq), BF16)
    bpad = jnp.zeros((SEL_BLKS, nq), F32)

    def aligned(x, m):
        return x if isinstance(x, int) else pl.multiple_of(x, m)

    groups = range(ATT_GROUPS)
    q4s, q_plains = [], []
    for g in groups:
        qb = q_ref[g * rows:(g + 1) * rows, :]
        q4s.append(jnp.concatenate([qb[h * HEAD_DIM:(h + 1) * HEAD_DIM] for h in range(HPG)], axis=1))
        q_plains.append(jnp.concatenate([q4s[g], jnp.zeros((LANES - HEAD_DIM, nq), BF16)], axis=0))

    def compressed(n_rows):
        chunks = [slice(r0, r0 + CMP_CHUNK) for r0 in range(0, n_rows, CMP_CHUNK)]
        ms = [jnp.full((1, nq), NEG, F32) for _ in groups]
        for rs in chunks:
            mask = cmask_ref[pl.ds(pl.multiple_of(cmask_off + rs.start, 8), CMP_CHUNK), :]
            mask = jnp.concatenate([mask] * HPG, axis=1)
            for g in groups:
                s = _dot(kc_ref[g, rs, :], q_plains[g]) + mask
                sc_ref[g, rs, :] = s
                ms[g] = jnp.maximum(ms[g], jnp.max(s, axis=0, keepdims=True))
        ls = [jnp.zeros((1, nq), F32) for _ in groups]
        for rs in chunks:
            for g in groups:
                e = jnp.exp2(sc_ref[g, rs, :] - ms[g])
                sc_ref[g, rs, :] = e
                ls[g] = ls[g] + jnp.sum(e, axis=0, keepdims=True)
        rls = [jnp.where(t4 >= L_CMP - 1, 1.0 / ls[g], 0.0) for g in groups]
        os_ = [jnp.zeros((HEAD_DIM, nq), F32) for _ in groups]
        for g in groups:
            ps_ref[g, 0:PS_PAD, :] = jnp.zeros((PS_PAD, Q_BLK), F32)
            if n_rows < nc:
                ps_ref[g, PS_PAD + n_rows:PS_PAD + nc, :] = jnp.zeros((nc - n_rows, Q_BLK), F32)
        for rs in chunks:
            for g in groups:
                pc = sc_ref[g, rs, :] * rls[g]
                os_[g] = os_[g] + _dot(vct_ref[g, 0:HEAD_DIM, rs], pc.astype(BF16))
                psum = pc[:, 0:Q_BLK]
                for h in range(1, HPG):
                    psum = psum + pc[:, h * Q_BLK:(h + 1) * Q_BLK]
                ps_ref[g, PS_PAD + rs.start:PS_PAD + rs.stop, :] = psum
        ratio = L_SEL // D_CMP
        imps = []
        for g in groups:
            imp = jnp.zeros((nb, Q_BLK), F32)
            for k in range(1 - L_CMP // D_CMP, ratio):
                imp = imp + ps_ref[g, pl.ds(PS_PAD + k, nb, stride=ratio), :]
            imps.append(imp)
        return tuple(os_) + tuple(imps)

    if nc % (2 * CMP_CHUNK) == 0:
        last_complete = ((i + 1) * Q_BLK - L_CMP) // D_CMP
        cmp_out = lax.cond(last_complete < nc // 2, lambda: compressed(nc // 2), lambda: compressed(nc))
    else:
        cmp_out = compressed(nc)
    o_cmps, imps = cmp_out[:ATT_GROUPS], cmp_out[ATT_GROUPS:]

    def select(imps):
        w = jnp.concatenate([imp + cand_off for imp in imps], axis=1)
        j_all = lax.broadcasted_iota(I32, w.shape, 0).astype(F32)
        for _ in range(N_SELECT - N_FORCED):
            m = jnp.max(w, axis=0, keepdims=True)
            first = jnp.min(jnp.where(w == m, j_all, float(nb)), axis=0, keepdims=True)
            first = jnp.where(m > 0.5 * NEG, first, -1.0)
            w = jnp.where(j_all == first, NEG, w)
        for g in range(ATT_GROUPS):
            bias1 = jnp.where(w[:, g * Q_BLK:(g + 1) * Q_BLK] < 0.5 * NEG, past_off, NEG)
            bias_ref[g] = jnp.concatenate([bias1] * HPG, axis=1)

    def local():
        sds = [_dot(ks_ref[g, pl.ds(doff, Q_BLK), :], q_plains[g]) + diag_mask for g in groups]
        sws = [_dot(kw_ref[g, pl.ds(wstart, span), :], q_plains[g]) + win_mask for g in groups]
        out = []
        for g in groups:
            m0 = jnp.max(sds[g], axis=0, keepdims=True)
            acc0 = _dot(vst_ref[g, :, pl.ds(doff, Q_BLK)], jnp.exp2(sds[g] - m0).astype(BF16))
            pw = jnp.exp2(sws[g] - jnp.max(sws[g], axis=0, keepdims=True)).astype(BF16)
            acc_w = _dot(vwt_ref[g, :, pl.ds(wstart, span)], pw)
            out.append((acc_w[0:HEAD_DIM] / acc_w[HEAD_DIM:HEAD_DIM + 1], m0, acc0))
        return out

    def query_operand(g, q4, c):
        brow = bias_ref[g, pl.ds(aligned(c * SEL_BLKS, SEL_BLKS), SEL_BLKS), :]
        b16 = jnp.concatenate([brow, bpad], axis=0).astype(BF16)
        return jnp.concatenate([q4, b16, vpad], axis=0)

    def phase(g, q4, c, slot, state, do_scores=True, do_softmax=True, do_values=True):
        m_run, alpha, acc, cmax = state
        other = 1 - slot
        if do_values:
            voff = aligned(c * SEL_CHUNK, SEL_CHUNK)
            acc = alpha * acc + _dot(vst_ref[g, :, pl.ds(voff, SEL_CHUNK)], p_refs[g][slot][...])
        if do_scores:
            koff = aligned((c + 2) * SEL_CHUNK, SEL_CHUNK)
            s = _dot(ks_ref[g, pl.ds(koff, SEL_CHUNK), :], query_operand(g, q4, c + 2))
            s_refs[g][slot][...] = s.astype(BF16)
            new_max = jnp.max(s, axis=0, keepdims=True)
        if do_softmax:
            m_new = jnp.maximum(m_run, cmax[other])
            p_refs[g][other][...] = jnp.exp2(s_refs[g][other][...] - m_new.astype(BF16))
            m_run, alpha = m_new, jnp.exp2(m_run - m_new)
        if do_scores:
            cmax = (new_max, cmax[1]) if slot == 0 else (cmax[0], new_max)
        return m_run, alpha, acc, cmax

    select(imps)
    heads = [(q4s[g], o_cmps[g]) + loc for g, loc in enumerate(local())]

    def pair(k, carries):
        out = []
        for g in range(ATT_GROUPS):
            q4 = heads[g][0]
            state = phase(g, q4, 2 * k, 0, carries[g])
            out.append(phase(g, q4, 2 * k + 1, 1, state))
        return tuple(out)

    n_pairs = jnp.maximum((i * Q_BLK + 2 * SEL_CHUNK - 1) // (2 * SEL_CHUNK), 1)
    init = []
    for g in range(ATT_GROUPS):
        q4, _, _, m0, acc0 = heads[g]
        neg_row = jnp.full((1, nq), NEG, F32)
        state = (m0, jnp.ones((1, nq), F32), acc0, (neg_row, neg_row))
        state = phase(g, q4, -2, 0, state, do_softmax=False, do_values=False)
        init.append(phase(g, q4, -1, 1, state, do_values=False))
    carries = lax.fori_loop(0, n_pairs - 1, pair, tuple(init))
    last = 2 * (n_pairs - 1)
    for g in range(ATT_GROUPS):
        q4, o_cmp, o_win, _, _ = heads[g]
        state = phase(g, q4, last, 0, carries[g], do_scores=False)
        _, _, acc_s, _ = phase(g, q4, last + 1, 1, state, do_scores=False, do_softmax=False)
        o_sel = acc_s[0:HEAD_DIM] / acc_s[HEAD_DIM:HEAD_DIM + 1]
        gates = gate_ref[g]
        outs = []
        for h in range(HPG):
            cols = slice(h * Q_BLK, (h + 1) * Q_BLK)
            outs.append(o_cmp[:, cols] * gates[3 * h:3 * h + 1]
                        + o_sel[:, cols] * gates[3 * h + 1:3 * h + 2]
                        + o_win[:, cols] * gates[3 * h + 2:3 * h + 3])
        o_ref[g * rows:(g + 1) * rows, :] = jnp.concatenate(outs, axis=0).astype(BF16)


def _attention(q_t, gate_t, kc, vct, ks, vst, kw, vwt):
    nqd, s = q_t.shape
    nc = kc.shape[1]
    nb = s // L_SEL
    nq = HPG * Q_BLK
    rows = ATT_GROUPS * HPG * HEAD_DIM
    assert (s // SEL_CHUNK) % 2 == 0 and s >= WINDOW + Q_BLK and N_KV % ATT_GROUPS == 0 and nb >= N_SELECT
    qq = jnp.arange(Q_BLK)[None, :]
    rc = jnp.arange(2 * nc)[:, None]
    cmp_mask = jnp.where(D_CMP * (rc - nc) + L_CMP - 1 <= qq, 0.0, NEG).astype(F32)
    rw = jnp.arange(2 * WINDOW + Q_BLK)[:, None]
    win_mask = jnp.where((qq < rw) & (rw <= qq + WINDOW), 0.0, NEG).astype(F32)
    return pl.pallas_call(
        _attn_kernel,
        grid=(N_KV // ATT_GROUPS, s // Q_BLK),
        in_specs=[pl.BlockSpec((rows, Q_BLK), lambda g, i: (g, i)),
                  pl.BlockSpec((ATT_GROUPS, GATE_ROWS, Q_BLK), lambda g, i: (g, 0, i)),
                  _resident(cmp_mask.shape, lambda g, i: (0, 0)),
                  _resident(win_mask.shape, lambda g, i: (0, 0)),
                  _resident((ATT_GROUPS, nc, LANES), lambda g, i: (g, 0, 0)),
                  _resident((ATT_GROUPS, LANES, nc), lambda g, i: (g, 0, 0)),
                  _resident((ATT_GROUPS, s, LANES), lambda g, i: (g, 0, 0)),
                  _resident((ATT_GROUPS, V_ROWS, s), lambda g, i: (g, 0, 0)),
                  _resident((ATT_GROUPS, s, LANES), lambda g, i: (g, 0, 0)),
                  _resident((ATT_GROUPS, V_ROWS, s), lambda g, i: (g, 0, 0))],
        out_specs=pl.BlockSpec((rows, Q_BLK), lambda g, i: (g, i)),
        out_shape=jax.ShapeDtypeStruct((nqd, s), BF16),
        scratch_shapes=[pltpu.VMEM((ATT_GROUPS, nb, nq), F32),
                        pltpu.VMEM((ATT_GROUPS, PS_PAD + nc, Q_BLK), F32),
                        pltpu.VMEM((ATT_GROUPS, nc, nq), F32),
                        *[pltpu.VMEM((SEL_CHUNK, nq), BF16) for _ in range(2 * ATT_GROUPS)],
                        *[pltpu.VMEM((SEL_CHUNK, nq), BF16) for _ in range(2 * ATT_GROUPS)]],
        compiler_params=_params("arbitrary", "arbitrary"),
        name="nsa_attention",
    )(q_t, gate_t, cmp_mask, win_mask, kc, vct, ks, vst, kw, vwt)


def _vec(pre_g, mod, post_g):
    d = pre_g.shape[0]
    shift, scale, gate = mod[:d], mod[d:2 * d], mod[2 * d:3 * d]
    rows = [pre_g, scale, shift, post_g, gate]
    return jnp.stack(rows + [jnp.zeros((d,), F32)] * (SUBLANES - len(rows)))


def kernel(x, c, ada_w, ada_b, norm_pre_g, norm_post_g, a_w_in, a_ln_g, a_ln_b, a_ws, a_bs, a_w_out, kv_norm_g, kv_ada_w, kv_ada_b, w_kv, cmp_pos, cmp_w1, cmp_b1, cmp_w2, cmp_b2, b_w_in, b_w_out, ffn_w_gate, ffn_w_up, ffn_w_down, moe_router, moe_router_b, moe_w_gate, moe_w_up, moe_w_down):
    batch, s, d = x.shape
    assert batch == 1 and s % SEL_CHUNK == 0 and s >= WINDOW + Q_BLK
    depth = ada_w.shape[0]
    n_a = depth // 2
    xs = x.reshape(s, d)

    mods = _ada(c, ada_w.reshape(depth * 2, d, 3 * d), ada_b.reshape(depth * 2, 3 * d)).reshape(depth, 2, 3 * d)
    kv_mod = _ada(c, kv_ada_w.reshape(1, d, 2 * d), kv_ada_b.reshape(1, 2 * d))[0]
    shared = None
    n_moe, _, _, f_moe = moe_w_gate.shape
    moe_wg = moe_w_gate.astype(BF16).reshape(n_moe * N_EXPERTS, d, f_moe)
    moe_wu = moe_w_up.astype(BF16).reshape(n_moe * N_EXPERTS, d, f_moe)
    moe_wd = moe_w_down.astype(BF16).reshape(n_moe * N_EXPERTS, f_moe, d)

    for layer in range(depth):
        vec = _vec(norm_pre_g[layer, 0], mods[layer, 0], norm_post_g[layer, 0])
        attn = None
        if layer < n_a:
            xs = _gmlp_layer(xs, vec, a_w_in[layer].astype(BF16),
                             jnp.stack([a_ln_g[layer], a_ln_b[layer]]),
                             a_ws[layer], a_bs[layer].T, a_w_out[layer].astype(BF16))
        else:
            if shared is None:
                kv_vec = _vec(kv_norm_g, jnp.concatenate([kv_mod, jnp.zeros((d,), F32)]), jnp.zeros((d,), F32))
                gd = N_KV * HEAD_DIM
                w_vt = jnp.concatenate([w_kv[:, 3 * gd:4 * gd], w_kv[:, 5 * gd:6 * gd]], axis=1).T
                raw, ks, kw, vst, vwt = _kvproj(xs, kv_vec, w_kv.astype(BF16), w_vt.astype(BF16))
                nc = s // D_CMP
                width = D_CMP * HEAD_DIM
                hid = cmp_w1.shape[-1]
                w2p = jnp.pad(cmp_w2, ((0, 0), (0, 0), (0, LANES - HEAD_DIM))).astype(BF16)
                b2p = jnp.pad(cmp_b2, ((0, 0), (0, LANES - HEAD_DIM))).reshape(2, 1, LANES)
                cmp_n, cmp_t = _compress(raw.reshape(2, N_KV, nc, width),
                                         cmp_pos.reshape(2, 2, 1, width),
                                         cmp_w1.reshape(2, 2, width, hid).astype(BF16),
                                         cmp_b1.reshape(2, 1, hid), w2p, b2p)
                shared = (cmp_n[0], cmp_t[1], ks, vst, kw, vwt)
            i = layer - n_a
            w_t = b_w_in[i].T.astype(BF16)
            q_t, gate_t = _qproj(xs, vec, w_t)
            kc, vct, ks, vst, kw, vwt = shared
            o_t = _attention(q_t, gate_t, kc, vct, ks, vst, kw, vwt)
            attn = (o_t, vec, b_w_out[i].astype(BF16))

        vec = _vec(norm_pre_g[layer, 1], mods[layer, 1], norm_post_g[layer, 1])
        j = layer // 2
        if layer % 2 == 0:
            xs = _swiglu_layer(xs, vec, ffn_w_gate[j].astype(BF16), ffn_w_up[j].astype(BF16),
                               ffn_w_down[j].astype(BF16), attn)
        else:
            xs = _moe_layer(xs, vec, moe_router[j].T, moe_router_b[j].reshape(N_EXPERTS, 1),
                            moe_wg, moe_wu, moe_wd, j * N_EXPERTS, attn)
    return xs.reshape(batch, s, d)
```

```python
import functools

import jax
import jax.numpy as jnp
from jax import lax
from jax.experimental import pallas as pl
from jax.experimental.pallas import tpu as pltpu

F32 = jnp.float32
BF16 = jnp.bfloat16
I32 = jnp.int32

EPS = 1e-6
NEG = -1e30

LANES = 128
SUBLANES = 8
VMEM_LIMIT_BYTES = 56 * 1024 * 1024

CHUNK = 128
A_GROUPS = 8
N_HEADS = 16
N_KV = 4
HPG = N_HEADS // N_KV
HEAD_DIM = 64
L_CMP = 32
D_CMP = 16
L_SEL = 64
SEL_SHIFT = 6
N_SELECT = 16
N_FORCED = 3
WINDOW = 512
Q_BLK = 128
N_EXPERTS = 8
MOE_BLOCK = 256

ROW_TILE = 512
TOK_TILE = 256
DISP_TILE = 512
SEL_CHUNK = 512
SEL_BLKS = SEL_CHUNK // L_SEL
CMP_CHUNK = 128
ATT_GROUPS = 2
GATE_ROWS = 16
PS_PAD = 8
V_ROWS = 72
F_CHUNK = 512
WIN_PART_ROWS = 128
WIN_PARTS = TOK_TILE // WIN_PART_ROWS + 1
WIN_ROWS = WIN_PARTS * WIN_PART_ROWS
LOG2E = 1.4426950408889634


def _params(*sem):
    return pltpu.CompilerParams(dimension_semantics=sem, vmem_limit_bytes=VMEM_LIMIT_BYTES)


def _resident(shape, index_map):
    return pl.BlockSpec(shape, index_map, pipeline_mode=pl.Buffered(1))


def _split_bf16(a):
    hi = a.astype(BF16)
    lo = (a - hi.astype(F32)).astype(BF16)
    return hi, lo


def _dot(a, b):
    return jnp.dot(a, b, preferred_element_type=F32)


def _dot_nt(a, b):
    return lax.dot_general(a, b, (((1,), (1,)), ((), ())), preferred_element_type=F32)


def _dot_tn(a, b):
    return lax.dot_general(a, b, (((0,), (0,)), ((), ())), preferred_element_type=F32)


def _prenorm(x, g, scale, shift):
    ms = jnp.mean(x * x, axis=-1, keepdims=True)
    return (x * lax.rsqrt(ms + EPS) * g) * (1.0 + scale) + shift


def _post_residual(x, y, g, gate):
    ms = jnp.mean(y * y, axis=-1, keepdims=True)
    return x + gate * (y * lax.rsqrt(ms + EPS) * g)


def _ada_kernel(c_ref, w_ref, b_ref, o_ref):
    c = c_ref[...]
    c_act = jnp.broadcast_to(c * jax.nn.sigmoid(c), (SUBLANES, c.shape[1]))
    c_hi, c_lo = _split_bf16(c_act)
    w_hi, w_lo = _split_bf16(w_ref[...])
    m = _dot(c_hi, w_hi) + (_dot(c_hi, w_lo) + _dot(c_lo, w_hi))
    o_ref[...] = m[0:1] + b_ref[...]


def _ada(c, w, b):
    n, d, nn = w.shape
    out = pl.pallas_call(
        _ada_kernel,
        grid=(n, nn // d),
        in_specs=[pl.BlockSpec((1, d), lambda i, j: (0, 0)),
                  pl.BlockSpec((None, d, d), lambda i, j: (i, 0, j)),
                  pl.BlockSpec((None, 1, d), lambda i, j: (i, 0, j))],
        out_specs=pl.BlockSpec((None, 1, d), lambda i, j: (i, 0, j)),
        out_shape=jax.ShapeDtypeStruct((n, 1, nn), F32),
        compiler_params=_params("arbitrary", "arbitrary"),
        name="ada",
    )(c, w, b.reshape(n, 1, nn))
    return out.reshape(n, nn)


def _cast_spec(w, steps):
    rows, cols = w.shape
    assert rows % (steps * 2 * SUBLANES) == 0 and cols % LANES == 0
    return pl.BlockSpec((rows // steps, cols), lambda i: (i, 0))


def _cast_slabs(src_refs, dst_refs):
    for src, dst in zip(src_refs, dst_refs):
        dst[...] = src[...].astype(BF16)


def _gmlp_kernel(n_cast, x_ref, vec_ref, win_ref, ln_ref, ws_ref, bst_ref, wout_ref, *refs):
    cast_refs, o_ref, cast_out_refs, gated_ref = refs[:n_cast], refs[n_cast], refs[n_cast + 1:-1], refs[-1]
    _cast_slabs(cast_refs, cast_out_refs)
    x = x_ref[...]
    vec = vec_ref[...]
    h = _prenorm(x, vec[0:1], vec[1:2], vec[2:3]).astype(BF16)
    z = jax.nn.gelu(_dot(h, win_ref[...]))
    width = z.shape[1] // 2
    u = z[:, :width]
    v = z[:, width:]
    mu = jnp.mean(v, axis=-1, keepdims=True)
    vc = v - mu
    var = jnp.mean(vc * vc, axis=-1, keepdims=True)
    ln = ln_ref[...]
    vn = (vc * lax.rsqrt(var + EPS) * ln[0:1] + ln[1:2]).astype(BF16)
    causal = (lax.broadcasted_iota(I32, (CHUNK, CHUNK), 0)
              >= lax.broadcasted_iota(I32, (CHUNK, CHUNK), 1))
    gw = width // A_GROUPS
    bst = bst_ref[...]
    for g in range(A_GROUPS):
        wg = jnp.where(causal, ws_ref[g], 0.0).astype(BF16)
        for ck in range(x.shape[0] // CHUNK):
            rows = slice(ck * CHUNK, (ck + 1) * CHUNK)
            cols = slice(g * gw, (g + 1) * gw)
            mixed = _dot(wg, vn[rows, cols]) + bst[:, g:g + 1]
            gated_ref[rows, cols] = (u[rows, cols] * mixed).astype(BF16)
    y = _dot(gated_ref[...], wout_ref[...])
    o_ref[...] = _post_residual(x, y, vec[3:4], vec[4:5])


def _gmlp_layer(x, vec, w_in, ln, ws, bst, w_out, cast=()):
    s, d = x.shape
    e2 = w_in.shape[1]
    steps = s // ROW_TILE
    cast_specs = [_cast_spec(w, steps) for w in cast]
    out, *cast_out = pl.pallas_call(
        functools.partial(_gmlp_kernel, len(cast)),
        grid=(steps,),
        in_specs=[pl.BlockSpec((ROW_TILE, d), lambda i: (i, 0)),
                  _resident((SUBLANES, d), lambda i: (0, 0)),
                  _resident((d, e2), lambda i: (0, 0)),
                  _resident((2, e2 // 2), lambda i: (0, 0)),
                  _resident(ws.shape, lambda i: (0, 0, 0)),
                  _resident(bst.shape, lambda i: (0, 0)),
                  _resident((e2 // 2, d), lambda i: (0, 0))] + cast_specs,
        out_specs=[pl.BlockSpec((ROW_TILE, d), lambda i: (i, 0))] + cast_specs,
        out_shape=[jax.ShapeDtypeStruct((s, d), F32)] + [jax.ShapeDtypeStruct(w.shape, BF16) for w in cast],
        scratch_shapes=[pltpu.VMEM((ROW_TILE, e2 // 2), BF16)],
        compiler_params=_params("arbitrary"),
        name="gmlp",
    )(x, vec, w_in, ln, ws, bst, w_out, *cast)
    return out, cast_out


def _attn_residual(ot_ref, x_ref, avec_ref, wo_ref):
    avec = avec_ref[...]
    return _post_residual(x_ref[...], _dot_tn(ot_ref[...], wo_ref[...]), avec[3:4], avec[4:5])


def _attn_specs(o_t, d):
    nqd = o_t.shape[0]
    return [pl.BlockSpec((nqd, ROW_TILE), lambda i: (0, i)),
            pl.BlockSpec((ROW_TILE, d), lambda i: (i, 0)),
            _resident((SUBLANES, d), lambda i: (0, 0)),
            _resident((nqd, d), lambda i: (0, 0))]


def _swiglu_rows(x, vec, wg_ref, wu_ref, wd_ref):
    h = _prenorm(x, vec[0:1], vec[1:2], vec[2:3]).astype(BF16)
    g = _dot(h, wg_ref[...])
    a = (g * jax.nn.sigmoid(g) * _dot(h, wu_ref[...])).astype(BF16)
    y = _dot(a, wd_ref[...])
    return _post_residual(x, y, vec[3:4], vec[4:5])


def _swiglu_kernel(n_cast, x_ref, vec_ref, wg_ref, wu_ref, wd_ref, *refs):
    _cast_slabs(refs[:n_cast], refs[n_cast + 1:])
    refs[n_cast][...] = _swiglu_rows(x_ref[...], vec_ref[...], wg_ref, wu_ref, wd_ref)


def _attn_swiglu_kernel(n_cast, ot_ref, x_ref, avec_ref, wo_ref, vec_ref, wg_ref, wu_ref, wd_ref, *refs):
    _cast_slabs(refs[:n_cast], refs[n_cast + 1:])
    x = _attn_residual(ot_ref, x_ref, avec_ref, wo_ref)
    refs[n_cast][...] = _swiglu_rows(x, vec_ref[...], wg_ref, wu_ref, wd_ref)


def _swiglu_layer(x, vec, w_gate, w_up, w_down, attn=None, cast=()):
    s, d = x.shape
    f = w_gate.shape[1]
    steps = s // ROW_TILE
    cast_specs = [_cast_spec(w, steps) for w in cast]
    own_specs = [_resident((SUBLANES, d), lambda i: (0, 0)),
                 _resident((d, f), lambda i: (0, 0)),
                 _resident((d, f), lambda i: (0, 0)),
                 _resident((f, d), lambda i: (0, 0))]
    if attn is None:
        body, in_specs, args = _swiglu_kernel, [pl.BlockSpec((ROW_TILE, d), lambda i: (i, 0))], (x,)
    else:
        o_t, avec, w_out = attn
        body, in_specs, args = _attn_swiglu_kernel, _attn_specs(o_t, d), (o_t, x, avec, w_out)
    out, *cast_out = pl.pallas_call(
        functools.partial(body, len(cast)),
        grid=(steps,),
        in_specs=in_specs + own_specs + cast_specs,
        out_specs=[pl.BlockSpec((ROW_TILE, d), lambda i: (i, 0))] + cast_specs,
        out_shape=[jax.ShapeDtypeStruct((s, d), F32)] + [jax.ShapeDtypeStruct(w.shape, BF16) for w in cast],
        compiler_params=_params("arbitrary"),
        name="swiglu",
    )(*args, vec, w_gate, w_up, w_down, *cast)
    return out, cast_out


def _router_kernel(x_ref, vec_ref, wrt_ref, br_ref, h_ref, idx_ref, gate_ref):
    _route_rows(x_ref[...], vec_ref[...], wrt_ref, br_ref, h_ref, idx_ref, gate_ref)


def _attn_router_kernel(ot_ref, x_ref, avec_ref, wo_ref, vec_ref, wrt_ref, br_ref, xo_ref, h_ref, idx_ref, gate_ref):
    x = _attn_residual(ot_ref, x_ref, avec_ref, wo_ref)
    xo_ref[...] = x
    _route_rows(x, vec_ref[...], wrt_ref, br_ref, h_ref, idx_ref, gate_ref)


def _route_rows(x, vec, wrt_ref, br_ref, h_ref, idx_ref, gate_ref):
    h = _prenorm(x, vec[0:1], vec[1:2], vec[2:3])
    h_ref[...] = h.astype(BF16)
    h_hi, h_lo = _split_bf16(h)
    w_hi, w_lo = _split_bf16(wrt_ref[...])
    logit = _dot_nt(w_hi, h_hi) + (_dot_nt(w_hi, h_lo) + _dot_nt(w_lo, h_hi)) + br_ref[...]
    ne = logit.shape[0]
    eidx = lax.broadcasted_iota(I32, logit.shape, 0)
    m1 = jnp.max(logit, axis=0, keepdims=True)
    i1 = jnp.min(jnp.where(logit == m1, eidx, ne), axis=0, keepdims=True)
    rest = jnp.where(eidx == i1, -jnp.inf, logit)
    m2 = jnp.max(rest, axis=0, keepdims=True)
    i2 = jnp.min(jnp.where(rest == m2, eidx, ne), axis=0, keepdims=True)
    e21 = jnp.exp(m2 - m1)
    g1 = 1.0 / (1.0 + e21)
    g2 = e21 * g1
    pad_i = jnp.zeros((SUBLANES - 2, i1.shape[1]), I32)
    pad_f = jnp.zeros((SUBLANES - 2, i1.shape[1]), F32)
    idx_ref[...] = jnp.concatenate([i1, i2, pad_i], axis=0)
    gate_ref[...] = jnp.concatenate([g1, g2, pad_f], axis=0)


def _router(x, vec, w_rt, b_r, attn=None):
    s, d = x.shape
    ne = w_rt.shape[0]
    own_specs = [_resident((SUBLANES, d), lambda i: (0, 0)),
                 _resident((ne, d), lambda i: (0, 0)),
                 _resident((ne, 1), lambda i: (0, 0))]
    out_specs = [pl.BlockSpec((ROW_TILE, d), lambda i: (i, 0)),
                 pl.BlockSpec((SUBLANES, ROW_TILE), lambda i: (0, i)),
                 pl.BlockSpec((SUBLANES, ROW_TILE), lambda i: (0, i))]
    out_shape = [jax.ShapeDtypeStruct((s, d), BF16),
                 jax.ShapeDtypeStruct((SUBLANES, s), I32),
                 jax.ShapeDtypeStruct((SUBLANES, s), F32)]
    if attn is None:
        h, idx, gates = pl.pallas_call(
            _router_kernel,
            grid=(s // ROW_TILE,),
            in_specs=[pl.BlockSpec((ROW_TILE, d), lambda i: (i, 0))] + own_specs,
            out_specs=out_specs,
            out_shape=out_shape,
            compiler_params=_params("arbitrary"),
            name="moe_router",
        )(x, vec, w_rt, b_r)
        return x, h, idx, gates
    o_t, avec, w_out = attn
    return pl.pallas_call(
        _attn_router_kernel,
        grid=(s // ROW_TILE,),
        in_specs=_attn_specs(o_t, d) + own_specs,
        out_specs=[pl.BlockSpec((ROW_TILE, d), lambda i: (i, 0))] + out_specs,
        out_shape=[jax.ShapeDtypeStruct((s, d), F32)] + out_shape,
        compiler_params=_params("arbitrary"),
        name="moe_router",
    )(o_t, x, avec, w_out, vec, w_rt, b_r)


def _rank_kernel(idx_ref, rank_ref, start_ref, count_ref, carry_ref):
    @pl.when(pl.program_id(0) == 0)
    def _():
        carry_ref[...] = jnp.zeros_like(carry_ref)

    idx = idx_ref[...]
    tt = idx.shape[1]
    eidx = lax.broadcasted_iota(I32, (N_EXPERTS, tt), 0)
    hit1 = eidx == idx[0:1]
    hit2 = eidx == idx[1:2]
    member = jnp.where(hit1 | hit2, 1.0, 0.0)
    before = (lax.broadcasted_iota(I32, (tt, tt), 0) < lax.broadcasted_iota(I32, (tt, tt), 1))
    carry = carry_ref[...]
    cum = _dot(member.astype(BF16), jnp.where(before, 1.0, 0.0).astype(BF16)) + carry[:, 0:1]
    r1 = jnp.sum(jnp.where(hit1, cum, 0.0), axis=0, keepdims=True)
    r2 = jnp.sum(jnp.where(hit2, cum, 0.0), axis=0, keepdims=True)
    rank_ref[...] = jnp.concatenate([r1, r2, jnp.zeros((SUBLANES - 2, tt), F32)], axis=0).astype(I32)
    tile_count = jnp.broadcast_to(jnp.sum(member, axis=1, keepdims=True), carry.shape)
    start_ref[...] = carry
    count_ref[...] = tile_count
    carry_ref[...] = carry + tile_count


def _ranks(idx):
    s = idx.shape[1]
    nt = s // TOK_TILE
    return pl.pallas_call(
        _rank_kernel,
        grid=(nt,),
        in_specs=[pl.BlockSpec((SUBLANES, TOK_TILE), lambda i: (0, i))],
        out_specs=[pl.BlockSpec((SUBLANES, TOK_TILE), lambda i: (0, i)),
                   pl.BlockSpec((None, N_EXPERTS, LANES), lambda i: (i, 0, 0)),
                   pl.BlockSpec((None, N_EXPERTS, LANES), lambda i: (i, 0, 0))],
        out_shape=[jax.ShapeDtypeStruct((SUBLANES, s), I32),
                   jax.ShapeDtypeStruct((nt, N_EXPERTS, LANES), F32),
                   jax.ShapeDtypeStruct((nt, N_EXPERTS, LANES), F32)],
        scratch_shapes=[pltpu.VMEM((N_EXPERTS, LANES), F32)],
        compiler_params=_params("arbitrary"),
        name="moe_ranks",
    )(idx)


def _match(idx, rank, expert, rows):
    r1 = jnp.where(idx[0:1] == expert, rank[0:1], -1)
    r2 = jnp.where(idx[1:2] == expert, rank[1:2], -1)
    return rows == r1, rows == r2


def _dispatch_kernel(be_ref, lb_ref, tlo_ref, thi_ref, idx_ref, rank_ref, gate_ref, h_ref, o_ref, rg_ref,
                     acc_ref, gacc_ref):
    b = pl.program_id(0)
    expert = be_ref[b]
    rows = lax.broadcasted_iota(I32, (MOE_BLOCK, DISP_TILE), 0) + lb_ref[b]
    acc_ref[...] = jnp.zeros_like(acc_ref)
    gacc_ref[...] = jnp.zeros_like(gacc_ref)

    def body(t, carry):
        off = pl.multiple_of(t * DISP_TILE, DISP_TILE)
        m1, m2 = _match(idx_ref[:, pl.ds(off, DISP_TILE)], rank_ref[:, pl.ds(off, DISP_TILE)], expert, rows)
        gates = gate_ref[:, pl.ds(off, DISP_TILE)]
        gacc_ref[...] += jnp.sum(jnp.where(m1, gates[0:1], 0.0) + jnp.where(m2, gates[1:2], 0.0),
                                 axis=1, keepdims=True)
        onehot = jnp.where(m1 | m2, 1.0, 0.0).astype(BF16)
        acc_ref[...] += _dot(onehot, h_ref[pl.ds(off, DISP_TILE), :])
        return carry

    lax.fori_loop(tlo_ref[b], thi_ref[b], body, 0)
    o_ref[...] = acc_ref[...].astype(BF16)
    rg_ref[...] = gacc_ref[...]


def _dispatch(blk_e, blk_lb, blk_tlo, blk_thi, idx, rank, gates, h):
    s, d = h.shape
    n_blk = blk_e.shape[0]
    grid_spec = pltpu.PrefetchScalarGridSpec(
        num_scalar_prefetch=4,
        grid=(n_blk,),
        in_specs=[_resident((SUBLANES, s), lambda b, *_: (0, 0)),
                  _resident((SUBLANES, s), lambda b, *_: (0, 0)),
                  _resident((SUBLANES, s), lambda b, *_: (0, 0)),
                  _resident((s, d), lambda b, *_: (0, 0))],
        out_specs=[pl.BlockSpec((MOE_BLOCK, d), lambda b, *_: (b, 0)),
                   pl.BlockSpec((MOE_BLOCK, 1), lambda b, *_: (b, 0))],
        scratch_shapes=[pltpu.VMEM((MOE_BLOCK, d), F32), pltpu.VMEM((MOE_BLOCK, 1), F32)],
    )
    return pl.pallas_call(
        _dispatch_kernel,
        grid_spec=grid_spec,
        out_shape=[jax.ShapeDtypeStruct((n_blk * MOE_BLOCK, d), BF16),
                   jax.ShapeDtypeStruct((n_blk * MOE_BLOCK, 1), F32)],
        compiler_params=_params("arbitrary"),
        name="moe_dispatch",
    )(blk_e, blk_lb, blk_tlo, blk_thi, idx, rank, gates, h)


def _expert_kernel(be_ref, used_ref, x_ref, rg_ref, wg_ref, wu_ref, wd_ref, o_ref):
    b = pl.program_id(0)

    @pl.when(b < used_ref[0])
    def _():
        x = x_ref[...]
        f = wg_ref.shape[1]
        y = jnp.zeros(o_ref.shape, F32)
        for c in range(f // F_CHUNK):
            cols = slice(c * F_CHUNK, (c + 1) * F_CHUNK)
            g = _dot(x, wg_ref[:, cols])
            a = (g * jax.nn.sigmoid(g) * _dot(x, wu_ref[:, cols])).astype(BF16)
            y = y + _dot(a, wd_ref[cols, :])
        o_ref[...] = (y * rg_ref[...]).astype(BF16)

    @pl.when(b >= used_ref[0])
    def _():
        o_ref[...] = jnp.zeros_like(o_ref)


def _experts(blk_e, n_used, x_buf, row_gate, w_gate, w_up, w_down):
    n_rows, d = x_buf.shape
    f = w_gate.shape[2]
    assert f % F_CHUNK == 0
    grid_spec = pltpu.PrefetchScalarGridSpec(
        num_scalar_prefetch=2,
        grid=(n_rows // MOE_BLOCK,),
        in_specs=[pl.BlockSpec((MOE_BLOCK, d), lambda b, be, nu: (b, 0)),
                  pl.BlockSpec((MOE_BLOCK, 1), lambda b, be, nu: (b, 0)),
                  pl.BlockSpec((None, d, f), lambda b, be, nu: (be[b], 0, 0)),
                  pl.BlockSpec((None, d, f), lambda b, be, nu: (be[b], 0, 0)),
                  pl.BlockSpec((None, f, d), lambda b, be, nu: (be[b], 0, 0))],
        out_specs=pl.BlockSpec((MOE_BLOCK, d), lambda b, be, nu: (b, 0)),
    )
    return pl.pallas_call(
        _expert_kernel,
        grid_spec=grid_spec,
        out_shape=jax.ShapeDtypeStruct((n_rows, d), BF16),
        compiler_params=_params("arbitrary"),
        name="moe_experts",
    )(blk_e, n_used, x_buf, row_gate, w_gate, w_up, w_down)


def _combine_kernel(win_ref, lb_ref, tail_ref, idx_ref, rank_ref, *refs):
    y_refs = refs[:N_EXPERTS * WIN_PARTS]
    x_ref, vec_ref, o_ref, acc_ref = refs[N_EXPERTS * WIN_PARTS:]
    t = pl.program_id(0)
    idx = idx_ref[...]
    rank = rank_ref[...]
    main_rows = (WIN_PARTS - 1) * WIN_PART_ROWS

    def gathered(e, first_part, n_parts):
        n_rows = n_parts * WIN_PART_ROWS
        rows = (lax.broadcasted_iota(I32, (n_rows, TOK_TILE), 0)
                + (lb_ref[t * N_EXPERTS + e] + first_part * WIN_PART_ROWS))
        m1, m2 = _match(idx, rank, e, rows)
        onehot = jnp.where(m1 | m2, 1.0, 0.0).astype(BF16)
        y = jnp.concatenate([y_refs[e * WIN_PARTS + first_part + k][...] for k in range(n_parts)], axis=0)
        return _dot_tn(onehot, y)

    acc = jnp.zeros(o_ref.shape, F32)
    for e in range(N_EXPERTS):
        acc = acc + gathered(e, 0, WIN_PARTS - 1)
    acc_ref[...] = acc
    for e in range(N_EXPERTS):
        @pl.when(tail_ref[t * N_EXPERTS + e] > 0)
        def _():
            acc_ref[...] += gathered(e, WIN_PARTS - 1, 1)
    vec = vec_ref[...]
    o_ref[...] = _post_residual(x_ref[...], acc_ref[...], vec[3:4], vec[4:5])


def _combine(win_start, win_lb, win_tail, idx, rank, y_buf, x, vec):
    s, d = x.shape

    def y_spec(e, k):
        return pl.BlockSpec((WIN_PART_ROWS, d), lambda t, ws, lb, tl: (ws[t * N_EXPERTS + e] + k, 0))

    grid_spec = pltpu.PrefetchScalarGridSpec(
        num_scalar_prefetch=3,
        grid=(s // TOK_TILE,),
        in_specs=[pl.BlockSpec((SUBLANES, TOK_TILE), lambda t, *_: (0, t)),
                  pl.BlockSpec((SUBLANES, TOK_TILE), lambda t, *_: (0, t))]
                 + [y_spec(e, k) for e in range(N_EXPERTS) for k in range(WIN_PARTS)]
                 + [pl.BlockSpec((TOK_TILE, d), lambda t, *_: (t, 0)),
                    _resident((SUBLANES, d), lambda t, *_: (0, 0))],
        out_specs=pl.BlockSpec((TOK_TILE, d), lambda t, *_: (t, 0)),
        scratch_shapes=[pltpu.VMEM((TOK_TILE, d), F32)],
    )
    return pl.pallas_call(
        _combine_kernel,
        grid_spec=grid_spec,
        out_shape=jax.ShapeDtypeStruct((s, d), F32),
        compiler_params=_params("arbitrary"),
        name="moe_combine",
    )(win_start, win_lb, win_tail, idx, rank, *([y_buf] * (N_EXPERTS * WIN_PARTS)), x, vec)


def _moe_layer(x, vec, w_rt, b_r, w_gate, w_up, w_down, attn=None):
    s, d = x.shape
    nt = s // TOK_TILE
    x, h, idx, gates = _router(x, vec, w_rt, b_r, attn)
    rank, tile_start, tile_count = _ranks(idx)

    tile_start = tile_start[:, :, 0].astype(I32)
    tile_count = tile_count[:, :, 0].astype(I32)
    tile_end = tile_start + tile_count
    counts = tile_end[-1]
    padded = (counts + MOE_BLOCK - 1) // MOE_BLOCK * MOE_BLOCK
    pend = jnp.cumsum(padded)
    pstart = pend - padded
    n_rows = -(-(2 * s) // MOE_BLOCK) * MOE_BLOCK + N_EXPERTS * MOE_BLOCK
    n_blk = n_rows // MOE_BLOCK
    blk_row = jnp.arange(n_blk, dtype=I32) * MOE_BLOCK
    blk_e = jnp.minimum(jnp.sum(blk_row[:, None] >= pend[None, :], axis=1), N_EXPERTS - 1).astype(I32)
    blk_lb = blk_row - pstart[blk_e]
    per = DISP_TILE // TOK_TILE
    te = tile_end[per - 1::per][:, blk_e]
    ts = tile_start[::per][:, blk_e]
    blk_tlo = jnp.sum(te <= blk_lb[None, :], axis=0).astype(I32)
    blk_thi = jnp.sum(ts < (blk_lb + MOE_BLOCK)[None, :], axis=0).astype(I32)
    n_used = (pend[-1] // MOE_BLOCK).astype(I32).reshape(1)

    x_buf, row_gate = _dispatch(blk_e, blk_lb.astype(I32), blk_tlo, blk_thi, idx, rank, gates, h)
    y_buf = _experts(blk_e, n_used, x_buf, row_gate, w_gate, w_up, w_down)

    win_start = jnp.minimum((pstart[None, :] + tile_start) // WIN_PART_ROWS, n_rows // WIN_PART_ROWS - WIN_PARTS)
    win_lb = win_start * WIN_PART_ROWS - pstart[None, :]
    win_tail = tile_end - win_lb > (WIN_PARTS - 1) * WIN_PART_ROWS
    return _combine(win_start.reshape(-1).astype(I32), win_lb.reshape(-1).astype(I32),
                    win_tail.reshape(-1).astype(I32), idx, rank, y_buf, x, vec)


def _kvproj_kernel(x_ref, vec_ref, wkv_ref, wvt_ref, raw_ref, ks_ref, kw_ref, vst_ref, vwt_ref):
    x = x_ref[...]
    vec = vec_ref[...]
    h = _prenorm(x, vec[0:1], vec[1:2], vec[2:3]).astype(BF16)
    kv = _dot(h, wkv_ref[...])
    vt = _dot_nt(wvt_ref[...], h)
    tm = x.shape[0]
    gd = N_KV * HEAD_DIM
    key_blk = jnp.right_shift(pl.program_id(0) * tm + lax.broadcasted_iota(I32, (tm, SEL_BLKS), 0), SEL_SHIFT)
    ind = jnp.where((key_blk & (SEL_BLKS - 1)) == lax.broadcasted_iota(I32, (tm, SEL_BLKS), 1), 1.0, 0.0)
    pad_s = jnp.zeros((tm, LANES - HEAD_DIM - SEL_BLKS), F32)
    pad_w = jnp.zeros((tm, LANES - HEAD_DIM), F32)
    ones_row = jnp.concatenate([jnp.ones((1, tm), F32), jnp.zeros((V_ROWS - HEAD_DIM - 1, tm), F32)], axis=0)
    for g in range(N_KV):
        c = g * HEAD_DIM
        raw_ref[0, g] = kv[:, c:c + HEAD_DIM]
        raw_ref[1, g] = kv[:, gd + c:gd + c + HEAD_DIM]
        ks = kv[:, 2 * gd + c:2 * gd + c + HEAD_DIM]
        kw = kv[:, 4 * gd + c:4 * gd + c + HEAD_DIM]
        ks_ref[g] = jnp.concatenate([ks, ind, pad_s], axis=1).astype(BF16)
        kw_ref[g] = jnp.concatenate([kw, pad_w], axis=1).astype(BF16)
        vst_ref[g] = jnp.concatenate([vt[c:c + HEAD_DIM], ones_row], axis=0).astype(BF16)
        vwt_ref[g] = jnp.concatenate([vt[gd + c:gd + c + HEAD_DIM], ones_row], axis=0).astype(BF16)


def _kvproj(x, vec, w_kv, w_vt):
    s, d = x.shape
    nkv = w_kv.shape[1]
    return pl.pallas_call(
        _kvproj_kernel,
        grid=(s // ROW_TILE,),
        in_specs=[pl.BlockSpec((ROW_TILE, d), lambda i: (i, 0)),
                  _resident((SUBLANES, d), lambda i: (0, 0)),
                  _resident((d, nkv), lambda i: (0, 0)),
                  _resident(w_vt.shape, lambda i: (0, 0))],
        out_specs=[pl.BlockSpec((2, N_KV, ROW_TILE, HEAD_DIM), lambda i: (0, 0, i, 0)),
                   pl.BlockSpec((N_KV, ROW_TILE, LANES), lambda i: (0, i, 0)),
                   pl.BlockSpec((N_KV, ROW_TILE, LANES), lambda i: (0, i, 0)),
                   pl.BlockSpec((N_KV, V_ROWS, ROW_TILE), lambda i: (0, 0, i)),
                   pl.BlockSpec((N_KV, V_ROWS, ROW_TILE), lambda i: (0, 0, i))],
        out_shape=[jax.ShapeDtypeStruct((2, N_KV, s, HEAD_DIM), F32),
                   jax.ShapeDtypeStruct((N_KV, s, LANES), BF16),
                   jax.ShapeDtypeStruct((N_KV, s, LANES), BF16),
                   jax.ShapeDtypeStruct((N_KV, V_ROWS, s), BF16),
                   jax.ShapeDtypeStruct((N_KV, V_ROWS, s), BF16)],
        compiler_params=_params("arbitrary"),
        name="nsa_kvproj",
    )(x, vec, w_kv, w_vt)


def _compress_kernel(raw_ref, pos_ref, w1_ref, b1_ref, w2_ref, b2_ref, n_ref, t_ref):
    raw = raw_ref[...]
    nc = raw.shape[0]
    first = _dot((raw + pos_ref[0]).astype(BF16), w1_ref[0])
    second = _dot((raw + pos_ref[1]).astype(BF16), w1_ref[1])
    hid = jax.nn.gelu(first + pltpu.roll(second, nc - 1, 0) + b1_ref[...])
    out = _dot(hid.astype(BF16), w2_ref[...]) + b2_ref[...]
    n_ref[...] = out.astype(BF16)
    row = lax.broadcasted_iota(I32, (LANES, nc), 0)
    t_ref[...] = jnp.where(row == HEAD_DIM, 1.0, out.T).astype(BF16)


def _compress(raw, pos, w1, b1, w2, b2):
    _, g, nc, width = raw.shape
    hid = w1.shape[-1]
    return pl.pallas_call(
        _compress_kernel,
        grid=(2, g),
        in_specs=[pl.BlockSpec((None, None, nc, width), lambda j, k: (j, k, 0, 0)),
                  pl.BlockSpec((None, 2, 1, width), lambda j, k: (j, 0, 0, 0)),
                  pl.BlockSpec((None, 2, width, hid), lambda j, k: (j, 0, 0, 0)),
                  pl.BlockSpec((None, 1, hid), lambda j, k: (j, 0, 0)),
                  pl.BlockSpec((None, hid, LANES), lambda j, k: (j, 0, 0)),
                  pl.BlockSpec((None, 1, LANES), lambda j, k: (j, 0, 0))],
        out_specs=[pl.BlockSpec((None, None, nc, LANES), lambda j, k: (j, k, 0, 0)),
                   pl.BlockSpec((None, None, LANES, nc), lambda j, k: (j, k, 0, 0))],
        out_shape=[jax.ShapeDtypeStruct((2, g, nc, LANES), BF16),
                   jax.ShapeDtypeStruct((2, g, LANES, nc), BF16)],
        compiler_params=_params("arbitrary", "arbitrary"),
        name="nsa_compress",
    )(raw, pos, w1, b1, w2, b2)


def _qproj_kernel(x_ref, vec_ref, wt_ref, q_ref, gate_ref):
    x = x_ref[...]
    vec = vec_ref[...]
    h = _prenorm(x, vec[0:1], vec[1:2], vec[2:3]).astype(BF16)
    pt = _dot_nt(wt_ref[...], h)
    nq = N_HEADS * HEAD_DIM
    q_ref[...] = (pt[:nq] * (HEAD_DIM ** -0.5 * LOG2E)).astype(BF16)
    gates = jax.nn.sigmoid(pt[nq:nq + 3 * N_HEADS])
    per = 3 * HPG
    pad = jnp.zeros((GATE_ROWS - per, x.shape[0]), F32)
    for g in range(N_KV):
        gate_ref[g] = jnp.concatenate([gates[g * per:(g + 1) * per], pad], axis=0)


def _qproj(x, vec, w_t):
    s, d = x.shape
    return pl.pallas_call(
        _qproj_kernel,
        grid=(s // ROW_TILE,),
        in_specs=[pl.BlockSpec((ROW_TILE, d), lambda i: (i, 0)),
                  _resident((SUBLANES, d), lambda i: (0, 0)),
                  _resident(w_t.shape, lambda i: (0, 0))],
        out_specs=[pl.BlockSpec((N_HEADS * HEAD_DIM, ROW_TILE), lambda i: (0, i)),
                   pl.BlockSpec((N_KV, GATE_ROWS, ROW_TILE), lambda i: (0, 0, i))],
        out_shape=[jax.ShapeDtypeStruct((N_HEADS * HEAD_DIM, s), BF16),
                   jax.ShapeDtypeStruct((N_KV, GATE_ROWS, s), F32)],
        compiler_params=_params("arbitrary"),
        name="nsa_qproj",
    )(x, vec, w_t)


def _attn_kernel(q_ref, gate_ref, cmask_ref, wmask_ref, kc_ref, vct_ref, ks_ref, vst_ref, kw_ref, vwt_ref,
                 o_ref, bias_ref, ps_ref, sc_ref, *sp_refs):
    s_refs = [[sp_refs[2 * g + k] for k in range(2)] for g in range(ATT_GROUPS)]
    p_refs = [[sp_refs[2 * ATT_GROUPS + 2 * g + k] for k in range(2)] for g in range(ATT_GROUPS)]
    i = pl.program_id(1)
    nq = HPG * Q_BLK
    nc = kc_ref.shape[1]
    nb = bias_ref.shape[1]
    rows = HPG * HEAD_DIM
    t1 = i * Q_BLK + lax.broadcasted_iota(I32, (1, Q_BLK), 1)
    t4 = jnp.concatenate([t1] * HPG, axis=1)
    j_io = lax.broadcasted_iota(I32, (nb, Q_BLK), 0)
    jt = jnp.right_shift(t1, SEL_SHIFT)
    forced = (j_io == 0) | (j_io == jt) | (j_io == jt - 1)
    cand_off = jnp.where((j_io * L_SEL <= t1) & jnp.logical_not(forced), 0.0, NEG)
    past_off = jnp.where(j_io * L_SEL < i * Q_BLK, 0.0, NEG)
    doff = pl.multiple_of(i * Q_BLK, Q_BLK)
    cmask_off = nc - i * (Q_BLK // D_CMP)
    span = WINDOW + Q_BLK
    wstart = pl.multiple_of(jnp.maximum(i * Q_BLK - WINDOW, 0), Q_BLK)
    win_mask = wmask_ref[pl.ds(pl.multiple_of(WINDOW - jnp.minimum(i * Q_BLK, WINDOW), Q_BLK), span), :]
    win_mask = jnp.concatenate([win_mask] * HPG, axis=1)
    diag_mask = jnp.concatenate([wmask_ref[WINDOW:WINDOW + Q_BLK, :]] * HPG, axis=1)
    vpad = jnp.zeros((LANES - HEAD_DIM - 2 * SEL_BLKS, nq), BF16)
    bpad = jnp.zeros((SEL_BLKS, nq), F32)

    def aligned(x, m):
        return x if isinstance(x, int) else pl.multiple_of(x, m)

    groups = range(ATT_GROUPS)
    q4s, q_plains = [], []
    for g in groups:
        qb = q_ref[g * rows:(g + 1) * rows, :]
        q4s.append(jnp.concatenate([qb[h * HEAD_DIM:(h + 1) * HEAD_DIM] for h in range(HPG)], axis=1))
        q_plains.append(jnp.concatenate([q4s[g], jnp.zeros((LANES - HEAD_DIM, nq), BF16)], axis=0))

    def compressed(n_rows):
        chunks = [slice(r0, r0 + CMP_CHUNK) for r0 in range(0, n_rows, CMP_CHUNK)]
        ms = [jnp.full((1, nq), NEG, F32) for _ in groups]
        for rs in chunks:
            mask = cmask_ref[pl.ds(pl.multiple_of(cmask_off + rs.start, 8), CMP_CHUNK), :]
            mask = jnp.concatenate([mask] * HPG, axis=1)
            for g in groups:
                s = _dot(kc_ref[g, rs, :], q_plains[g]) + mask
                sc_ref[g, rs, :] = s
                ms[g] = jnp.maximum(ms[g], jnp.max(s, axis=0, keepdims=True))
        ls = [jnp.zeros((1, nq), F32) for _ in groups]
        for rs in chunks:
            for g in groups:
                e = jnp.exp2(sc_ref[g, rs, :] - ms[g])
                sc_ref[g, rs, :] = e
                ls[g] = ls[g] + jnp.sum(e, axis=0, keepdims=True)
        rls = [jnp.where(t4 >= L_CMP - 1, 1.0 / ls[g], 0.0) for g in groups]
        os_ = [jnp.zeros((HEAD_DIM, nq), F32) for _ in groups]
        for g in groups:
            ps_ref[g, 0:PS_PAD, :] = jnp.zeros((PS_PAD, Q_BLK), F32)
            if n_rows < nc:
                ps_ref[g, PS_PAD + n_rows:PS_PAD + nc, :] = jnp.zeros((nc - n_rows, Q_BLK), F32)
        for rs in chunks:
            for g in groups:
                pc = sc_ref[g, rs, :] * rls[g]
                os_[g] = os_[g] + _dot(vct_ref[g, 0:HEAD_DIM, rs], pc.astype(BF16))
                psum = pc[:, 0:Q_BLK]
                for h in range(1, HPG):
                    psum = psum + pc[:, h * Q_BLK:(h + 1) * Q_BLK]
                ps_ref[g, PS_PAD + rs.start:PS_PAD + rs.stop, :] = psum
        ratio = L_SEL // D_CMP
        imps = []
        for g in groups:
            imp = jnp.zeros((nb, Q_BLK), F32)
            for k in range(1 - L_CMP // D_CMP, ratio):
                imp = imp + ps_ref[g, pl.ds(PS_PAD + k, nb, stride=ratio), :]
            imps.append(imp)
        return tuple(os_) + tuple(imps)

    if nc % (2 * CMP_CHUNK) == 0:
        last_complete = ((i + 1) * Q_BLK - L_CMP) // D_CMP
        cmp_out = lax.cond(last_complete < nc // 2, lambda: compressed(nc // 2), lambda: compressed(nc))
    else:
        cmp_out = compressed(nc)
    o_cmps, imps = cmp_out[:ATT_GROUPS], cmp_out[ATT_GROUPS:]

    def select(imps):
        w = jnp.concatenate([imp + cand_off for imp in imps], axis=1)
        j_all = lax.broadcasted_iota(I32, w.shape, 0).astype(F32)
        for _ in range(N_SELECT - N_FORCED):
            m = jnp.max(w, axis=0, keepdims=True)
            first = jnp.min(jnp.where(w == m, j_all, float(nb)), axis=0, keepdims=True)
            first = jnp.where(m > 0.5 * NEG, first, -1.0)
            w = jnp.where(j_all == first, NEG, w)
        for g in range(ATT_GROUPS):
            bias1 = jnp.where(w[:, g * Q_BLK:(g + 1) * Q_BLK] < 0.5 * NEG, past_off, NEG)
            bias_ref[g] = jnp.concatenate([bias1] * HPG, axis=1)

    def local():
        sds = [_dot(ks_ref[g, pl.ds(doff, Q_BLK), :], q_plains[g]) + diag_mask for g in groups]
        sws = [_dot(kw_ref[g, pl.ds(wstart, span), :], q_plains[g]) + win_mask for g in groups]
        out = []
        for g in groups:
            m0 = jnp.max(sds[g], axis=0, keepdims=True)
            acc0 = _dot(vst_ref[g, :, pl.ds(doff, Q_BLK)], jnp.exp2(sds[g] - m0).astype(BF16))
            pw = jnp.exp2(sws[g] - jnp.max(sws[g], axis=0, keepdims=True)).astype(BF16)
            acc_w = _dot(vwt_ref[g, :, pl.ds(wstart, span)], pw)
            out.append((acc_w[0:HEAD_DIM] / acc_w[HEAD_DIM:HEAD_DIM + 1], m0, acc0))
        return out

    def query_operand(g, q4, c):
        brow = bias_ref[g, pl.ds(aligned(c * SEL_BLKS, SEL_BLKS), SEL_BLKS), :]
        b16 = jnp.concatenate([brow, bpad], axis=0).astype(BF16)
        return jnp.concatenate([q4, b16, vpad], axis=0)

    def phase(g, q4, c, slot, state, do_scores=True, do_softmax=True, do_values=True):
        m_run, alpha, acc, cmax = state
        other = 1 - slot
        if do_values:
            voff = aligned(c * SEL_CHUNK, SEL_CHUNK)
            acc = alpha * acc + _dot(vst_ref[g, :, pl.ds(voff, SEL_CHUNK)], p_refs[g][slot][...])
        if do_scores:
            koff = aligned((c + 2) * SEL_CHUNK, SEL_CHUNK)
            s = _dot(ks_ref[g, pl.ds(koff, SEL_CHUNK), :], query_operand(g, q4, c + 2))
            s_refs[g][slot][...] = s.astype(BF16)
            new_max = jnp.max(s, axis=0, keepdims=True)
        if do_softmax:
            m_new = jnp.maximum(m_run, cmax[other])
            p_refs[g][other][...] = jnp.exp2(s_refs[g][other][...] - m_new.astype(BF16))
            m_run, alpha = m_new, jnp.exp2(m_run - m_new)
        if do_scores:
            cmax = (new_max, cmax[1]) if slot == 0 else (cmax[0], new_max)
        return m_run, alpha, acc, cmax

    select(imps)
    heads = [(q4s[g], o_cmps[g]) + loc for g, loc in enumerate(local())]

    def pair(k, carries):
        out = []
        for g in range(ATT_GROUPS):
            q4 = heads[g][0]
            state = phase(g, q4, 2 * k, 0, carries[g])
            out.append(phase(g, q4, 2 * k + 1, 1, state))
        return tuple(out)

    n_pairs = jnp.maximum((i * Q_BLK + 2 * SEL_CHUNK - 1) // (2 * SEL_CHUNK), 1)
    init = []
    for g in range(ATT_GROUPS):
        q4, _, _, m0, acc0 = heads[g]
        neg_row = jnp.full((1, nq), NEG, F32)
        state = (m0, jnp.ones((1, nq), F32), acc0, (neg_row, neg_row))
        state = phase(g, q4, -2, 0, state, do_softmax=False, do_values=False)
        init.append(phase(g, q4, -1, 1, state, do_values=False))
    carries = lax.fori_loop(0, n_pairs - 1, pair, tuple(init))
    last = 2 * (n_pairs - 1)
    for g in range(ATT_GROUPS):
        q4, o_cmp, o_win, _, _ = heads[g]
        state = phase(g, q4, last, 0, carries[g], do_scores=False)
        _, _, acc_s, _ = phase(g, q4, last + 1, 1, state, do_scores=False, do_softmax=False)
        o_sel = acc_s[0:HEAD_DIM] / acc_s[HEAD_DIM:HEAD_DIM + 1]
        gates = gate_ref[g]
        outs = []
        for h in range(HPG):
            cols = slice(h * Q_BLK, (h + 1) * Q_BLK)
            outs.append(o_cmp[:, cols] * gates[3 * h:3 * h + 1]
                        + o_sel[:, cols] * gates[3 * h + 1:3 * h + 2]
                        + o_win[:, cols] * gates[3 * h + 2:3 * h + 3])
        o_ref[g * rows:(g + 1) * rows, :] = jnp.concatenate(outs, axis=0).astype(BF16)


def _attention(q_t, gate_t, kc, vct, ks, vst, kw, vwt):
    nqd, s = q_t.shape
    nc = kc.shape[1]
    nb = s // L_SEL
    nq = HPG * Q_BLK
    rows = ATT_GROUPS * HPG * HEAD_DIM
    assert (s // SEL_CHUNK) % 2 == 0 and s >= WINDOW + Q_BLK and N_KV % ATT_GROUPS == 0 and nb >= N_SELECT
    qq = jnp.arange(Q_BLK)[None, :]
    rc = jnp.arange(2 * nc)[:, None]
    cmp_mask = jnp.where(D_CMP * (rc - nc) + L_CMP - 1 <= qq, 0.0, NEG).astype(F32)
    rw = jnp.arange(2 * WINDOW + Q_BLK)[:, None]
    win_mask = jnp.where((qq < rw) & (rw <= qq + WINDOW), 0.0, NEG).astype(F32)
    return pl.pallas_call(
        _attn_kernel,
        grid=(N_KV // ATT_GROUPS, s // Q_BLK),
        in_specs=[pl.BlockSpec((rows, Q_BLK), lambda g, i: (g, i)),
                  pl.BlockSpec((ATT_GROUPS, GATE_ROWS, Q_BLK), lambda g, i: (g, 0, i)),
                  _resident(cmp_mask.shape, lambda g, i: (0, 0)),
                  _resident(win_mask.shape, lambda g, i: (0, 0)),
                  _resident((ATT_GROUPS, nc, LANES), lambda g, i: (g, 0, 0)),
                  _resident((ATT_GROUPS, LANES, nc), lambda g, i: (g, 0, 0)),
                  _resident((ATT_GROUPS, s, LANES), lambda g, i: (g, 0, 0)),
                  _resident((ATT_GROUPS, V_ROWS, s), lambda g, i: (g, 0, 0)),
                  _resident((ATT_GROUPS, s, LANES), lambda g, i: (g, 0, 0)),
                  _resident((ATT_GROUPS, V_ROWS, s), lambda g, i: (g, 0, 0))],
        out_specs=pl.BlockSpec((rows, Q_BLK), lambda g, i: (g, i)),
        out_shape=jax.ShapeDtypeStruct((nqd, s), BF16),
        scratch_shapes=[pltpu.VMEM((ATT_GROUPS, nb, nq), F32),
                        pltpu.VMEM((ATT_GROUPS, PS_PAD + nc, Q_BLK), F32),
                        pltpu.VMEM((ATT_GROUPS, nc, nq), F32),
                        *[pltpu.VMEM((SEL_CHUNK, nq), BF16) for _ in range(2 * ATT_GROUPS)],
                        *[pltpu.VMEM((SEL_CHUNK, nq), BF16) for _ in range(2 * ATT_GROUPS)]],
        compiler_params=_params("arbitrary", "arbitrary"),
        name="nsa_attention",
    )(q_t, gate_t, cmp_mask, win_mask, kc, vct, ks, vst, kw, vwt)


def _vec(pre_g, mod, post_g):
    d = pre_g.shape[0]
    shift, scale, gate = mod[:d], mod[d:2 * d], mod[2 * d:3 * d]
    rows = [pre_g, scale, shift, post_g, gate]
    return jnp.stack(rows + [jnp.zeros((d,), F32)] * (SUBLANES - len(rows)))


def kernel(x, c, ada_w, ada_b, norm_pre_g, norm_post_g, a_w_in, a_ln_g, a_ln_b, a_ws, a_bs, a_w_out, kv_norm_g, kv_ada_w, kv_ada_b, w_kv, cmp_pos, cmp_w1, cmp_b1, cmp_w2, cmp_b2, b_w_in, b_w_out, ffn_w_gate, ffn_w_up, ffn_w_down, moe_router, moe_router_b, moe_w_gate, moe_w_up, moe_w_down):
    batch, s, d = x.shape
    assert batch == 1 and s % SEL_CHUNK == 0 and s >= WINDOW + Q_BLK
    depth = ada_w.shape[0]
    n_a = depth // 2
    xs = x.reshape(s, d)

    mods = _ada(c, ada_w.reshape(depth * 2, d, 3 * d), ada_b.reshape(depth * 2, 3 * d)).reshape(depth, 2, 3 * d)
    kv_mod = _ada(c, kv_ada_w.reshape(1, d, 2 * d), kv_ada_b.reshape(1, 2 * d))[0]
    shared = None
    n_moe, _, _, f_moe = moe_w_gate.shape
    assert n_moe <= n_a
    moe_gate_up = [None] * n_moe
    moe_down = [None] * n_moe

    for layer in range(depth):
        vec = _vec(norm_pre_g[layer, 0], mods[layer, 0], norm_post_g[layer, 0])
        attn = None
        if layer < n_a:
            cast = ()
            if layer < n_moe:
                cast = (moe_w_gate[layer].reshape(N_EXPERTS * d, f_moe), moe_w_up[layer].reshape(N_EXPERTS * d, f_moe))
            xs, cast = _gmlp_layer(xs, vec, a_w_in[layer].astype(BF16),
                                   jnp.stack([a_ln_g[layer], a_ln_b[layer]]),
                                   a_ws[layer], a_bs[layer].T, a_w_out[layer].astype(BF16), cast)
            if layer < n_moe:
                moe_gate_up[layer] = [w.reshape(N_EXPERTS, d, f_moe) for w in cast]
        else:
            if shared is None:
                kv_vec = _vec(kv_norm_g, jnp.concatenate([kv_mod, jnp.zeros((d,), F32)]), jnp.zeros((d,), F32))
                gd = N_KV * HEAD_DIM
                w_vt = jnp.concatenate([w_kv[:, 3 * gd:4 * gd], w_kv[:, 5 * gd:6 * gd]], axis=1).T
                raw, ks, kw, vst, vwt = _kvproj(xs, kv_vec, w_kv.astype(BF16), w_vt.astype(BF16))
                nc = s // D_CMP
                width = D_CMP * HEAD_DIM
                hid = cmp_w1.shape[-1]
                w2p = jnp.pad(cmp_w2, ((0, 0), (0, 0), (0, LANES - HEAD_DIM))).astype(BF16)
                b2p = jnp.pad(cmp_b2, ((0, 0), (0, LANES - HEAD_DIM))).reshape(2, 1, LANES)
                cmp_n, cmp_t = _compress(raw.reshape(2, N_KV, nc, width),
                                         cmp_pos.reshape(2, 2, 1, width),
                                         cmp_w1.reshape(2, 2, width, hid).astype(BF16),
                                         cmp_b1.reshape(2, 1, hid), w2p, b2p)
                shared = (cmp_n[0], cmp_t[1], ks, vst, kw, vwt)
            i = layer - n_a
            w_t = b_w_in[i].T.astype(BF16)
            q_t, gate_t = _qproj(xs, vec, w_t)
            kc, vct, ks, vst, kw, vwt = shared
            o_t = _attention(q_t, gate_t, kc, vct, ks, vst, kw, vwt)
            attn = (o_t, vec, b_w_out[i].astype(BF16))

        vec = _vec(norm_pre_g[layer, 1], mods[layer, 1], norm_post_g[layer, 1])
        j = layer // 2
        if layer % 2 == 0:
            cast = (moe_w_down[j].reshape(N_EXPERTS * f_moe, d),) if j < n_moe else ()
            xs, cast = _swiglu_layer(xs, vec, ffn_w_gate[j].astype(BF16), ffn_w_up[j].astype(BF16),
                                     ffn_w_down[j].astype(BF16), attn, cast)
            if j < n_moe:
                moe_down[j] = cast[0].reshape(N_EXPERTS, f_moe, d)
        else:
            xs = _moe_layer(xs, vec, moe_router[j].T, moe_router_b[j].reshape(N_EXPERTS, 1),
                            *moe_gate_up[j], moe_down[j], attn)
    return xs.reshape(batch, s, d)
```

```python
import functools

import jax
import jax.numpy as jnp
from jax import lax
from jax.experimental import pallas as pl
from jax.experimental.pallas import tpu as pltpu

F32 = jnp.float32
BF16 = jnp.bfloat16
I32 = jnp.int32

EPS = 1e-6
NEG = -1e30

LANES = 128
SUBLANES = 8
VMEM_LIMIT_BYTES = 56 * 1024 * 1024

CHUNK = 128
A_GROUPS = 8
N_HEADS = 16
N_KV = 4
HPG = N_HEADS // N_KV
HEAD_DIM = 64
L_CMP = 32
D_CMP = 16
L_SEL = 64
SEL_SHIFT = 6
N_SELECT = 16
N_FORCED = 3
WINDOW = 512
Q_BLK = 128
N_EXPERTS = 8
MOE_BLOCK = 256

ROW_TILE = 512
TOK_TILE = 256
DISP_TILE = 512
SEL_CHUNK = 512
SEL_BLKS = SEL_CHUNK // L_SEL
CMP_CHUNK = 128
ATT_GROUPS = 2
GATE_ROWS = 16
PS_PAD = 8
V_ROWS = 72
F_CHUNK = 512
WIN_PART_ROWS = 128
WIN_PARTS = TOK_TILE // WIN_PART_ROWS + 1
WIN_ROWS = WIN_PARTS * WIN_PART_ROWS
LOG2E = 1.4426950408889634


def _params(*sem):
    return pltpu.CompilerParams(dimension_semantics=sem, vmem_limit_bytes=VMEM_LIMIT_BYTES)


def _resident(shape, index_map):
    return pl.BlockSpec(shape, index_map, pipeline_mode=pl.Buffered(1))


def _split_bf16(a):
    hi = a.astype(BF16)
    lo = (a - hi.astype(F32)).astype(BF16)
    return hi, lo


def _dot(a, b):
    return jnp.dot(a, b, preferred_element_type=F32)


def _dot_nt(a, b):
    return lax.dot_general(a, b, (((1,), (1,)), ((), ())), preferred_element_type=F32)


def _dot_tn(a, b):
    return lax.dot_general(a, b, (((0,), (0,)), ((), ())), preferred_element_type=F32)


def _prenorm(x, g, scale, shift):
    ms = jnp.mean(x * x, axis=-1, keepdims=True)
    return (x * lax.rsqrt(ms + EPS) * g) * (1.0 + scale) + shift


def _post_residual(x, y, g, gate):
    ms = jnp.mean(y * y, axis=-1, keepdims=True)
    return x + gate * (y * lax.rsqrt(ms + EPS) * g)


def _ada_kernel(c_ref, w_ref, b_ref, o_ref):
    c = c_ref[...]
    c_act = jnp.broadcast_to(c * jax.nn.sigmoid(c), (SUBLANES, c.shape[1]))
    c_hi, c_lo = _split_bf16(c_act)
    w_hi, w_lo = _split_bf16(w_ref[...])
    m = _dot(c_hi, w_hi) + (_dot(c_hi, w_lo) + _dot(c_lo, w_hi))
    o_ref[...] = m[0:1] + b_ref[...]


def _ada(c, w, b):
    n, d, nn = w.shape
    out = pl.pallas_call(
        _ada_kernel,
        grid=(n, nn // d),
        in_specs=[pl.BlockSpec((1, d), lambda i, j: (0, 0)),
                  pl.BlockSpec((None, d, d), lambda i, j: (i, 0, j)),
                  pl.BlockSpec((None, 1, d), lambda i, j: (i, 0, j))],
        out_specs=pl.BlockSpec((None, 1, d), lambda i, j: (i, 0, j)),
        out_shape=jax.ShapeDtypeStruct((n, 1, nn), F32),
        compiler_params=_params("arbitrary", "arbitrary"),
        name="ada",
    )(c, w, b.reshape(n, 1, nn))
    return out.reshape(n, nn)


def _cast_specs(cast, steps):
    in_specs, out_specs, out_shapes = [], [], []
    for w, part, n_parts in cast:
        rows, cols = w.shape[0] // n_parts, w.shape[1]
        assert w.shape[0] % n_parts == 0 and rows % (steps * 2 * SUBLANES) == 0 and cols % LANES == 0
        in_specs.append(pl.BlockSpec((rows // steps, cols), lambda i, first=part * steps: (first + i, 0)))
        out_specs.append(pl.BlockSpec((rows // steps, cols), lambda i: (i, 0)))
        out_shapes.append(jax.ShapeDtypeStruct((rows, cols), BF16))
    return in_specs, out_specs, out_shapes


def _cast_slabs(src_refs, dst_refs):
    for src, dst in zip(src_refs, dst_refs):
        dst[...] = src[...].astype(BF16)


def _gmlp_kernel(n_cast, x_ref, vec_ref, win_ref, ln_ref, ws_ref, bst_ref, wout_ref, *refs):
    cast_refs, o_ref, cast_out_refs, gated_ref = refs[:n_cast], refs[n_cast], refs[n_cast + 1:-1], refs[-1]
    _cast_slabs(cast_refs, cast_out_refs)
    x = x_ref[...]
    vec = vec_ref[...]
    h = _prenorm(x, vec[0:1], vec[1:2], vec[2:3]).astype(BF16)
    z = jax.nn.gelu(_dot(h, win_ref[...]))
    width = z.shape[1] // 2
    u = z[:, :width]
    v = z[:, width:]
    mu = jnp.mean(v, axis=-1, keepdims=True)
    vc = v - mu
    var = jnp.mean(vc * vc, axis=-1, keepdims=True)
    ln = ln_ref[...]
    vn = (vc * lax.rsqrt(var + EPS) * ln[0:1] + ln[1:2]).astype(BF16)
    causal = (lax.broadcasted_iota(I32, (CHUNK, CHUNK), 0)
              >= lax.broadcasted_iota(I32, (CHUNK, CHUNK), 1))
    gw = width // A_GROUPS
    bst = bst_ref[...]
    for g in range(A_GROUPS):
        wg = jnp.where(causal, ws_ref[g], 0.0).astype(BF16)
        for ck in range(x.shape[0] // CHUNK):
            rows = slice(ck * CHUNK, (ck + 1) * CHUNK)
            cols = slice(g * gw, (g + 1) * gw)
            mixed = _dot(wg, vn[rows, cols]) + bst[:, g:g + 1]
            gated_ref[rows, cols] = (u[rows, cols] * mixed).astype(BF16)
    y = _dot(gated_ref[...], wout_ref[...])
    o_ref[...] = _post_residual(x, y, vec[3:4], vec[4:5])


def _gmlp_layer(x, vec, w_in, ln, ws, bst, w_out, cast=()):
    s, d = x.shape
    e2 = w_in.shape[1]
    steps = s // ROW_TILE
    cast_in, cast_out_specs, cast_shapes = _cast_specs(cast, steps)
    out, *cast_out = pl.pallas_call(
        functools.partial(_gmlp_kernel, len(cast)),
        grid=(steps,),
        in_specs=[pl.BlockSpec((ROW_TILE, d), lambda i: (i, 0)),
                  _resident((SUBLANES, d), lambda i: (0, 0)),
                  _resident((d, e2), lambda i: (0, 0)),
                  _resident((2, e2 // 2), lambda i: (0, 0)),
                  _resident(ws.shape, lambda i: (0, 0, 0)),
                  _resident(bst.shape, lambda i: (0, 0)),
                  _resident((e2 // 2, d), lambda i: (0, 0))] + cast_in,
        out_specs=[pl.BlockSpec((ROW_TILE, d), lambda i: (i, 0))] + cast_out_specs,
        out_shape=[jax.ShapeDtypeStruct((s, d), F32)] + cast_shapes,
        scratch_shapes=[pltpu.VMEM((ROW_TILE, e2 // 2), BF16)],
        compiler_params=_params("arbitrary"),
        name="gmlp",
    )(x, vec, w_in, ln, ws, bst, w_out, *[w for w, _, _ in cast])
    return out, cast_out


def _attn_residual(ot_ref, x_ref, avec_ref, wo_ref):
    avec = avec_ref[...]
    return _post_residual(x_ref[...], _dot_tn(ot_ref[...], wo_ref[...]), avec[3:4], avec[4:5])


def _attn_specs(o_t, d):
    nqd = o_t.shape[0]
    return [pl.BlockSpec((nqd, ROW_TILE), lambda i: (0, i)),
            pl.BlockSpec((ROW_TILE, d), lambda i: (i, 0)),
            _resident((SUBLANES, d), lambda i: (0, 0)),
            _resident((nqd, d), lambda i: (0, 0))]


def _swiglu_rows(x, vec, wg_ref, wu_ref, wd_ref):
    h = _prenorm(x, vec[0:1], vec[1:2], vec[2:3]).astype(BF16)
    g = _dot(h, wg_ref[...])
    a = (g * jax.nn.sigmoid(g) * _dot(h, wu_ref[...])).astype(BF16)
    y = _dot(a, wd_ref[...])
    return _post_residual(x, y, vec[3:4], vec[4:5])


def _swiglu_kernel(n_cast, x_ref, vec_ref, wg_ref, wu_ref, wd_ref, *refs):
    _cast_slabs(refs[:n_cast], refs[n_cast + 1:])
    refs[n_cast][...] = _swiglu_rows(x_ref[...], vec_ref[...], wg_ref, wu_ref, wd_ref)


def _attn_swiglu_kernel(n_cast, ot_ref, x_ref, avec_ref, wo_ref, vec_ref, wg_ref, wu_ref, wd_ref, *refs):
    _cast_slabs(refs[:n_cast], refs[n_cast + 1:])
    x = _attn_residual(ot_ref, x_ref, avec_ref, wo_ref)
    refs[n_cast][...] = _swiglu_rows(x, vec_ref[...], wg_ref, wu_ref, wd_ref)


def _swiglu_layer(x, vec, w_gate, w_up, w_down, attn=None, cast=()):
    s, d = x.shape
    f = w_gate.shape[1]
    steps = s // ROW_TILE
    cast_in, cast_out_specs, cast_shapes = _cast_specs(cast, steps)
    own_specs = [_resident((SUBLANES, d), lambda i: (0, 0)),
                 _resident((d, f), lambda i: (0, 0)),
                 _resident((d, f), lambda i: (0, 0)),
                 _resident((f, d), lambda i: (0, 0))]
    if attn is None:
        body, in_specs, args = _swiglu_kernel, [pl.BlockSpec((ROW_TILE, d), lambda i: (i, 0))], (x,)
    else:
        o_t, avec, w_out = attn
        body, in_specs, args = _attn_swiglu_kernel, _attn_specs(o_t, d), (o_t, x, avec, w_out)
    out, *cast_out = pl.pallas_call(
        functools.partial(body, len(cast)),
        grid=(steps,),
        in_specs=in_specs + own_specs + cast_in,
        out_specs=[pl.BlockSpec((ROW_TILE, d), lambda i: (i, 0))] + cast_out_specs,
        out_shape=[jax.ShapeDtypeStruct((s, d), F32)] + cast_shapes,
        compiler_params=_params("arbitrary"),
        name="swiglu",
    )(*args, vec, w_gate, w_up, w_down, *[w for w, _, _ in cast])
    return out, cast_out


def _router_kernel(x_ref, vec_ref, wrt_ref, br_ref, h_ref, idx_ref, gate_ref):
    _route_rows(x_ref[...], vec_ref[...], wrt_ref, br_ref, h_ref, idx_ref, gate_ref)


def _attn_router_kernel(ot_ref, x_ref, avec_ref, wo_ref, vec_ref, wrt_ref, br_ref, xo_ref, h_ref, idx_ref, gate_ref):
    x = _attn_residual(ot_ref, x_ref, avec_ref, wo_ref)
    xo_ref[...] = x
    _route_rows(x, vec_ref[...], wrt_ref, br_ref, h_ref, idx_ref, gate_ref)


def _route_rows(x, vec, wrt_ref, br_ref, h_ref, idx_ref, gate_ref):
    h = _prenorm(x, vec[0:1], vec[1:2], vec[2:3])
    h_ref[...] = h.astype(BF16)
    h_hi, h_lo = _split_bf16(h)
    w_hi, w_lo = _split_bf16(wrt_ref[...])
    logit = _dot_nt(w_hi, h_hi) + (_dot_nt(w_hi, h_lo) + _dot_nt(w_lo, h_hi)) + br_ref[...]
    ne = logit.shape[0]
    eidx = lax.broadcasted_iota(I32, logit.shape, 0)
    m1 = jnp.max(logit, axis=0, keepdims=True)
    i1 = jnp.min(jnp.where(logit == m1, eidx, ne), axis=0, keepdims=True)
    rest = jnp.where(eidx == i1, -jnp.inf, logit)
    m2 = jnp.max(rest, axis=0, keepdims=True)
    i2 = jnp.min(jnp.where(rest == m2, eidx, ne), axis=0, keepdims=True)
    e21 = jnp.exp(m2 - m1)
    g1 = 1.0 / (1.0 + e21)
    g2 = e21 * g1
    pad_i = jnp.zeros((SUBLANES - 2, i1.shape[1]), I32)
    pad_f = jnp.zeros((SUBLANES - 2, i1.shape[1]), F32)
    idx_ref[...] = jnp.concatenate([i1, i2, pad_i], axis=0)
    gate_ref[...] = jnp.concatenate([g1, g2, pad_f], axis=0)


def _router(x, vec, w_rt, b_r, attn=None):
    s, d = x.shape
    ne = w_rt.shape[0]
    own_specs = [_resident((SUBLANES, d), lambda i: (0, 0)),
                 _resident((ne, d), lambda i: (0, 0)),
                 _resident((ne, 1), lambda i: (0, 0))]
    out_specs = [pl.BlockSpec((ROW_TILE, d), lambda i: (i, 0)),
                 pl.BlockSpec((SUBLANES, ROW_TILE), lambda i: (0, i)),
                 pl.BlockSpec((SUBLANES, ROW_TILE), lambda i: (0, i))]
    out_shape = [jax.ShapeDtypeStruct((s, d), BF16),
                 jax.ShapeDtypeStruct((SUBLANES, s), I32),
                 jax.ShapeDtypeStruct((SUBLANES, s), F32)]
    if attn is None:
        h, idx, gates = pl.pallas_call(
            _router_kernel,
            grid=(s // ROW_TILE,),
            in_specs=[pl.BlockSpec((ROW_TILE, d), lambda i: (i, 0))] + own_specs,
            out_specs=out_specs,
            out_shape=out_shape,
            compiler_params=_params("arbitrary"),
            name="moe_router",
        )(x, vec, w_rt, b_r)
        return x, h, idx, gates
    o_t, avec, w_out = attn
    return pl.pallas_call(
        _attn_router_kernel,
        grid=(s // ROW_TILE,),
        in_specs=_attn_specs(o_t, d) + own_specs,
        out_specs=[pl.BlockSpec((ROW_TILE, d), lambda i: (i, 0))] + out_specs,
        out_shape=[jax.ShapeDtypeStruct((s, d), F32)] + out_shape,
        compiler_params=_params("arbitrary"),
        name="moe_router",
    )(o_t, x, avec, w_out, vec, w_rt, b_r)


def _rank_kernel(idx_ref, rank_ref, start_ref, count_ref, carry_ref):
    @pl.when(pl.program_id(0) == 0)
    def _():
        carry_ref[...] = jnp.zeros_like(carry_ref)

    idx = idx_ref[...]
    tt = idx.shape[1]
    eidx = lax.broadcasted_iota(I32, (N_EXPERTS, tt), 0)
    hit1 = eidx == idx[0:1]
    hit2 = eidx == idx[1:2]
    member = jnp.where(hit1 | hit2, 1.0, 0.0)
    before = (lax.broadcasted_iota(I32, (tt, tt), 0) < lax.broadcasted_iota(I32, (tt, tt), 1))
    carry = carry_ref[...]
    cum = _dot(member.astype(BF16), jnp.where(before, 1.0, 0.0).astype(BF16)) + carry[:, 0:1]
    r1 = jnp.sum(jnp.where(hit1, cum, 0.0), axis=0, keepdims=True)
    r2 = jnp.sum(jnp.where(hit2, cum, 0.0), axis=0, keepdims=True)
    rank_ref[...] = jnp.concatenate([r1, r2, jnp.zeros((SUBLANES - 2, tt), F32)], axis=0).astype(I32)
    tile_count = jnp.broadcast_to(jnp.sum(member, axis=1, keepdims=True), carry.shape)
    start_ref[...] = carry
    count_ref[...] = tile_count
    carry_ref[...] = carry + tile_count


def _ranks(idx):
    s = idx.shape[1]
    nt = s // TOK_TILE
    return pl.pallas_call(
        _rank_kernel,
        grid=(nt,),
        in_specs=[pl.BlockSpec((SUBLANES, TOK_TILE), lambda i: (0, i))],
        out_specs=[pl.BlockSpec((SUBLANES, TOK_TILE), lambda i: (0, i)),
                   pl.BlockSpec((None, N_EXPERTS, LANES), lambda i: (i, 0, 0)),
                   pl.BlockSpec((None, N_EXPERTS, LANES), lambda i: (i, 0, 0))],
        out_shape=[jax.ShapeDtypeStruct((SUBLANES, s), I32),
                   jax.ShapeDtypeStruct((nt, N_EXPERTS, LANES), F32),
                   jax.ShapeDtypeStruct((nt, N_EXPERTS, LANES), F32)],
        scratch_shapes=[pltpu.VMEM((N_EXPERTS, LANES), F32)],
        compiler_params=_params("arbitrary"),
        name="moe_ranks",
    )(idx)


def _match(idx, rank, expert, rows):
    r1 = jnp.where(idx[0:1] == expert, rank[0:1], -1)
    r2 = jnp.where(idx[1:2] == expert, rank[1:2], -1)
    return rows == r1, rows == r2


def _dispatch_kernel(be_ref, lb_ref, tlo_ref, thi_ref, idx_ref, rank_ref, gate_ref, h_ref, o_ref, rg_ref,
                     acc_ref, gacc_ref):
    b = pl.program_id(0)
    expert = be_ref[b]
    rows = lax.broadcasted_iota(I32, (MOE_BLOCK, DISP_TILE), 0) + lb_ref[b]
    acc_ref[...] = jnp.zeros_like(acc_ref)
    gacc_ref[...] = jnp.zeros_like(gacc_ref)

    def body(t, carry):
        off = pl.multiple_of(t * DISP_TILE, DISP_TILE)
        m1, m2 = _match(idx_ref[:, pl.ds(off, DISP_TILE)], rank_ref[:, pl.ds(off, DISP_TILE)], expert, rows)
        gates = gate_ref[:, pl.ds(off, DISP_TILE)]
        gacc_ref[...] += jnp.sum(jnp.where(m1, gates[0:1], 0.0) + jnp.where(m2, gates[1:2], 0.0),
                                 axis=1, keepdims=True)
        onehot = jnp.where(m1 | m2, 1.0, 0.0).astype(BF16)
        acc_ref[...] += _dot(onehot, h_ref[pl.ds(off, DISP_TILE), :])
        return carry

    lax.fori_loop(tlo_ref[b], thi_ref[b], body, 0)
    o_ref[...] = acc_ref[...].astype(BF16)
    rg_ref[...] = gacc_ref[...]


def _dispatch(blk_e, blk_lb, blk_tlo, blk_thi, idx, rank, gates, h):
    s, d = h.shape
    n_blk = blk_e.shape[0]
    grid_spec = pltpu.PrefetchScalarGridSpec(
        num_scalar_prefetch=4,
        grid=(n_blk,),
        in_specs=[_resident((SUBLANES, s), lambda b, *_: (0, 0)),
                  _resident((SUBLANES, s), lambda b, *_: (0, 0)),
                  _resident((SUBLANES, s), lambda b, *_: (0, 0)),
                  _resident((s, d), lambda b, *_: (0, 0))],
        out_specs=[pl.BlockSpec((MOE_BLOCK, d), lambda b, *_: (b, 0)),
                   pl.BlockSpec((MOE_BLOCK, 1), lambda b, *_: (b, 0))],
        scratch_shapes=[pltpu.VMEM((MOE_BLOCK, d), F32), pltpu.VMEM((MOE_BLOCK, 1), F32)],
    )
    return pl.pallas_call(
        _dispatch_kernel,
        grid_spec=grid_spec,
        out_shape=[jax.ShapeDtypeStruct((n_blk * MOE_BLOCK, d), BF16),
                   jax.ShapeDtypeStruct((n_blk * MOE_BLOCK, 1), F32)],
        compiler_params=_params("arbitrary"),
        name="moe_dispatch",
    )(blk_e, blk_lb, blk_tlo, blk_thi, idx, rank, gates, h)


def _expert_kernel(be_ref, used_ref, x_ref, rg_ref, wg_ref, wu_ref, wd_ref, o_ref):
    b = pl.program_id(0)

    @pl.when(b < used_ref[0])
    def _():
        x = x_ref[...]
        f = wg_ref.shape[1]
        y = jnp.zeros(o_ref.shape, F32)
        for c in range(f // F_CHUNK):
            cols = slice(c * F_CHUNK, (c + 1) * F_CHUNK)
            g = _dot(x, wg_ref[:, cols])
            a = (g * jax.nn.sigmoid(g) * _dot(x, wu_ref[:, cols])).astype(BF16)
            y = y + _dot(a, wd_ref[cols, :])
        o_ref[...] = (y * rg_ref[...]).astype(BF16)

    @pl.when(b >= used_ref[0])
    def _():
        o_ref[...] = jnp.zeros_like(o_ref)


def _experts(blk_e, n_used, x_buf, row_gate, w_gate, w_up, w_down):
    n_rows, d = x_buf.shape
    f = w_gate.shape[2]
    assert f % F_CHUNK == 0
    grid_spec = pltpu.PrefetchScalarGridSpec(
        num_scalar_prefetch=2,
        grid=(n_rows // MOE_BLOCK,),
        in_specs=[pl.BlockSpec((MOE_BLOCK, d), lambda b, be, nu: (b, 0)),
                  pl.BlockSpec((MOE_BLOCK, 1), lambda b, be, nu: (b, 0)),
                  pl.BlockSpec((None, d, f), lambda b, be, nu: (be[b], 0, 0)),
                  pl.BlockSpec((None, d, f), lambda b, be, nu: (be[b], 0, 0)),
                  pl.BlockSpec((None, f, d), lambda b, be, nu: (be[b], 0, 0))],
        out_specs=pl.BlockSpec((MOE_BLOCK, d), lambda b, be, nu: (b, 0)),
    )
    return pl.pallas_call(
        _expert_kernel,
        grid_spec=grid_spec,
        out_shape=jax.ShapeDtypeStruct((n_rows, d), BF16),
        compiler_params=_params("arbitrary"),
        name="moe_experts",
    )(blk_e, n_used, x_buf, row_gate, w_gate, w_up, w_down)


def _combine_kernel(win_ref, lb_ref, tail_ref, idx_ref, rank_ref, *refs):
    y_refs = refs[:N_EXPERTS * WIN_PARTS]
    x_ref, vec_ref, o_ref, acc_ref = refs[N_EXPERTS * WIN_PARTS:]
    t = pl.program_id(0)
    idx = idx_ref[...]
    rank = rank_ref[...]
    main_rows = (WIN_PARTS - 1) * WIN_PART_ROWS

    def gathered(e, first_part, n_parts):
        n_rows = n_parts * WIN_PART_ROWS
        rows = (lax.broadcasted_iota(I32, (n_rows, TOK_TILE), 0)
                + (lb_ref[t * N_EXPERTS + e] + first_part * WIN_PART_ROWS))
        m1, m2 = _match(idx, rank, e, rows)
        onehot = jnp.where(m1 | m2, 1.0, 0.0).astype(BF16)
        y = jnp.concatenate([y_refs[e * WIN_PARTS + first_part + k][...] for k in range(n_parts)], axis=0)
        return _dot_tn(onehot, y)

    acc = jnp.zeros(o_ref.shape, F32)
    for e in range(N_EXPERTS):
        acc = acc + gathered(e, 0, WIN_PARTS - 1)
    acc_ref[...] = acc
    for e in range(N_EXPERTS):
        @pl.when(tail_ref[t * N_EXPERTS + e] > 0)
        def _():
            acc_ref[...] += gathered(e, WIN_PARTS - 1, 1)
    vec = vec_ref[...]
    o_ref[...] = _post_residual(x_ref[...], acc_ref[...], vec[3:4], vec[4:5])


def _combine(win_start, win_lb, win_tail, idx, rank, y_buf, x, vec):
    s, d = x.shape

    def y_spec(e, k):
        return pl.BlockSpec((WIN_PART_ROWS, d), lambda t, ws, lb, tl: (ws[t * N_EXPERTS + e] + k, 0))

    grid_spec = pltpu.PrefetchScalarGridSpec(
        num_scalar_prefetch=3,
        grid=(s // TOK_TILE,),
        in_specs=[pl.BlockSpec((SUBLANES, TOK_TILE), lambda t, *_: (0, t)),
                  pl.BlockSpec((SUBLANES, TOK_TILE), lambda t, *_: (0, t))]
                 + [y_spec(e, k) for e in range(N_EXPERTS) for k in range(WIN_PARTS)]
                 + [pl.BlockSpec((TOK_TILE, d), lambda t, *_: (t, 0)),
                    _resident((SUBLANES, d), lambda t, *_: (0, 0))],
        out_specs=pl.BlockSpec((TOK_TILE, d), lambda t, *_: (t, 0)),
        scratch_shapes=[pltpu.VMEM((TOK_TILE, d), F32)],
    )
    return pl.pallas_call(
        _combine_kernel,
        grid_spec=grid_spec,
        out_shape=jax.ShapeDtypeStruct((s, d), F32),
        compiler_params=_params("arbitrary"),
        name="moe_combine",
    )(win_start, win_lb, win_tail, idx, rank, *([y_buf] * (N_EXPERTS * WIN_PARTS)), x, vec)


def _moe_layer(x, vec, w_rt, b_r, w_gate, w_up, w_down, attn=None):
    s, d = x.shape
    nt = s // TOK_TILE
    x, h, idx, gates = _router(x, vec, w_rt, b_r, attn)
    rank, tile_start, tile_count = _ranks(idx)

    tile_start = tile_start[:, :, 0].astype(I32)
    tile_count = tile_count[:, :, 0].astype(I32)
    tile_end = tile_start + tile_count
    counts = tile_end[-1]
    padded = (counts + MOE_BLOCK - 1) // MOE_BLOCK * MOE_BLOCK
    pend = jnp.cumsum(padded)
    pstart = pend - padded
    n_rows = -(-(2 * s) // MOE_BLOCK) * MOE_BLOCK + N_EXPERTS * MOE_BLOCK
    n_blk = n_rows // MOE_BLOCK
    blk_row = jnp.arange(n_blk, dtype=I32) * MOE_BLOCK
    blk_e = jnp.minimum(jnp.sum(blk_row[:, None] >= pend[None, :], axis=1), N_EXPERTS - 1).astype(I32)
    blk_lb = blk_row - pstart[blk_e]
    per = DISP_TILE // TOK_TILE
    te = tile_end[per - 1::per][:, blk_e]
    ts = tile_start[::per][:, blk_e]
    blk_tlo = jnp.sum(te <= blk_lb[None, :], axis=0).astype(I32)
    blk_thi = jnp.sum(ts < (blk_lb + MOE_BLOCK)[None, :], axis=0).astype(I32)
    n_used = (pend[-1] // MOE_BLOCK).astype(I32).reshape(1)

    x_buf, row_gate = _dispatch(blk_e, blk_lb.astype(I32), blk_tlo, blk_thi, idx, rank, gates, h)
    y_buf = _experts(blk_e, n_used, x_buf, row_gate, w_gate, w_up, w_down)

    win_start = jnp.minimum((pstart[None, :] + tile_start) // WIN_PART_ROWS, n_rows // WIN_PART_ROWS - WIN_PARTS)
    win_lb = win_start * WIN_PART_ROWS - pstart[None, :]
    win_tail = tile_end - win_lb > (WIN_PARTS - 1) * WIN_PART_ROWS
    return _combine(win_start.reshape(-1).astype(I32), win_lb.reshape(-1).astype(I32),
                    win_tail.reshape(-1).astype(I32), idx, rank, y_buf, x, vec)


def _kvproj_kernel(x_ref, vec_ref, wkv_ref, wvt_ref, raw_ref, ks_ref, kw_ref, vst_ref, vwt_ref):
    x = x_ref[...]
    vec = vec_ref[...]
    h = _prenorm(x, vec[0:1], vec[1:2], vec[2:3]).astype(BF16)
    kv = _dot(h, wkv_ref[...])
    vt = _dot_nt(wvt_ref[...], h)
    tm = x.shape[0]
    gd = N_KV * HEAD_DIM
    key_blk = jnp.right_shift(pl.program_id(0) * tm + lax.broadcasted_iota(I32, (tm, SEL_BLKS), 0), SEL_SHIFT)
    ind = jnp.where((key_blk & (SEL_BLKS - 1)) == lax.broadcasted_iota(I32, (tm, SEL_BLKS), 1), 1.0, 0.0)
    pad_s = jnp.zeros((tm, LANES - HEAD_DIM - SEL_BLKS), F32)
    pad_w = jnp.zeros((tm, LANES - HEAD_DIM), F32)
    ones_row = jnp.concatenate([jnp.ones((1, tm), F32), jnp.zeros((V_ROWS - HEAD_DIM - 1, tm), F32)], axis=0)
    for g in range(N_KV):
        c = g * HEAD_DIM
        raw_ref[0, g] = kv[:, c:c + HEAD_DIM]
        raw_ref[1, g] = kv[:, gd + c:gd + c + HEAD_DIM]
        ks = kv[:, 2 * gd + c:2 * gd + c + HEAD_DIM]
        kw = kv[:, 4 * gd + c:4 * gd + c + HEAD_DIM]
        ks_ref[g] = jnp.concatenate([ks, ind, pad_s], axis=1).astype(BF16)
        kw_ref[g] = jnp.concatenate([kw, pad_w], axis=1).astype(BF16)
        vst_ref[g] = jnp.concatenate([vt[c:c + HEAD_DIM], ones_row], axis=0).astype(BF16)
        vwt_ref[g] = jnp.concatenate([vt[gd + c:gd + c + HEAD_DIM], ones_row], axis=0).astype(BF16)


def _kvproj(x, vec, w_kv, w_vt):
    s, d = x.shape
    nkv = w_kv.shape[1]
    return pl.pallas_call(
        _kvproj_kernel,
        grid=(s // ROW_TILE,),
        in_specs=[pl.BlockSpec((ROW_TILE, d), lambda i: (i, 0)),
                  _resident((SUBLANES, d), lambda i: (0, 0)),
                  _resident((d, nkv), lambda i: (0, 0)),
                  _resident(w_vt.shape, lambda i: (0, 0))],
        out_specs=[pl.BlockSpec((2, N_KV, ROW_TILE, HEAD_DIM), lambda i: (0, 0, i, 0)),
                   pl.BlockSpec((N_KV, ROW_TILE, LANES), lambda i: (0, i, 0)),
                   pl.BlockSpec((N_KV, ROW_TILE, LANES), lambda i: (0, i, 0)),
                   pl.BlockSpec((N_KV, V_ROWS, ROW_TILE), lambda i: (0, 0, i)),
                   pl.BlockSpec((N_KV, V_ROWS, ROW_TILE), lambda i: (0, 0, i))],
        out_shape=[jax.ShapeDtypeStruct((2, N_KV, s, HEAD_DIM), F32),
                   jax.ShapeDtypeStruct((N_KV, s, LANES), BF16),
                   jax.ShapeDtypeStruct((N_KV, s, LANES), BF16),
                   jax.ShapeDtypeStruct((N_KV, V_ROWS, s), BF16),
                   jax.ShapeDtypeStruct((N_KV, V_ROWS, s), BF16)],
        compiler_params=_params("arbitrary"),
        name="nsa_kvproj",
    )(x, vec, w_kv, w_vt)


def _compress_kernel(raw_ref, pos_ref, w1_ref, b1_ref, w2_ref, b2_ref, n_ref, t_ref):
    raw = raw_ref[...]
    nc = raw.shape[0]
    first = _dot((raw + pos_ref[0]).astype(BF16), w1_ref[0])
    second = _dot((raw + pos_ref[1]).astype(BF16), w1_ref[1])
    hid = jax.nn.gelu(first + pltpu.roll(second, nc - 1, 0) + b1_ref[...])
    out = _dot(hid.astype(BF16), w2_ref[...]) + b2_ref[...]
    n_ref[...] = out.astype(BF16)
    row = lax.broadcasted_iota(I32, (LANES, nc), 0)
    t_ref[...] = jnp.where(row == HEAD_DIM, 1.0, out.T).astype(BF16)


def _compress(raw, pos, w1, b1, w2, b2):
    _, g, nc, width = raw.shape
    hid = w1.shape[-1]
    return pl.pallas_call(
        _compress_kernel,
        grid=(2, g),
        in_specs=[pl.BlockSpec((None, None, nc, width), lambda j, k: (j, k, 0, 0)),
                  pl.BlockSpec((None, 2, 1, width), lambda j, k: (j, 0, 0, 0)),
                  pl.BlockSpec((None, 2, width, hid), lambda j, k: (j, 0, 0, 0)),
                  pl.BlockSpec((None, 1, hid), lambda j, k: (j, 0, 0)),
                  pl.BlockSpec((None, hid, LANES), lambda j, k: (j, 0, 0)),
                  pl.BlockSpec((None, 1, LANES), lambda j, k: (j, 0, 0))],
        out_specs=[pl.BlockSpec((None, None, nc, LANES), lambda j, k: (j, k, 0, 0)),
                   pl.BlockSpec((None, None, LANES, nc), lambda j, k: (j, k, 0, 0))],
        out_shape=[jax.ShapeDtypeStruct((2, g, nc, LANES), BF16),
                   jax.ShapeDtypeStruct((2, g, LANES, nc), BF16)],
        compiler_params=_params("arbitrary", "arbitrary"),
        name="nsa_compress",
    )(raw, pos, w1, b1, w2, b2)


def _qproj_kernel(x_ref, vec_ref, wt_ref, q_ref, gate_ref):
    x = x_ref[...]
    vec = vec_ref[...]
    h = _prenorm(x, vec[0:1], vec[1:2], vec[2:3]).astype(BF16)
    pt = _dot_nt(wt_ref[...], h)
    nq = N_HEADS * HEAD_DIM
    q_ref[...] = (pt[:nq] * (HEAD_DIM ** -0.5 * LOG2E)).astype(BF16)
    gates = jax.nn.sigmoid(pt[nq:nq + 3 * N_HEADS])
    per = 3 * HPG
    pad = jnp.zeros((GATE_ROWS - per, x.shape[0]), F32)
    for g in range(N_KV):
        gate_ref[g] = jnp.concatenate([gates[g * per:(g + 1) * per], pad], axis=0)


def _qproj(x, vec, w_t):
    s, d = x.shape
    return pl.pallas_call(
        _qproj_kernel,
        grid=(s // ROW_TILE,),
        in_specs=[pl.BlockSpec((ROW_TILE, d), lambda i: (i, 0)),
                  _resident((SUBLANES, d), lambda i: (0, 0)),
                  _resident(w_t.shape, lambda i: (0, 0))],
        out_specs=[pl.BlockSpec((N_HEADS * HEAD_DIM, ROW_TILE), lambda i: (0, i)),
                   pl.BlockSpec((N_KV, GATE_ROWS, ROW_TILE), lambda i: (0, 0, i))],
        out_shape=[jax.ShapeDtypeStruct((N_HEADS * HEAD_DIM, s), BF16),
                   jax.ShapeDtypeStruct((N_KV, GATE_ROWS, s), F32)],
        compiler_params=_params("arbitrary"),
        name="nsa_qproj",
    )(x, vec, w_t)


def _attn_kernel(q_ref, gate_ref, cmask_ref, wmask_ref, kc_ref, vct_ref, ks_ref, vst_ref, kw_ref, vwt_ref,
                 o_ref, bias_ref, ps_ref, sc_ref, *sp_refs):
    s_refs = [[sp_refs[2 * g + k] for k in range(2)] for g in range(ATT_GROUPS)]
    p_refs = [[sp_refs[2 * ATT_GROUPS + 2 * g + k] for k in range(2)] for g in range(ATT_GROUPS)]
    i = pl.program_id(1)
    nq = HPG * Q_BLK
    nc = kc_ref.shape[1]
    nb = bias_ref.shape[1]
    rows = HPG * HEAD_DIM
    t1 = i * Q_BLK + lax.broadcasted_iota(I32, (1, Q_BLK), 1)
    t4 = jnp.concatenate([t1] * HPG, axis=1)
    j_io = lax.broadcasted_iota(I32, (nb, Q_BLK), 0)
    jt = jnp.right_shift(t1, SEL_SHIFT)
    forced = (j_io == 0) | (j_io == jt) | (j_io == jt - 1)
    cand_off = jnp.where((j_io * L_SEL <= t1) & jnp.logical_not(forced), 0.0, NEG)
    past_off = jnp.where(j_io * L_SEL < i * Q_BLK, 0.0, NEG)
    doff = pl.multiple_of(i * Q_BLK, Q_BLK)
    cmask_off = nc - i * (Q_BLK // D_CMP)
    span = WINDOW + Q_BLK
    wstart = pl.multiple_of(jnp.maximum(i * Q_BLK - WINDOW, 0), Q_BLK)
    win_mask = wmask_ref[pl.ds(pl.multiple_of(WINDOW - jnp.minimum(i * Q_BLK, WINDOW), Q_BLK), span), :]
    win_mask = jnp.concatenate([win_mask] * HPG, axis=1)
    diag_mask = jnp.concatenate([wmask_ref[WINDOW:WINDOW + Q_BLK, :]] * HPG, axis=1)
    vpad = jnp.zeros((LANES - HEAD_DIM - 2 * SEL_BLKS, nq), BF16)
    bpad = jnp.zeros((SEL_BLKS, nq), F32)

    def aligned(x, m):
        return x if isinstance(x, int) else pl.multiple_of(x, m)

    groups = range(ATT_GROUPS)
    q4s, q_plains = [], []
    for g in groups:
        qb = q_ref[g * rows:(g + 1) * rows, :]
        q4s.append(jnp.concatenate([qb[h * HEAD_DIM:(h + 1) * HEAD_DIM] for h in range(HPG)], axis=1))
        q_plains.append(jnp.concatenate([q4s[g], jnp.zeros((LANES - HEAD_DIM, nq), BF16)], axis=0))

    def compressed(n_rows):
        chunks = [slice(r0, r0 + CMP_CHUNK) for r0 in range(0, n_rows, CMP_CHUNK)]
        ms = [jnp.full((1, nq), NEG, F32) for _ in groups]
        for rs in chunks:
            mask = cmask_ref[pl.ds(pl.multiple_of(cmask_off + rs.start, 8), CMP_CHUNK), :]
            mask = jnp.concatenate([mask] * HPG, axis=1)
            for g in groups:
                s = _dot(kc_ref[g, rs, :], q_plains[g]) + mask
                sc_ref[g, rs, :] = s
                ms[g] = jnp.maximum(ms[g], jnp.max(s, axis=0, keepdims=True))
        ls = [jnp.zeros((1, nq), F32) for _ in groups]
        for rs in chunks:
            for g in groups:
                e = jnp.exp2(sc_ref[g, rs, :] - ms[g])
                sc_ref[g, rs, :] = e
                ls[g] = ls[g] + jnp.sum(e, axis=0, keepdims=True)
        rls = [jnp.where(t4 >= L_CMP - 1, 1.0 / ls[g], 0.0) for g in groups]
        os_ = [jnp.zeros((HEAD_DIM, nq), F32) for _ in groups]
        for g in groups:
            ps_ref[g, 0:PS_PAD, :] = jnp.zeros((PS_PAD, Q_BLK), F32)
            if n_rows < nc:
                ps_ref[g, PS_PAD + n_rows:PS_PAD + nc, :] = jnp.zeros((nc - n_rows, Q_BLK), F32)
        for rs in chunks:
            for g in groups:
                pc = sc_ref[g, rs, :] * rls[g]
                os_[g] = os_[g] + _dot(vct_ref[g, 0:HEAD_DIM, rs], pc.astype(BF16))
                psum = pc[:, 0:Q_BLK]
                for h in range(1, HPG):
                    psum = psum + pc[:, h * Q_BLK:(h + 1) * Q_BLK]
                ps_ref[g, PS_PAD + rs.start:PS_PAD + rs.stop, :] = psum
        ratio = L_SEL // D_CMP
        imps = []
        for g in groups:
            imp = jnp.zeros((nb, Q_BLK), F32)
            for k in range(1 - L_CMP // D_CMP, ratio):
                imp = imp + ps_ref[g, pl.ds(PS_PAD + k, nb, stride=ratio), :]
            imps.append(imp)
        return tuple(os_) + tuple(imps)

    if nc % (2 * CMP_CHUNK) == 0:
        last_complete = ((i + 1) * Q_BLK - L_CMP) // D_CMP
        cmp_out = lax.cond(last_complete < nc // 2, lambda: compressed(nc // 2), lambda: compressed(nc))
    else:
        cmp_out = compressed(nc)
    o_cmps, imps = cmp_out[:ATT_GROUPS], cmp_out[ATT_GROUPS:]

    def select(imps):
        w = jnp.concatenate([imp + cand_off for imp in imps], axis=1)
        j_all = lax.broadcasted_iota(I32, w.shape, 0).astype(F32)
        for _ in range(N_SELECT - N_FORCED):
            m = jnp.max(w, axis=0, keepdims=True)
            first = jnp.min(jnp.where(w == m, j_all, float(nb)), axis=0, keepdims=True)
            first = jnp.where(m > 0.5 * NEG, first, -1.0)
            w = jnp.where(j_all == first, NEG, w)
        for g in range(ATT_GROUPS):
            bias1 = jnp.where(w[:, g * Q_BLK:(g + 1) * Q_BLK] < 0.5 * NEG, past_off, NEG)
            bias_ref[g] = jnp.concatenate([bias1] * HPG, axis=1)

    def local():
        sds = [_dot(ks_ref[g, pl.ds(doff, Q_BLK), :], q_plains[g]) + diag_mask for g in groups]
        sws = [_dot(kw_ref[g, pl.ds(wstart, span), :], q_plains[g]) + win_mask for g in groups]
        out = []
        for g in groups:
            m0 = jnp.max(sds[g], axis=0, keepdims=True)
            acc0 = _dot(vst_ref[g, :, pl.ds(doff, Q_BLK)], jnp.exp2(sds[g] - m0).astype(BF16))
            pw = jnp.exp2(sws[g] - jnp.max(sws[g], axis=0, keepdims=True)).astype(BF16)
            acc_w = _dot(vwt_ref[g, :, pl.ds(wstart, span)], pw)
            out.append((acc_w[0:HEAD_DIM] / acc_w[HEAD_DIM:HEAD_DIM + 1], m0, acc0))
        return out

    def query_operand(g, q4, c):
        brow = bias_ref[g, pl.ds(aligned(c * SEL_BLKS, SEL_BLKS), SEL_BLKS), :]
        b16 = jnp.concatenate([brow, bpad], axis=0).astype(BF16)
        return jnp.concatenate([q4, b16, vpad], axis=0)

    def phase(g, q4, c, slot, state, do_scores=True, do_softmax=True, do_values=True):
        m_run, alpha, acc, cmax = state
        other = 1 - slot
        if do_values:
            voff = aligned(c * SEL_CHUNK, SEL_CHUNK)
            acc = alpha * acc + _dot(vst_ref[g, :, pl.ds(voff, SEL_CHUNK)], p_refs[g][slot][...])
        if do_scores:
            koff = aligned((c + 2) * SEL_CHUNK, SEL_CHUNK)
            s = _dot(ks_ref[g, pl.ds(koff, SEL_CHUNK), :], query_operand(g, q4, c + 2))
            s_refs[g][slot][...] = s.astype(BF16)
            new_max = jnp.max(s, axis=0, keepdims=True)
        if do_softmax:
            m_new = jnp.maximum(m_run, cmax[other])
            p_refs[g][other][...] = jnp.exp2(s_refs[g][other][...] - m_new.astype(BF16))
            m_run, alpha = m_new, jnp.exp2(m_run - m_new)
        if do_scores:
            cmax = (new_max, cmax[1]) if slot == 0 else (cmax[0], new_max)
        return m_run, alpha, acc, cmax

    select(imps)
    heads = [(q4s[g], o_cmps[g]) + loc for g, loc in enumerate(local())]

    def pair(k, carries):
        out = []
        for g in range(ATT_GROUPS):
            q4 = heads[g][0]
            state = phase(g, q4, 2 * k, 0, carries[g])
            out.append(phase(g, q4, 2 * k + 1, 1, state))
        return tuple(out)

    n_pairs = jnp.maximum((i * Q_BLK + 2 * SEL_CHUNK - 1) // (2 * SEL_CHUNK), 1)
    init = []
    for g in range(ATT_GROUPS):
        q4, _, _, m0, acc0 = heads[g]
        neg_row = jnp.full((1, nq), NEG, F32)
        state = (m0, jnp.ones((1, nq), F32), acc0, (neg_row, neg_row))
        state = phase(g, q4, -2, 0, state, do_softmax=False, do_values=False)
        init.append(phase(g, q4, -1, 1, state, do_values=False))
    carries = lax.fori_loop(0, n_pairs - 1, pair, tuple(init))
    last = 2 * (n_pairs - 1)
    for g in range(ATT_GROUPS):
        q4, o_cmp, o_win, _, _ = heads[g]
        state = phase(g, q4, last, 0, carries[g], do_scores=False)
        _, _, acc_s, _ = phase(g, q4, last + 1, 1, state, do_scores=False, do_softmax=False)
        o_sel = acc_s[0:HEAD_DIM] / acc_s[HEAD_DIM:HEAD_DIM + 1]
        gates = gate_ref[g]
        outs = []
        for h in range(HPG):
            cols = slice(h * Q_BLK, (h + 1) * Q_BLK)
            outs.append(o_cmp[:, cols] * gates[3 * h:3 * h + 1]
                        + o_sel[:, cols] * gates[3 * h + 1:3 * h + 2]
                        + o_win[:, cols] * gates[3 * h + 2:3 * h + 3])
        o_ref[g * rows:(g + 1) * rows, :] = jnp.concatenate(outs, axis=0).astype(BF16)


def _attention(q_t, gate_t, kc, vct, ks, vst, kw, vwt):
    nqd, s = q_t.shape
    nc = kc.shape[1]
    nb = s // L_SEL
    nq = HPG * Q_BLK
    rows = ATT_GROUPS * HPG * HEAD_DIM
    assert (s // SEL_CHUNK) % 2 == 0 and s >= WINDOW + Q_BLK and N_KV % ATT_GROUPS == 0 and nb >= N_SELECT
    qq = jnp.arange(Q_BLK)[None, :]
    rc = jnp.arange(2 * nc)[:, None]
    cmp_mask = jnp.where(D_CMP * (rc - nc) + L_CMP - 1 <= qq, 0.0, NEG).astype(F32)
    rw = jnp.arange(2 * WINDOW + Q_BLK)[:, None]
    win_mask = jnp.where((qq < rw) & (rw <= qq + WINDOW), 0.0, NEG).astype(F32)
    return pl.pallas_call(
        _attn_kernel,
        grid=(N_KV // ATT_GROUPS, s // Q_BLK),
        in_specs=[pl.BlockSpec((rows, Q_BLK), lambda g, i: (g, i)),
                  pl.BlockSpec((ATT_GROUPS, GATE_ROWS, Q_BLK), lambda g, i: (g, 0, i)),
                  _resident(cmp_mask.shape, lambda g, i: (0, 0)),
                  _resident(win_mask.shape, lambda g, i: (0, 0)),
                  _resident((ATT_GROUPS, nc, LANES), lambda g, i: (g, 0, 0)),
                  _resident((ATT_GROUPS, LANES, nc), lambda g, i: (g, 0, 0)),
                  _resident((ATT_GROUPS, s, LANES), lambda g, i: (g, 0, 0)),
                  _resident((ATT_GROUPS, V_ROWS, s), lambda g, i: (g, 0, 0)),
                  _resident((ATT_GROUPS, s, LANES), lambda g, i: (g, 0, 0)),
                  _resident((ATT_GROUPS, V_ROWS, s), lambda g, i: (g, 0, 0))],
        out_specs=pl.BlockSpec((rows, Q_BLK), lambda g, i: (g, i)),
        out_shape=jax.ShapeDtypeStruct((nqd, s), BF16),
        scratch_shapes=[pltpu.VMEM((ATT_GROUPS, nb, nq), F32),
                        pltpu.VMEM((ATT_GROUPS, PS_PAD + nc, Q_BLK), F32),
                        pltpu.VMEM((ATT_GROUPS, nc, nq), F32),
                        *[pltpu.VMEM((SEL_CHUNK, nq), BF16) for _ in range(2 * ATT_GROUPS)],
                        *[pltpu.VMEM((SEL_CHUNK, nq), BF16) for _ in range(2 * ATT_GROUPS)]],
        compiler_params=_params("arbitrary", "arbitrary"),
        name="nsa_attention",
    )(q_t, gate_t, cmp_mask, win_mask, kc, vct, ks, vst, kw, vwt)


def _vec(pre_g, mod, post_g):
    d = pre_g.shape[0]
    shift, scale, gate = mod[:d], mod[d:2 * d], mod[2 * d:3 * d]
    rows = [pre_g, scale, shift, post_g, gate]
    return jnp.stack(rows + [jnp.zeros((d,), F32)] * (SUBLANES - len(rows)))


def kernel(x, c, ada_w, ada_b, norm_pre_g, norm_post_g, a_w_in, a_ln_g, a_ln_b, a_ws, a_bs, a_w_out, kv_norm_g, kv_ada_w, kv_ada_b, w_kv, cmp_pos, cmp_w1, cmp_b1, cmp_w2, cmp_b2, b_w_in, b_w_out, ffn_w_gate, ffn_w_up, ffn_w_down, moe_router, moe_router_b, moe_w_gate, moe_w_up, moe_w_down):
    batch, s, d = x.shape
    assert batch == 1 and s % SEL_CHUNK == 0 and s >= WINDOW + Q_BLK
    depth = ada_w.shape[0]
    n_a = depth // 2
    xs = x.reshape(s, d)

    mods = _ada(c, ada_w.reshape(depth * 2, d, 3 * d), ada_b.reshape(depth * 2, 3 * d)).reshape(depth, 2, 3 * d)
    kv_mod = _ada(c, kv_ada_w.reshape(1, d, 2 * d), kv_ada_b.reshape(1, 2 * d))[0]
    shared = None
    n_moe, _, _, f_moe = moe_w_gate.shape
    assert n_moe <= n_a
    moe_gate_up = [None] * n_moe
    moe_down = [None] * n_moe
    gate_2d = moe_w_gate.reshape(n_moe * N_EXPERTS * d, f_moe)
    up_2d = moe_w_up.reshape(n_moe * N_EXPERTS * d, f_moe)
    down_2d = moe_w_down.reshape(n_moe * N_EXPERTS * f_moe, d)

    for layer in range(depth):
        vec = _vec(norm_pre_g[layer, 0], mods[layer, 0], norm_post_g[layer, 0])
        attn = None
        if layer < n_a:
            cast = ((gate_2d, layer, n_moe), (up_2d, layer, n_moe)) if layer < n_moe else ()
            xs, cast = _gmlp_layer(xs, vec, a_w_in[layer].astype(BF16),
                                   jnp.stack([a_ln_g[layer], a_ln_b[layer]]),
                                   a_ws[layer], a_bs[layer].T, a_w_out[layer].astype(BF16), cast)
            if layer < n_moe:
                moe_gate_up[layer] = [w.reshape(N_EXPERTS, d, f_moe) for w in cast]
        else:
            if shared is None:
                kv_vec = _vec(kv_norm_g, jnp.concatenate([kv_mod, jnp.zeros((d,), F32)]), jnp.zeros((d,), F32))
                gd = N_KV * HEAD_DIM
                w_vt = jnp.concatenate([w_kv[:, 3 * gd:4 * gd], w_kv[:, 5 * gd:6 * gd]], axis=1).T
                raw, ks, kw, vst, vwt = _kvproj(xs, kv_vec, w_kv.astype(BF16), w_vt.astype(BF16))
                nc = s // D_CMP
                width = D_CMP * HEAD_DIM
                hid = cmp_w1.shape[-1]
                w2p = jnp.pad(cmp_w2, ((0, 0), (0, 0), (0, LANES - HEAD_DIM))).astype(BF16)
                b2p = jnp.pad(cmp_b2, ((0, 0), (0, LANES - HEAD_DIM))).reshape(2, 1, LANES)
                cmp_n, cmp_t = _compress(raw.reshape(2, N_KV, nc, width),
                                         cmp_pos.reshape(2, 2, 1, width),
                                         cmp_w1.reshape(2, 2, width, hid).astype(BF16),
                                         cmp_b1.reshape(2, 1, hid), w2p, b2p)
                shared = (cmp_n[0], cmp_t[1], ks, vst, kw, vwt)
            i = layer - n_a
            w_t = b_w_in[i].T.astype(BF16)
            q_t, gate_t = _qproj(xs, vec, w_t)
            kc, vct, ks, vst, kw, vwt = shared
            o_t = _attention(q_t, gate_t, kc, vct, ks, vst, kw, vwt)
            attn = (o_t, vec, b_w_out[i].astype(BF16))

        vec = _vec(norm_pre_g[layer, 1], mods[layer, 1], norm_post_g[layer, 1])
        j = layer // 2
        if layer % 2 == 0:
            cast = ((down_2d, j, n_moe),) if j < n_moe else ()
            xs, cast = _swiglu_layer(xs, vec, ffn_w_gate[j].astype(BF16), ffn_w_up[j].astype(BF16),
                                     ffn_w_down[j].astype(BF16), attn, cast)
            if j < n_moe:
                moe_down[j] = cast[0].reshape(N_EXPERTS, f_moe, d)
        else:
            xs = _moe_layer(xs, vec, moe_router[j].T, moe_router_b[j].reshape(N_EXPERTS, 1),
                            *moe_gate_up[j], moe_down[j], attn)
    return xs.reshape(batch, s, d)
```

```python
import functools

import jax
import jax.numpy as jnp
from jax import lax
from jax.experimental import pallas as pl
from jax.experimental.pallas import tpu as pltpu

F32 = jnp.float32
BF16 = jnp.bfloat16
I32 = jnp.int32

EPS = 1e-6
NEG = -1e30

LANES = 128
SUBLANES = 8
VMEM_LIMIT_BYTES = 56 * 1024 * 1024

CHUNK = 128
A_GROUPS = 8
N_HEADS = 16
N_KV = 4
HPG = N_HEADS // N_KV
HEAD_DIM = 64
L_CMP = 32
D_CMP = 16
L_SEL = 64
SEL_SHIFT = 6
N_SELECT = 16
N_FORCED = 3
WINDOW = 512
Q_BLK = 128
N_EXPERTS = 8
MOE_BLOCK = 256

ROW_TILE = 512
TOK_TILE = 256
DISP_TILE = 512
SEL_CHUNK = 512
SEL_BLKS = SEL_CHUNK // L_SEL
CMP_CHUNK = 128
ATT_GROUPS = 2
GATE_ROWS = 16
PS_PAD = 8
V_ROWS = 72
F_CHUNK = 512
WIN_PART_ROWS = 128
WIN_PARTS = TOK_TILE // WIN_PART_ROWS + 1
WIN_ROWS = WIN_PARTS * WIN_PART_ROWS
LOG2E = 1.4426950408889634


def _params(*sem):
    return pltpu.CompilerParams(dimension_semantics=sem, vmem_limit_bytes=VMEM_LIMIT_BYTES)


def _resident(shape, index_map):
    return pl.BlockSpec(shape, index_map, pipeline_mode=pl.Buffered(1))


def _split_bf16(a):
    hi = a.astype(BF16)
    lo = (a - hi.astype(F32)).astype(BF16)
    return hi, lo


def _dot(a, b):
    return jnp.dot(a, b, preferred_element_type=F32)


def _dot_nt(a, b):
    return lax.dot_general(a, b, (((1,), (1,)), ((), ())), preferred_element_type=F32)


def _dot_tn(a, b):
    return lax.dot_general(a, b, (((0,), (0,)), ((), ())), preferred_element_type=F32)


def _prenorm(x, g, scale, shift):
    ms = jnp.mean(x * x, axis=-1, keepdims=True)
    return (x * lax.rsqrt(ms + EPS) * g) * (1.0 + scale) + shift


def _post_residual(x, y, g, gate):
    ms = jnp.mean(y * y, axis=-1, keepdims=True)
    return x + gate * (y * lax.rsqrt(ms + EPS) * g)


def _ada_kernel(c_ref, w_ref, b_ref, o_ref):
    c = c_ref[...]
    c_act = jnp.broadcast_to(c * jax.nn.sigmoid(c), (SUBLANES, c.shape[1]))
    c_hi, c_lo = _split_bf16(c_act)
    w_hi, w_lo = _split_bf16(w_ref[...])
    m = _dot(c_hi, w_hi) + (_dot(c_hi, w_lo) + _dot(c_lo, w_hi))
    o_ref[...] = m[0:1] + b_ref[...]


def _ada(c, w, b):
    n, d, nn = w.shape
    out = pl.pallas_call(
        _ada_kernel,
        grid=(n, nn // d),
        in_specs=[pl.BlockSpec((1, d), lambda i, j: (0, 0)),
                  pl.BlockSpec((None, d, d), lambda i, j: (i, 0, j)),
                  pl.BlockSpec((None, 1, d), lambda i, j: (i, 0, j))],
        out_specs=pl.BlockSpec((None, 1, d), lambda i, j: (i, 0, j)),
        out_shape=jax.ShapeDtypeStruct((n, 1, nn), F32),
        compiler_params=_params("arbitrary", "arbitrary"),
        name="ada",
    )(c, w, b.reshape(n, 1, nn))
    return out.reshape(n, nn)


def _cast_specs(cast, steps):
    in_specs, out_specs, out_shapes = [], [], []
    for w, part, n_parts in cast:
        rows, cols = w.shape[0] // n_parts, w.shape[1]
        assert w.shape[0] % n_parts == 0 and rows % (steps * 2 * SUBLANES) == 0 and cols % LANES == 0
        in_specs.append(pl.BlockSpec((rows // steps, cols), lambda i, first=part * steps: (first + i, 0)))
        out_specs.append(pl.BlockSpec((rows // steps, cols), lambda i: (i, 0)))
        out_shapes.append(jax.ShapeDtypeStruct((rows, cols), BF16))
    return in_specs, out_specs, out_shapes


def _cast_slabs(src_refs, dst_refs):
    for src, dst in zip(src_refs, dst_refs):
        dst[...] = src[...].astype(BF16)


def _gmlp_kernel(n_cast, x_ref, vec_ref, win_ref, ln_ref, ws_ref, bst_ref, wout_ref, *refs):
    cast_refs, o_ref, cast_out_refs, gated_ref = refs[:n_cast], refs[n_cast], refs[n_cast + 1:-1], refs[-1]
    _cast_slabs(cast_refs, cast_out_refs)
    x = x_ref[...]
    vec = vec_ref[...]
    h = _prenorm(x, vec[0:1], vec[1:2], vec[2:3]).astype(BF16)
    z = jax.nn.gelu(_dot(h, win_ref[...]))
    width = z.shape[1] // 2
    u = z[:, :width]
    v = z[:, width:]
    mu = jnp.mean(v, axis=-1, keepdims=True)
    vc = v - mu
    var = jnp.mean(vc * vc, axis=-1, keepdims=True)
    ln = ln_ref[...]
    vn = (vc * lax.rsqrt(var + EPS) * ln[0:1] + ln[1:2]).astype(BF16)
    causal = (lax.broadcasted_iota(I32, (CHUNK, CHUNK), 0)
              >= lax.broadcasted_iota(I32, (CHUNK, CHUNK), 1))
    gw = width // A_GROUPS
    bst = bst_ref[...]
    for g in range(A_GROUPS):
        wg = jnp.where(causal, ws_ref[g], 0.0).astype(BF16)
        for ck in range(x.shape[0] // CHUNK):
            rows = slice(ck * CHUNK, (ck + 1) * CHUNK)
            cols = slice(g * gw, (g + 1) * gw)
            mixed = _dot(wg, vn[rows, cols]) + bst[:, g:g + 1]
            gated_ref[rows, cols] = (u[rows, cols] * mixed).astype(BF16)
    y = _dot(gated_ref[...], wout_ref[...])
    o_ref[...] = _post_residual(x, y, vec[3:4], vec[4:5])


def _gmlp_layer(x, vec, w_in, ln, ws, bst, w_out, cast=()):
    s, d = x.shape
    e2 = w_in.shape[1]
    steps = s // ROW_TILE
    cast_in, cast_out_specs, cast_shapes = _cast_specs(cast, steps)
    out, *cast_out = pl.pallas_call(
        functools.partial(_gmlp_kernel, len(cast)),
        grid=(steps,),
        in_specs=[pl.BlockSpec((ROW_TILE, d), lambda i: (i, 0)),
                  _resident((SUBLANES, d), lambda i: (0, 0)),
                  _resident((d, e2), lambda i: (0, 0)),
                  _resident((2, e2 // 2), lambda i: (0, 0)),
                  _resident(ws.shape, lambda i: (0, 0, 0)),
                  _resident(bst.shape, lambda i: (0, 0)),
                  _resident((e2 // 2, d), lambda i: (0, 0))] + cast_in,
        out_specs=[pl.BlockSpec((ROW_TILE, d), lambda i: (i, 0))] + cast_out_specs,
        out_shape=[jax.ShapeDtypeStruct((s, d), F32)] + cast_shapes,
        scratch_shapes=[pltpu.VMEM((ROW_TILE, e2 // 2), BF16)],
        compiler_params=_params("arbitrary"),
        name="gmlp",
    )(x, vec, w_in, ln, ws, bst, w_out, *[w for w, _, _ in cast])
    return out, cast_out


def _attn_residual(ot_ref, x_ref, avec_ref, wo_ref):
    avec = avec_ref[...]
    return _post_residual(x_ref[...], _dot_tn(ot_ref[...], wo_ref[...]), avec[3:4], avec[4:5])


def _attn_specs(o_t, d):
    nqd = o_t.shape[0]
    return [pl.BlockSpec((nqd, ROW_TILE), lambda i: (0, i)),
            pl.BlockSpec((ROW_TILE, d), lambda i: (i, 0)),
            _resident((SUBLANES, d), lambda i: (0, 0)),
            _resident((nqd, d), lambda i: (0, 0))]


def _swiglu_rows(x, vec, wg_ref, wu_ref, wd_ref):
    h = _prenorm(x, vec[0:1], vec[1:2], vec[2:3]).astype(BF16)
    g = _dot(h, wg_ref[...])
    a = (g * jax.nn.sigmoid(g) * _dot(h, wu_ref[...])).astype(BF16)
    y = _dot(a, wd_ref[...])
    return _post_residual(x, y, vec[3:4], vec[4:5])


def _swiglu_kernel(n_cast, x_ref, vec_ref, wg_ref, wu_ref, wd_ref, *refs):
    _cast_slabs(refs[:n_cast], refs[n_cast + 1:])
    refs[n_cast][...] = _swiglu_rows(x_ref[...], vec_ref[...], wg_ref, wu_ref, wd_ref)


def _attn_swiglu_kernel(n_cast, ot_ref, x_ref, avec_ref, wo_ref, vec_ref, wg_ref, wu_ref, wd_ref, *refs):
    _cast_slabs(refs[:n_cast], refs[n_cast + 1:])
    x = _attn_residual(ot_ref, x_ref, avec_ref, wo_ref)
    refs[n_cast][...] = _swiglu_rows(x, vec_ref[...], wg_ref, wu_ref, wd_ref)


def _swiglu_layer(x, vec, w_gate, w_up, w_down, attn=None, cast=()):
    s, d = x.shape
    f = w_gate.shape[1]
    steps = s // ROW_TILE
    cast_in, cast_out_specs, cast_shapes = _cast_specs(cast, steps)
    own_specs = [_resident((SUBLANES, d), lambda i: (0, 0)),
                 _resident((d, f), lambda i: (0, 0)),
                 _resident((d, f), lambda i: (0, 0)),
                 _resident((f, d), lambda i: (0, 0))]
    if attn is None:
        body, in_specs, args = _swiglu_kernel, [pl.BlockSpec((ROW_TILE, d), lambda i: (i, 0))], (x,)
    else:
        o_t, avec, w_out = attn
        body, in_specs, args = _attn_swiglu_kernel, _attn_specs(o_t, d), (o_t, x, avec, w_out)
    out, *cast_out = pl.pallas_call(
        functools.partial(body, len(cast)),
        grid=(steps,),
        in_specs=in_specs + own_specs + cast_in,
        out_specs=[pl.BlockSpec((ROW_TILE, d), lambda i: (i, 0))] + cast_out_specs,
        out_shape=[jax.ShapeDtypeStruct((s, d), F32)] + cast_shapes,
        compiler_params=_params("arbitrary"),
        name="swiglu",
    )(*args, vec, w_gate, w_up, w_down, *[w for w, _, _ in cast])
    return out, cast_out


def _router_kernel(x_ref, vec_ref, wrt_ref, br_ref, h_ref, idx_ref, gate_ref):
    _route_rows(x_ref[...], vec_ref[...], wrt_ref, br_ref, h_ref, idx_ref, gate_ref)


def _attn_router_kernel(ot_ref, x_ref, avec_ref, wo_ref, vec_ref, wrt_ref, br_ref, xo_ref, h_ref, idx_ref, gate_ref):
    x = _attn_residual(ot_ref, x_ref, avec_ref, wo_ref)
    xo_ref[...] = x
    _route_rows(x, vec_ref[...], wrt_ref, br_ref, h_ref, idx_ref, gate_ref)


def _route_rows(x, vec, wrt_ref, br_ref, h_ref, idx_ref, gate_ref):
    h = _prenorm(x, vec[0:1], vec[1:2], vec[2:3])
    h_ref[...] = h.astype(BF16)
    h_hi, h_lo = _split_bf16(h)
    w_hi, w_lo = _split_bf16(wrt_ref[...])
    logit = _dot_nt(w_hi, h_hi) + (_dot_nt(w_hi, h_lo) + _dot_nt(w_lo, h_hi)) + br_ref[...]
    ne = logit.shape[0]
    eidx = lax.broadcasted_iota(I32, logit.shape, 0)
    m1 = jnp.max(logit, axis=0, keepdims=True)
    i1 = jnp.min(jnp.where(logit == m1, eidx, ne), axis=0, keepdims=True)
    rest = jnp.where(eidx == i1, -jnp.inf, logit)
    m2 = jnp.max(rest, axis=0, keepdims=True)
    i2 = jnp.min(jnp.where(rest == m2, eidx, ne), axis=0, keepdims=True)
    e21 = jnp.exp(m2 - m1)
    g1 = 1.0 / (1.0 + e21)
    g2 = e21 * g1
    pad_i = jnp.zeros((SUBLANES - 2, i1.shape[1]), I32)
    pad_f = jnp.zeros((SUBLANES - 2, i1.shape[1]), F32)
    idx_ref[...] = jnp.concatenate([i1, i2, pad_i], axis=0)
    gate_ref[...] = jnp.concatenate([g1, g2, pad_f], axis=0)


def _router(x, vec, w_rt, b_r, attn=None):
    s, d = x.shape
    ne = w_rt.shape[0]
    own_specs = [_resident((SUBLANES, d), lambda i: (0, 0)),
                 _resident((ne, d), lambda i: (0, 0)),
                 _resident((ne, 1), lambda i: (0, 0))]
    out_specs = [pl.BlockSpec((ROW_TILE, d), lambda i: (i, 0)),
                 pl.BlockSpec((SUBLANES, ROW_TILE), lambda i: (0, i)),
                 pl.BlockSpec((SUBLANES, ROW_TILE), lambda i: (0, i))]
    out_shape = [jax.ShapeDtypeStruct((s, d), BF16),
                 jax.ShapeDtypeStruct((SUBLANES, s), I32),
                 jax.ShapeDtypeStruct((SUBLANES, s), F32)]
    if attn is None:
        h, idx, gates = pl.pallas_call(
            _router_kernel,
            grid=(s // ROW_TILE,),
            in_specs=[pl.BlockSpec((ROW_TILE, d), lambda i: (i, 0))] + own_specs,
            out_specs=out_specs,
            out_shape=out_shape,
            compiler_params=_params("arbitrary"),
            name="moe_router",
        )(x, vec, w_rt, b_r)
        return x, h, idx, gates
    o_t, avec, w_out = attn
    return pl.pallas_call(
        _attn_router_kernel,
        grid=(s // ROW_TILE,),
        in_specs=_attn_specs(o_t, d) + own_specs,
        out_specs=[pl.BlockSpec((ROW_TILE, d), lambda i: (i, 0))] + out_specs,
        out_shape=[jax.ShapeDtypeStruct((s, d), F32)] + out_shape,
        compiler_params=_params("arbitrary"),
        name="moe_router",
    )(o_t, x, avec, w_out, vec, w_rt, b_r)


def _rank_kernel(idx_ref, rank_ref, start_ref, count_ref, carry_ref):
    @pl.when(pl.program_id(0) == 0)
    def _():
        carry_ref[...] = jnp.zeros_like(carry_ref)

    idx = idx_ref[...]
    tt = idx.shape[1]
    eidx = lax.broadcasted_iota(I32, (N_EXPERTS, tt), 0)
    hit1 = eidx == idx[0:1]
    hit2 = eidx == idx[1:2]
    member = jnp.where(hit1 | hit2, 1.0, 0.0)
    before = (lax.broadcasted_iota(I32, (tt, tt), 0) < lax.broadcasted_iota(I32, (tt, tt), 1))
    carry = carry_ref[...]
    cum = _dot(member.astype(BF16), jnp.where(before, 1.0, 0.0).astype(BF16)) + carry[:, 0:1]
    r1 = jnp.sum(jnp.where(hit1, cum, 0.0), axis=0, keepdims=True)
    r2 = jnp.sum(jnp.where(hit2, cum, 0.0), axis=0, keepdims=True)
    rank_ref[...] = jnp.concatenate([r1, r2, jnp.zeros((SUBLANES - 2, tt), F32)], axis=0).astype(I32)
    tile_count = jnp.broadcast_to(jnp.sum(member, axis=1, keepdims=True), carry.shape)
    start_ref[...] = carry
    count_ref[...] = tile_count
    carry_ref[...] = carry + tile_count


def _ranks(idx):
    s = idx.shape[1]
    nt = s // TOK_TILE
    return pl.pallas_call(
        _rank_kernel,
        grid=(nt,),
        in_specs=[pl.BlockSpec((SUBLANES, TOK_TILE), lambda i: (0, i))],
        out_specs=[pl.BlockSpec((SUBLANES, TOK_TILE), lambda i: (0, i)),
                   pl.BlockSpec((None, N_EXPERTS, LANES), lambda i: (i, 0, 0)),
                   pl.BlockSpec((None, N_EXPERTS, LANES), lambda i: (i, 0, 0))],
        out_shape=[jax.ShapeDtypeStruct((SUBLANES, s), I32),
                   jax.ShapeDtypeStruct((nt, N_EXPERTS, LANES), F32),
                   jax.ShapeDtypeStruct((nt, N_EXPERTS, LANES), F32)],
        scratch_shapes=[pltpu.VMEM((N_EXPERTS, LANES), F32)],
        compiler_params=_params("arbitrary"),
        name="moe_ranks",
    )(idx)


def _match(idx, rank, expert, rows):
    r1 = jnp.where(idx[0:1] == expert, rank[0:1], -1)
    r2 = jnp.where(idx[1:2] == expert, rank[1:2], -1)
    return rows == r1, rows == r2


def _dispatch_kernel(be_ref, lb_ref, tlo_ref, thi_ref, idx_ref, rank_ref, gate_ref, h_ref, o_ref, rg_ref,
                     acc_ref, gacc_ref):
    b = pl.program_id(0)
    expert = be_ref[b]
    rows = lax.broadcasted_iota(I32, (MOE_BLOCK, DISP_TILE), 0) + lb_ref[b]
    acc_ref[...] = jnp.zeros_like(acc_ref)
    gacc_ref[...] = jnp.zeros_like(gacc_ref)

    def body(t, carry):
        off = pl.multiple_of(t * DISP_TILE, DISP_TILE)
        m1, m2 = _match(idx_ref[:, pl.ds(off, DISP_TILE)], rank_ref[:, pl.ds(off, DISP_TILE)], expert, rows)
        gates = gate_ref[:, pl.ds(off, DISP_TILE)]
        gacc_ref[...] += jnp.sum(jnp.where(m1, gates[0:1], 0.0) + jnp.where(m2, gates[1:2], 0.0),
                                 axis=1, keepdims=True)
        onehot = jnp.where(m1 | m2, 1.0, 0.0).astype(BF16)
        acc_ref[...] += _dot(onehot, h_ref[pl.ds(off, DISP_TILE), :])
        return carry

    lax.fori_loop(tlo_ref[b], thi_ref[b], body, 0)
    o_ref[...] = acc_ref[...].astype(BF16)
    rg_ref[...] = gacc_ref[...]


def _dispatch(blk_e, blk_lb, blk_tlo, blk_thi, idx, rank, gates, h):
    s, d = h.shape
    n_blk = blk_e.shape[0]
    grid_spec = pltpu.PrefetchScalarGridSpec(
        num_scalar_prefetch=4,
        grid=(n_blk,),
        in_specs=[_resident((SUBLANES, s), lambda b, *_: (0, 0)),
                  _resident((SUBLANES, s), lambda b, *_: (0, 0)),
                  _resident((SUBLANES, s), lambda b, *_: (0, 0)),
                  _resident((s, d), lambda b, *_: (0, 0))],
        out_specs=[pl.BlockSpec((MOE_BLOCK, d), lambda b, *_: (b, 0)),
                   pl.BlockSpec((MOE_BLOCK, 1), lambda b, *_: (b, 0))],
        scratch_shapes=[pltpu.VMEM((MOE_BLOCK, d), F32), pltpu.VMEM((MOE_BLOCK, 1), F32)],
    )
    return pl.pallas_call(
        _dispatch_kernel,
        grid_spec=grid_spec,
        out_shape=[jax.ShapeDtypeStruct((n_blk * MOE_BLOCK, d), BF16),
                   jax.ShapeDtypeStruct((n_blk * MOE_BLOCK, 1), F32)],
        compiler_params=_params("arbitrary"),
        name="moe_dispatch",
    )(blk_e, blk_lb, blk_tlo, blk_thi, idx, rank, gates, h)


def _expert_kernel(be_ref, used_ref, x_ref, rg_ref, wg_ref, wu_ref, wd_ref, o_ref):
    b = pl.program_id(0)

    @pl.when(b < used_ref[0])
    def _():
        x = x_ref[...]
        f = wg_ref.shape[1]
        y = jnp.zeros(o_ref.shape, F32)
        for c in range(f // F_CHUNK):
            cols = slice(c * F_CHUNK, (c + 1) * F_CHUNK)
            g = _dot(x, wg_ref[:, cols])
            a = (g * jax.nn.sigmoid(g) * _dot(x, wu_ref[:, cols])).astype(BF16)
            y = y + _dot(a, wd_ref[cols, :])
        o_ref[...] = (y * rg_ref[...]).astype(BF16)

    @pl.when(b >= used_ref[0])
    def _():
        o_ref[...] = jnp.zeros_like(o_ref)


def _experts(blk_e, n_used, x_buf, row_gate, w_gate, w_up, w_down):
    n_rows, d = x_buf.shape
    f = w_gate.shape[2]
    assert f % F_CHUNK == 0
    grid_spec = pltpu.PrefetchScalarGridSpec(
        num_scalar_prefetch=2,
        grid=(n_rows // MOE_BLOCK,),
        in_specs=[pl.BlockSpec((MOE_BLOCK, d), lambda b, be, nu: (b, 0)),
                  pl.BlockSpec((MOE_BLOCK, 1), lambda b, be, nu: (b, 0)),
                  pl.BlockSpec((None, d, f), lambda b, be, nu: (be[b], 0, 0)),
                  pl.BlockSpec((None, d, f), lambda b, be, nu: (be[b], 0, 0)),
                  pl.BlockSpec((None, f, d), lambda b, be, nu: (be[b], 0, 0))],
        out_specs=pl.BlockSpec((MOE_BLOCK, d), lambda b, be, nu: (b, 0)),
    )
    return pl.pallas_call(
        _expert_kernel,
        grid_spec=grid_spec,
        out_shape=jax.ShapeDtypeStruct((n_rows, d), BF16),
        compiler_params=_params("arbitrary"),
        name="moe_experts",
    )(blk_e, n_used, x_buf, row_gate, w_gate, w_up, w_down)


def _combine_kernel(win_ref, lb_ref, tail_ref, idx_ref, rank_ref, *refs):
    y_refs = refs[:N_EXPERTS * WIN_PARTS]
    x_ref, vec_ref, o_ref, acc_ref = refs[N_EXPERTS * WIN_PARTS:]
    t = pl.program_id(0)
    idx = idx_ref[...]
    rank = rank_ref[...]
    main_rows = (WIN_PARTS - 1) * WIN_PART_ROWS

    def gathered(e, first_part, n_parts):
        n_rows = n_parts * WIN_PART_ROWS
        rows = (lax.broadcasted_iota(I32, (n_rows, TOK_TILE), 0)
                + (lb_ref[t * N_EXPERTS + e] + first_part * WIN_PART_ROWS))
        m1, m2 = _match(idx, rank, e, rows)
        onehot = jnp.where(m1 | m2, 1.0, 0.0).astype(BF16)
        y = jnp.concatenate([y_refs[e * WIN_PARTS + first_part + k][...] for k in range(n_parts)], axis=0)
        return _dot_tn(onehot, y)

    acc = jnp.zeros(o_ref.shape, F32)
    for e in range(N_EXPERTS):
        acc = acc + gathered(e, 0, WIN_PARTS - 1)
    acc_ref[...] = acc
    for e in range(N_EXPERTS):
        @pl.when(tail_ref[t * N_EXPERTS + e] > 0)
        def _():
            acc_ref[...] += gathered(e, WIN_PARTS - 1, 1)
    vec = vec_ref[...]
    o_ref[...] = _post_residual(x_ref[...], acc_ref[...], vec[3:4], vec[4:5])


def _combine(win_start, win_lb, win_tail, idx, rank, y_buf, x, vec):
    s, d = x.shape

    def y_spec(e, k):
        return pl.BlockSpec((WIN_PART_ROWS, d), lambda t, ws, lb, tl: (ws[t * N_EXPERTS + e] + k, 0))

    grid_spec = pltpu.PrefetchScalarGridSpec(
        num_scalar_prefetch=3,
        grid=(s // TOK_TILE,),
        in_specs=[pl.BlockSpec((SUBLANES, TOK_TILE), lambda t, *_: (0, t)),
                  pl.BlockSpec((SUBLANES, TOK_TILE), lambda t, *_: (0, t))]
                 + [y_spec(e, k) for e in range(N_EXPERTS) for k in range(WIN_PARTS)]
                 + [pl.BlockSpec((TOK_TILE, d), lambda t, *_: (t, 0)),
                    _resident((SUBLANES, d), lambda t, *_: (0, 0))],
        out_specs=pl.BlockSpec((TOK_TILE, d), lambda t, *_: (t, 0)),
        scratch_shapes=[pltpu.VMEM((TOK_TILE, d), F32)],
    )
    return pl.pallas_call(
        _combine_kernel,
        grid_spec=grid_spec,
        out_shape=jax.ShapeDtypeStruct((s, d), F32),
        compiler_params=_params("arbitrary"),
        name="moe_combine",
    )(win_start, win_lb, win_tail, idx, rank, *([y_buf] * (N_EXPERTS * WIN_PARTS)), x, vec)


def _moe_layer(x, vec, w_rt, b_r, w_gate, w_up, w_down, attn=None):
    s, d = x.shape
    nt = s // TOK_TILE
    x, h, idx, gates = _router(x, vec, w_rt, b_r, attn)
    rank, tile_start, tile_count = _ranks(idx)

    tile_start = tile_start[:, :, 0].astype(I32)
    tile_count = tile_count[:, :, 0].astype(I32)
    tile_end = tile_start + tile_count
    counts = tile_end[-1]
    padded = (counts + MOE_BLOCK - 1) // MOE_BLOCK * MOE_BLOCK
    pend = jnp.cumsum(padded)
    pstart = pend - padded
    n_rows = -(-(2 * s) // MOE_BLOCK) * MOE_BLOCK + N_EXPERTS * MOE_BLOCK
    n_blk = n_rows // MOE_BLOCK
    blk_row = jnp.arange(n_blk, dtype=I32) * MOE_BLOCK
    blk_e = jnp.minimum(jnp.sum(blk_row[:, None] >= pend[None, :], axis=1), N_EXPERTS - 1).astype(I32)
    blk_lb = blk_row - pstart[blk_e]
    per = DISP_TILE // TOK_TILE
    te = tile_end[per - 1::per][:, blk_e]
    ts = tile_start[::per][:, blk_e]
    blk_tlo = jnp.sum(te <= blk_lb[None, :], axis=0).astype(I32)
    blk_thi = jnp.sum(ts < (blk_lb + MOE_BLOCK)[None, :], axis=0).astype(I32)
    n_used = (pend[-1] // MOE_BLOCK).astype(I32).reshape(1)

    x_buf, row_gate = _dispatch(blk_e, blk_lb.astype(I32), blk_tlo, blk_thi, idx, rank, gates, h)
    y_buf = _experts(blk_e, n_used, x_buf, row_gate, w_gate, w_up, w_down)

    win_start = jnp.minimum((pstart[None, :] + tile_start) // WIN_PART_ROWS, n_rows // WIN_PART_ROWS - WIN_PARTS)
    win_lb = win_start * WIN_PART_ROWS - pstart[None, :]
    win_tail = tile_end - win_lb > (WIN_PARTS - 1) * WIN_PART_ROWS
    return _combine(win_start.reshape(-1).astype(I32), win_lb.reshape(-1).astype(I32),
                    win_tail.reshape(-1).astype(I32), idx, rank, y_buf, x, vec)


def _kvproj_kernel(x_ref, vec_ref, wkv_ref, wvt_ref, raw_ref, ks_ref, kw_ref, vst_ref, vwt_ref):
    x = x_ref[...]
    vec = vec_ref[...]
    h = _prenorm(x, vec[0:1], vec[1:2], vec[2:3]).astype(BF16)
    kv = _dot(h, wkv_ref[...])
    vt = _dot_nt(wvt_ref[...], h)
    tm = x.shape[0]
    gd = N_KV * HEAD_DIM
    key_blk = jnp.right_shift(pl.program_id(0) * tm + lax.broadcasted_iota(I32, (tm, SEL_BLKS), 0), SEL_SHIFT)
    ind = jnp.where((key_blk & (SEL_BLKS - 1)) == lax.broadcasted_iota(I32, (tm, SEL_BLKS), 1), 1.0, 0.0)
    pad_s = jnp.zeros((tm, LANES - HEAD_DIM - SEL_BLKS), F32)
    pad_w = jnp.zeros((tm, LANES - HEAD_DIM), F32)
    ones_row = jnp.concatenate([jnp.ones((1, tm), F32), jnp.zeros((V_ROWS - HEAD_DIM - 1, tm), F32)], axis=0)
    for g in range(N_KV):
        c = g * HEAD_DIM
        raw_ref[0, g] = kv[:, c:c + HEAD_DIM]
        raw_ref[1, g] = kv[:, gd + c:gd + c + HEAD_DIM]
        ks = kv[:, 2 * gd + c:2 * gd + c + HEAD_DIM]
        kw = kv[:, 4 * gd + c:4 * gd + c + HEAD_DIM]
        ks_ref[g] = jnp.concatenate([ks, ind, pad_s], axis=1).astype(BF16)
        kw_ref[g] = jnp.concatenate([kw, pad_w], axis=1).astype(BF16)
        vst_ref[g] = jnp.concatenate([vt[c:c + HEAD_DIM], ones_row], axis=0).astype(BF16)
        vwt_ref[g] = jnp.concatenate([vt[gd + c:gd + c + HEAD_DIM], ones_row], axis=0).astype(BF16)


def _kvproj(x, vec, w_kv, w_vt):
    s, d = x.shape
    nkv = w_kv.shape[1]
    return pl.pallas_call(
        _kvproj_kernel,
        grid=(s // ROW_TILE,),
        in_specs=[pl.BlockSpec((ROW_TILE, d), lambda i: (i, 0)),
                  _resident((SUBLANES, d), lambda i: (0, 0)),
                  _resident((d, nkv), lambda i: (0, 0)),
                  _resident(w_vt.shape, lambda i: (0, 0))],
        out_specs=[pl.BlockSpec((2, N_KV, ROW_TILE, HEAD_DIM), lambda i: (0, 0, i, 0)),
                   pl.BlockSpec((N_KV, ROW_TILE, LANES), lambda i: (0, i, 0)),
                   pl.BlockSpec((N_KV, ROW_TILE, LANES), lambda i: (0, i, 0)),
                   pl.BlockSpec((N_KV, V_ROWS, ROW_TILE), lambda i: (0, 0, i)),
                   pl.BlockSpec((N_KV, V_ROWS, ROW_TILE), lambda i: (0, 0, i))],
        out_shape=[jax.ShapeDtypeStruct((2, N_KV, s, HEAD_DIM), F32),
                   jax.ShapeDtypeStruct((N_KV, s, LANES), BF16),
                   jax.ShapeDtypeStruct((N_KV, s, LANES), BF16),
                   jax.ShapeDtypeStruct((N_KV, V_ROWS, s), BF16),
                   jax.ShapeDtypeStruct((N_KV, V_ROWS, s), BF16)],
        compiler_params=_params("arbitrary"),
        name="nsa_kvproj",
    )(x, vec, w_kv, w_vt)


def _compress_kernel(raw_ref, pos_ref, w1_ref, b1_ref, w2_ref, b2_ref, n_ref, t_ref):
    raw = raw_ref[...]
    nc = raw.shape[0]
    first = _dot((raw + pos_ref[0]).astype(BF16), w1_ref[0])
    second = _dot((raw + pos_ref[1]).astype(BF16), w1_ref[1])
    hid = jax.nn.gelu(first + pltpu.roll(second, nc - 1, 0) + b1_ref[...])
    out = _dot(hid.astype(BF16), w2_ref[...]) + b2_ref[...]
    n_ref[...] = out.astype(BF16)
    row = lax.broadcasted_iota(I32, (LANES, nc), 0)
    t_ref[...] = jnp.where(row == HEAD_DIM, 1.0, out.T).astype(BF16)


def _compress(raw, pos, w1, b1, w2, b2):
    _, g, nc, width = raw.shape
    hid = w1.shape[-1]
    return pl.pallas_call(
        _compress_kernel,
        grid=(2, g),
        in_specs=[pl.BlockSpec((None, None, nc, width), lambda j, k: (j, k, 0, 0)),
                  pl.BlockSpec((None, 2, 1, width), lambda j, k: (j, 0, 0, 0)),
                  pl.BlockSpec((None, 2, width, hid), lambda j, k: (j, 0, 0, 0)),
                  pl.BlockSpec((None, 1, hid), lambda j, k: (j, 0, 0)),
                  pl.BlockSpec((None, hid, LANES), lambda j, k: (j, 0, 0)),
                  pl.BlockSpec((None, 1, LANES), lambda j, k: (j, 0, 0))],
        out_specs=[pl.BlockSpec((None, None, nc, LANES), lambda j, k: (j, k, 0, 0)),
                   pl.BlockSpec((None, None, LANES, nc), lambda j, k: (j, k, 0, 0))],
        out_shape=[jax.ShapeDtypeStruct((2, g, nc, LANES), BF16),
                   jax.ShapeDtypeStruct((2, g, LANES, nc), BF16)],
        compiler_params=_params("arbitrary", "arbitrary"),
        name="nsa_compress",
    )(raw, pos, w1, b1, w2, b2)


def _qproj_kernel(x_ref, vec_ref, wt_ref, q_ref, gate_ref):
    x = x_ref[...]
    vec = vec_ref[...]
    h = _prenorm(x, vec[0:1], vec[1:2], vec[2:3]).astype(BF16)
    pt = _dot_nt(wt_ref[...], h)
    nq = N_HEADS * HEAD_DIM
    q_ref[...] = (pt[:nq] * (HEAD_DIM ** -0.5 * LOG2E)).astype(BF16)
    gates = jax.nn.sigmoid(pt[nq:nq + 3 * N_HEADS])
    per = 3 * HPG
    pad = jnp.zeros((GATE_ROWS - per, x.shape[0]), F32)
    for g in range(N_KV):
        gate_ref[g] = jnp.concatenate([gates[g * per:(g + 1) * per], pad], axis=0)


def _qproj(x, vec, w_t):
    s, d = x.shape
    return pl.pallas_call(
        _qproj_kernel,
        grid=(s // ROW_TILE,),
        in_specs=[pl.BlockSpec((ROW_TILE, d), lambda i: (i, 0)),
                  _resident((SUBLANES, d), lambda i: (0, 0)),
                  _resident(w_t.shape, lambda i: (0, 0))],
        out_specs=[pl.BlockSpec((N_HEADS * HEAD_DIM, ROW_TILE), lambda i: (0, i)),
                   pl.BlockSpec((N_KV, GATE_ROWS, ROW_TILE), lambda i: (0, 0, i))],
        out_shape=[jax.ShapeDtypeStruct((N_HEADS * HEAD_DIM, s), BF16),
                   jax.ShapeDtypeStruct((N_KV, GATE_ROWS, s), F32)],
        compiler_params=_params("arbitrary"),
        name="nsa_qproj",
    )(x, vec, w_t)


def _attn_kernel(q_ref, gate_ref, cmask_ref, wmask_ref, kc_ref, vct_ref, ks_ref, vst_ref, kw_ref, vwt_ref,
                 o_ref, bias_ref, ps_ref, sc_ref, *sp_refs):
    s_refs = [[sp_refs[2 * g + k] for k in range(2)] for g in range(ATT_GROUPS)]
    p_refs = [[sp_refs[2 * ATT_GROUPS + 2 * g + k] for k in range(2)] for g in range(ATT_GROUPS)]
    i = pl.program_id(1)
    nq = HPG * Q_BLK
    nc = kc_ref.shape[1]
    nb = bias_ref.shape[1]
    rows = HPG * HEAD_DIM
    t1 = i * Q_BLK + lax.broadcasted_iota(I32, (1, Q_BLK), 1)
    t4 = jnp.concatenate([t1] * HPG, axis=1)
    j_io = lax.broadcasted_iota(I32, (nb, Q_BLK), 0)
    jt = jnp.right_shift(t1, SEL_SHIFT)
    forced = (j_io == 0) | (j_io == jt) | (j_io == jt - 1)
    cand_off = jnp.where((j_io * L_SEL <= t1) & jnp.logical_not(forced), 0.0, NEG)
    past_off = jnp.where(j_io * L_SEL < i * Q_BLK, 0.0, NEG)
    doff = pl.multiple_of(i * Q_BLK, Q_BLK)
    cmask_off = nc - i * (Q_BLK // D_CMP)
    span = WINDOW + Q_BLK
    wstart = pl.multiple_of(jnp.maximum(i * Q_BLK - WINDOW, 0), Q_BLK)
    win_mask = wmask_ref[pl.ds(pl.multiple_of(WINDOW - jnp.minimum(i * Q_BLK, WINDOW), Q_BLK), span), :]
    win_mask = jnp.concatenate([win_mask] * HPG, axis=1)
    diag_mask = jnp.concatenate([wmask_ref[WINDOW:WINDOW + Q_BLK, :]] * HPG, axis=1)
    vpad = jnp.zeros((LANES - HEAD_DIM - 2 * SEL_BLKS, nq), BF16)
    bpad = jnp.zeros((SEL_BLKS, nq), F32)

    def aligned(x, m):
        return x if isinstance(x, int) else pl.multiple_of(x, m)

    groups = range(ATT_GROUPS)
    q4s, q_plains = [], []
    for g in groups:
        qb = q_ref[g * rows:(g + 1) * rows, :]
        q4s.append(jnp.concatenate([qb[h * HEAD_DIM:(h + 1) * HEAD_DIM] for h in range(HPG)], axis=1))
        q_plains.append(jnp.concatenate([q4s[g], jnp.zeros((LANES - HEAD_DIM, nq), BF16)], axis=0))

    def compressed(n_rows):
        chunks = [slice(r0, r0 + CMP_CHUNK) for r0 in range(0, n_rows, CMP_CHUNK)]
        ms = [jnp.full((1, nq), NEG, F32) for _ in groups]
        for rs in chunks:
            mask = cmask_ref[pl.ds(pl.multiple_of(cmask_off + rs.start, 8), CMP_CHUNK), :]
            mask = jnp.concatenate([mask] * HPG, axis=1)
            for g in groups:
                s = _dot(kc_ref[g, rs, :], q_plains[g]) + mask
                sc_ref[g, rs, :] = s
                ms[g] = jnp.maximum(ms[g], jnp.max(s, axis=0, keepdims=True))
        ls = [jnp.zeros((1, nq), F32) for _ in groups]
        for rs in chunks:
            for g in groups:
                e = jnp.exp2(sc_ref[g, rs, :] - ms[g])
                sc_ref[g, rs, :] = e
                ls[g] = ls[g] + jnp.sum(e, axis=0, keepdims=True)
        rls = [jnp.where(t4 >= L_CMP - 1, 1.0 / ls[g], 0.0) for g in groups]
        os_ = [jnp.zeros((HEAD_DIM, nq), F32) for _ in groups]
        for g in groups:
            ps_ref[g, 0:PS_PAD, :] = jnp.zeros((PS_PAD, Q_BLK), F32)
            if n_rows < nc:
                ps_ref[g, PS_PAD + n_rows:PS_PAD + nc, :] = jnp.zeros((nc - n_rows, Q_BLK), F32)
        for rs in chunks:
            for g in groups:
                pc = sc_ref[g, rs, :] * rls[g]
                os_[g] = os_[g] + _dot(vct_ref[g, 0:HEAD_DIM, rs], pc.astype(BF16))
                psum = pc[:, 0:Q_BLK]
                for h in range(1, HPG):
                    psum = psum + pc[:, h * Q_BLK:(h + 1) * Q_BLK]
                ps_ref[g, PS_PAD + rs.start:PS_PAD + rs.stop, :] = psum
        ratio = L_SEL // D_CMP
        imps = []
        for g in groups:
            imp = jnp.zeros((nb, Q_BLK), F32)
            for k in range(1 - L_CMP // D_CMP, ratio):
                imp = imp + ps_ref[g, pl.ds(PS_PAD + k, nb, stride=ratio), :]
            imps.append(imp)
        return tuple(os_) + tuple(imps)

    if nc % (2 * CMP_CHUNK) == 0:
        last_complete = ((i + 1) * Q_BLK - L_CMP) // D_CMP
        cmp_out = lax.cond(last_complete < nc // 2, lambda: compressed(nc // 2), lambda: compressed(nc))
    else:
        cmp_out = compressed(nc)
    o_cmps, imps = cmp_out[:ATT_GROUPS], cmp_out[ATT_GROUPS:]

    def select(imps):
        w = jnp.concatenate([imp + cand_off for imp in imps], axis=1)
        j_all = lax.broadcasted_iota(I32, w.shape, 0).astype(F32)
        for _ in range(N_SELECT - N_FORCED):
            m = jnp.max(w, axis=0, keepdims=True)
            first = jnp.min(jnp.where(w == m, j_all, float(nb)), axis=0, keepdims=True)
            first = jnp.where(m > 0.5 * NEG, first, -1.0)
            w = jnp.where(j_all == first, NEG, w)
        for g in range(ATT_GROUPS):
            bias1 = jnp.where(w[:, g * Q_BLK:(g + 1) * Q_BLK] < 0.5 * NEG, past_off, NEG)
            bias_ref[g] = jnp.concatenate([bias1] * HPG, axis=1)

    def local():
        sds = [_dot(ks_ref[g, pl.ds(doff, Q_BLK), :], q_plains[g]) + diag_mask for g in groups]
        sws = [_dot(kw_ref[g, pl.ds(wstart, span), :], q_plains[g]) + win_mask for g in groups]
        out = []
        for g in groups:
            m0 = jnp.max(sds[g], axis=0, keepdims=True)
            acc0 = _dot(vst_ref[g, :, pl.ds(doff, Q_BLK)], jnp.exp2(sds[g] - m0).astype(BF16))
            pw = jnp.exp2(sws[g] - jnp.max(sws[g], axis=0, keepdims=True)).astype(BF16)
            acc_w = _dot(vwt_ref[g, :, pl.ds(wstart, span)], pw)
            out.append((acc_w[0:HEAD_DIM] / acc_w[HEAD_DIM:HEAD_DIM + 1], m0, acc0))
        return out

    def query_operand(g, q4, c):
        brow = bias_ref[g, pl.ds(aligned(c * SEL_BLKS, SEL_BLKS), SEL_BLKS), :]
        b16 = jnp.concatenate([brow, bpad], axis=0).astype(BF16)
        return jnp.concatenate([q4, b16, vpad], axis=0)

    def phase(g, q4, c, slot, state, do_scores=True, do_softmax=True, do_values=True):
        m_run, alpha, acc, cmax = state
        other = 1 - slot
        if do_values:
            voff = aligned(c * SEL_CHUNK, SEL_CHUNK)
            acc = alpha * acc + _dot(vst_ref[g, :, pl.ds(voff, SEL_CHUNK)], p_refs[g][slot][...])
        if do_scores:
            koff = aligned((c + 2) * SEL_CHUNK, SEL_CHUNK)
            s = _dot(ks_ref[g, pl.ds(koff, SEL_CHUNK), :], query_operand(g, q4, c + 2))
            s_refs[g][slot][...] = s.astype(BF16)
            new_max = jnp.max(s, axis=0, keepdims=True)
        if do_softmax:
            m_new = jnp.maximum(m_run, cmax[other])
            p_refs[g][other][...] = jnp.exp2(s_refs[g][other][...] - m_new.astype(BF16))
            m_run, alpha = m_new, jnp.exp2(m_run - m_new)
        if do_scores:
            cmax = (new_max, cmax[1]) if slot == 0 else (cmax[0], new_max)
        return m_run, alpha, acc, cmax

    select(imps)
    heads = [(q4s[g], o_cmps[g]) + loc for g, loc in enumerate(local())]

    def pair(k, carries):
        out = []
        for g in range(ATT_GROUPS):
            q4 = heads[g][0]
            state = phase(g, q4, 2 * k, 0, carries[g])
            out.append(phase(g, q4, 2 * k + 1, 1, state))
        return tuple(out)

    n_pairs = jnp.maximum((i * Q_BLK + 2 * SEL_CHUNK - 1) // (2 * SEL_CHUNK), 1)
    init = []
    for g in range(ATT_GROUPS):
        q4, _, _, m0, acc0 = heads[g]
        neg_row = jnp.full((1, nq), NEG, F32)
        state = (m0, jnp.ones((1, nq), F32), acc0, (neg_row, neg_row))
        state = phase(g, q4, -2, 0, state, do_softmax=False, do_values=False)
        init.append(phase(g, q4, -1, 1, state, do_values=False))
    carries = lax.fori_loop(0, n_pairs - 1, pair, tuple(init))
    last = 2 * (n_pairs - 1)
    for g in range(ATT_GROUPS):
        q4, o_cmp, o_win, _, _ = heads[g]
        state = phase(g, q4, last, 0, carries[g], do_scores=False)
        _, _, acc_s, _ = phase(g, q4, last + 1, 1, state, do_scores=False, do_softmax=False)
        o_sel = acc_s[0:HEAD_DIM] / acc_s[HEAD_DIM:HEAD_DIM + 1]
        gates = gate_ref[g]
        outs = []
        for h in range(HPG):
            cols = slice(h * Q_BLK, (h + 1) * Q_BLK)
            outs.append(o_cmp[:, cols] * gates[3 * h:3 * h + 1]
                        + o_sel[:, cols] * gates[3 * h + 1:3 * h + 2]
                        + o_win[:, cols] * gates[3 * h + 2:3 * h + 3])
        o_ref[g * rows:(g + 1) * rows, :] = jnp.concatenate(outs, axis=0).astype(BF16)


def _attention(q_t, gate_t, kc, vct, ks, vst, kw, vwt):
    nqd, s = q_t.shape
    nc = kc.shape[1]
    nb = s // L_SEL
    nq = HPG * Q_BLK
    rows = ATT_GROUPS * HPG * HEAD_DIM
    assert (s // SEL_CHUNK) % 2 == 0 and s >= WINDOW + Q_BLK and N_KV % ATT_GROUPS == 0 and nb >= N_SELECT
    qq = jnp.arange(Q_BLK)[None, :]
    rc = jnp.arange(2 * nc)[:, None]
    cmp_mask = jnp.where(D_CMP * (rc - nc) + L_CMP - 1 <= qq, 0.0, NEG).astype(F32)
    rw = jnp.arange(2 * WINDOW + Q_BLK)[:, None]
    win_mask = jnp.where((qq < rw) & (rw <= qq + WINDOW), 0.0, NEG).astype(F32)
    return pl.pallas_call(
        _attn_kernel,
        grid=(N_KV // ATT_GROUPS, s // Q_BLK),
        in_specs=[pl.BlockSpec((rows, Q_BLK), lambda g, i: (g, i)),
                  pl.BlockSpec((ATT_GROUPS, GATE_ROWS, Q_BLK), lambda g, i: (g, 0, i)),
                  _resident(cmp_mask.shape, lambda g, i: (0, 0)),
                  _resident(win_mask.shape, lambda g, i: (0, 0)),
                  _resident((ATT_GROUPS, nc, LANES), lambda g, i: (g, 0, 0)),
                  _resident((ATT_GROUPS, LANES, nc), lambda g, i: (g, 0, 0)),
                  _resident((ATT_GROUPS, s, LANES), lambda g, i: (g, 0, 0)),
                  _resident((ATT_GROUPS, V_ROWS, s), lambda g, i: (g, 0, 0)),
                  _resident((ATT_GROUPS, s, LANES), lambda g, i: (g, 0, 0)),
                  _resident((ATT_GROUPS, V_ROWS, s), lambda g, i: (g, 0, 0))],
        out_specs=pl.BlockSpec((rows, Q_BLK), lambda g, i: (g, i)),
        out_shape=jax.ShapeDtypeStruct((nqd, s), BF16),
        scratch_shapes=[pltpu.VMEM((ATT_GROUPS, nb, nq), F32),
                        pltpu.VMEM((ATT_GROUPS, PS_PAD + nc, Q_BLK), F32),
                        pltpu.VMEM((ATT_GROUPS, nc, nq), F32),
                        *[pltpu.VMEM((SEL_CHUNK, nq), BF16) for _ in range(2 * ATT_GROUPS)],
                        *[pltpu.VMEM((SEL_CHUNK, nq), BF16) for _ in range(2 * ATT_GROUPS)]],
        compiler_params=_params("arbitrary", "arbitrary"),
        name="nsa_attention",
    )(q_t, gate_t, cmp_mask, win_mask, kc, vct, ks, vst, kw, vwt)


def _vec(pre_g, mod, post_g):
    d = pre_g.shape[0]
    shift, scale, gate = mod[:d], mod[d:2 * d], mod[2 * d:3 * d]
    rows = [pre_g, scale, shift, post_g, gate]
    return jnp.stack(rows + [jnp.zeros((d,), F32)] * (SUBLANES - len(rows)))


def kernel(x, c, ada_w, ada_b, norm_pre_g, norm_post_g, a_w_in, a_ln_g, a_ln_b, a_ws, a_bs, a_w_out, kv_norm_g, kv_ada_w, kv_ada_b, w_kv, cmp_pos, cmp_w1, cmp_b1, cmp_w2, cmp_b2, b_w_in, b_w_out, ffn_w_gate, ffn_w_up, ffn_w_down, moe_router, moe_router_b, moe_w_gate, moe_w_up, moe_w_down):
    batch, s, d = x.shape
    assert batch == 1 and s % SEL_CHUNK == 0 and s >= WINDOW + Q_BLK
    depth = ada_w.shape[0]
    n_a = depth // 2
    xs = x.reshape(s, d)

    mods = _ada(c, ada_w.reshape(depth * 2, d, 3 * d), ada_b.reshape(depth * 2, 3 * d)).reshape(depth, 2, 3 * d)
    kv_mod = _ada(c, kv_ada_w.reshape(1, d, 2 * d), kv_ada_b.reshape(1, 2 * d))[0]
    shared = None
    n_moe, _, _, f_moe = moe_w_gate.shape
    assert n_moe <= n_a
    moe_gate_up = [None] * n_moe
    moe_down = [None] * n_moe
    gate_2d = moe_w_gate.reshape(n_moe * N_EXPERTS * d, f_moe)
    up_2d = moe_w_up.reshape(n_moe * N_EXPERTS * d, f_moe)
    down_2d = moe_w_down.reshape(n_moe * N_EXPERTS * f_moe, d)
    n_dense, _, f_dense = ffn_w_gate.shape
    assert n_dense <= n_a
    ffn_gate_up = [None] * n_dense
    ffn_gate_2d = ffn_w_gate.reshape(n_dense * d, f_dense)
    ffn_up_2d = ffn_w_up.reshape(n_dense * d, f_dense)

    for layer in range(depth):
        vec = _vec(norm_pre_g[layer, 0], mods[layer, 0], norm_post_g[layer, 0])
        attn = None
        if layer < n_a:
            cast = []
            if layer < n_moe:
                cast += [(gate_2d, layer, n_moe), (up_2d, layer, n_moe)]
            if layer < n_dense:
                cast += [(ffn_gate_2d, layer, n_dense), (ffn_up_2d, layer, n_dense)]
            xs, cast = _gmlp_layer(xs, vec, a_w_in[layer].astype(BF16),
                                   jnp.stack([a_ln_g[layer], a_ln_b[layer]]),
                                   a_ws[layer], a_bs[layer].T, a_w_out[layer].astype(BF16), cast)
            if layer < n_moe:
                moe_gate_up[layer] = [w.reshape(N_EXPERTS, d, f_moe) for w in cast[:2]]
                cast = cast[2:]
            if layer < n_dense:
                ffn_gate_up[layer] = cast
        else:
            if shared is None:
                kv_vec = _vec(kv_norm_g, jnp.concatenate([kv_mod, jnp.zeros((d,), F32)]), jnp.zeros((d,), F32))
                gd = N_KV * HEAD_DIM
                w_vt = jnp.concatenate([w_kv[:, 3 * gd:4 * gd], w_kv[:, 5 * gd:6 * gd]], axis=1).T
                raw, ks, kw, vst, vwt = _kvproj(xs, kv_vec, w_kv.astype(BF16), w_vt.astype(BF16))
                nc = s // D_CMP
                width = D_CMP * HEAD_DIM
                hid = cmp_w1.shape[-1]
                w2p = jnp.pad(cmp_w2, ((0, 0), (0, 0), (0, LANES - HEAD_DIM))).astype(BF16)
                b2p = jnp.pad(cmp_b2, ((0, 0), (0, LANES - HEAD_DIM))).reshape(2, 1, LANES)
                cmp_n, cmp_t = _compress(raw.reshape(2, N_KV, nc, width),
                                         cmp_pos.reshape(2, 2, 1, width),
                                         cmp_w1.reshape(2, 2, width, hid).astype(BF16),
                                         cmp_b1.reshape(2, 1, hid), w2p, b2p)
                shared = (cmp_n[0], cmp_t[1], ks, vst, kw, vwt)
            i = layer - n_a
            w_t = b_w_in[i].T.astype(BF16)
            q_t, gate_t = _qproj(xs, vec, w_t)
            kc, vct, ks, vst, kw, vwt = shared
            o_t = _attention(q_t, gate_t, kc, vct, ks, vst, kw, vwt)
            attn = (o_t, vec, b_w_out[i].astype(BF16))

        vec = _vec(norm_pre_g[layer, 1], mods[layer, 1], norm_post_g[layer, 1])
        j = layer // 2
        if layer % 2 == 0:
            cast = ((down_2d, j, n_moe),) if j < n_moe else ()
            xs, cast = _swiglu_layer(xs, vec, *ffn_gate_up[j], ffn_w_down[j].astype(BF16), attn, cast)
            if j < n_moe:
                moe_down[j] = cast[0].reshape(N_EXPERTS, f_moe, d)
        else:
            xs = _moe_layer(xs, vec, moe_router[j].T, moe_router_b[j].reshape(N_EXPERTS, 1),
                            *moe_gate_up[j], moe_down[j], attn)
    return xs.reshape(batch, s, d)
```
